```python
import jax, jax.numpy as jnp
from jax import lax
import numpy as np

D_MODEL = 1024
BATCH = 8
SEQ = 4096
DEPTH = 1

HEAD_DIM = 64
RET_HEADS = 8
NSA_Q_HEADS = 8
NSA_KV_HEADS = 2
NSA_GROUP = NSA_Q_HEADS // NSA_KV_HEADS
RET_WIDTH = RET_HEADS * HEAD_DIM
NSA_WIDTH = NSA_Q_HEADS * HEAD_DIM
MIX_WIDTH = RET_WIDTH + NSA_WIDTH
KV_WIDTH = NSA_KV_HEADS * HEAD_DIM
RET_CHUNK = 128
RET_THETA = 10000.0
ROPE_THETA = 500000.0
ROPE_DIM = HEAD_DIM // 4
CMP_BLOCK = 32
CMP_STRIDE = 16
CMP_HIDDEN = 256
SEL_BLOCK = 64
SEL_TOPK = 16
WINDOW = 512
NSA_QBLOCK = 64
N_BRANCH = 3
D_FF = 2816
CONV_WIDTH = 3
PLE_DIM = 256
EPS = 1e-6

IN_SIZES = [RET_WIDTH, RET_WIDTH, RET_WIDTH, RET_WIDTH,
            NSA_WIDTH,
            KV_WIDTH, KV_WIDTH, KV_WIDTH, KV_WIDTH, KV_WIDTH, KV_WIDTH,
            NSA_Q_HEADS * N_BRANCH]
IN_WIDTH = int(sum(IN_SIZES))
IN_SPLITS = [int(v) for v in np.cumsum(IN_SIZES)[:-1]]

kernel_name = 'hymba_retnet_nsa_convffn_ple'


def rms_norm(x, w):
    x32 = x.astype(jnp.float32)
    y = x32 * lax.rsqrt(jnp.mean(x32 * x32, axis=-1, keepdims=True) + EPS)
    return (y * w.astype(jnp.float32)).astype(x.dtype)


def rope(x, pos, rot_dim, theta):
    half = rot_dim // 2
    inv = jnp.power(jnp.float32(theta), -jnp.arange(half, dtype=jnp.float32) / half)
    ang = pos.astype(jnp.float32)[..., None] * inv
    cos = jnp.cos(ang)[:, :, None, :].astype(x.dtype)
    sin = jnp.sin(ang)[:, :, None, :].astype(x.dtype)
    x1, x2, rest = x[..., :half], x[..., half:rot_dim], x[..., rot_dim:]
    return jnp.concatenate([x1 * cos - x2 * sin, x1 * sin + x2 * cos, rest], axis=-1)


def masked_softmax(s, mask):
    s = jnp.where(mask, s.astype(jnp.float32), -jnp.inf)
    m = jnp.max(s, axis=-1, keepdims=True)
    m = jnp.where(jnp.isfinite(m), m, 0.0)
    e = jnp.exp(s - m)
    return e / jnp.maximum(jnp.sum(e, axis=-1, keepdims=True), 1e-30)


def retention(q, k, v):
    B, S, H, d = q.shape
    C = RET_CHUNK
    NC = S // C
    dt = q.dtype
    log_g = jnp.log1p(-jnp.exp2(-5.0 - jnp.arange(H, dtype=jnp.float32)))
    pos = jnp.arange(C, dtype=jnp.float32)
    diff = pos[:, None] - pos[None, :]
    decay = jnp.where(diff >= 0, jnp.exp(jnp.maximum(diff, 0.0) * log_g[:, None, None]), 0.0).astype(dt)
    q_decay = jnp.exp((pos + 1.0) * log_g[:, None]).astype(dt).T
    k_decay = jnp.exp((C - 1.0 - pos) * log_g[:, None]).astype(dt).T
    chunk_decay = jnp.exp(C * log_g).astype(dt)
    qc = q.reshape(B, NC, C, H, d)
    kc = k.reshape(B, NC, C, H, d)
    vc = v.reshape(B, NC, C, H, d)
    scores = jnp.einsum('bnchd,bnkhd->bhnck', qc, kc) * decay[None, :, None]
    o_intra = jnp.einsum('bhnck,bnkhd->bnchd', scores, vc)
    kv = jnp.einsum('bnkhd,bnkhe->nbhde', kc * k_decay[None, None, :, :, None], vc)

    def step(state, kv_n):
        return state * chunk_decay[None, :, None, None] + kv_n, state

    _, prev = lax.scan(step, jnp.zeros_like(kv[0]), kv)
    o_cross = jnp.einsum('bnchd,nbhde->bnche', qc * q_decay[None, None, :, :, None], prev)
    return (o_intra + o_cross).reshape(B, S, H, d)


def nsa(q, k_cmp, v_cmp, k_slc, v_slc, k_win, v_win, gates,
        cmp_pos, cmp_k_w1, cmp_k_w2, cmp_v_w1, cmp_v_w2):
    B, S, Hq, d = q.shape
    G, R = NSA_KV_HEADS, NSA_GROUP
    dt = q.dtype
    scale = d ** -0.5
    n_cmp = (S - CMP_BLOCK) // CMP_STRIDE + 1
    tok = jnp.arange(n_cmp)[:, None] * CMP_STRIDE + jnp.arange(CMP_BLOCK)[None, :]

    def compress(t, w1, w2):
        blocks = t[:, tok] + cmp_pos[:, None, :]
        flat = blocks.transpose(0, 3, 1, 2, 4).reshape(B, G, n_cmp, CMP_BLOCK * d)
        return jax.nn.gelu(flat @ w1) @ w2

    ck = compress(k_cmp, cmp_k_w1, cmp_k_w2)
    cv = compress(v_cmp, cmp_v_w1, cmp_v_w2)
    ci = jnp.arange(n_cmp)
    cmp_end = ci * CMP_STRIDE + CMP_BLOCK - 1
    n_sel = S // SEL_BLOCK
    top = min(SEL_TOPK, n_sel)
    ks_blocks = k_slc.transpose(0, 2, 1, 3).reshape(B, G, n_sel, SEL_BLOCK, d)
    vs_blocks = v_slc.transpose(0, 2, 1, 3).reshape(B, G, n_sel, SEL_BLOCK, d)
    sj = jnp.arange(n_sel)
    overlap = ((ci[:, None] * CMP_STRIDE < (sj[None, :] + 1) * SEL_BLOCK)
               & (ci[:, None] * CMP_STRIDE + CMP_BLOCK > sj[None, :] * SEL_BLOCK)).astype(jnp.float32)
    pad = ((0, 0), (0, 0), (WINDOW, 0), (0, 0))
    kw = jnp.pad(k_win.transpose(0, 2, 1, 3), pad)
    vw = jnp.pad(v_win.transpose(0, 2, 1, 3), pad)
    QB = NSA_QBLOCK
    nqb = S // QB
    qb = q.reshape(B, nqb, QB, G, R, d).transpose(1, 0, 3, 4, 2, 5)
    bi = jnp.arange(B)[:, None, None, None]
    gi = jnp.arange(G)[None, :, None, None]

    def block(args):
        qi, i = args
        t0 = i * QB
        t = t0 + jnp.arange(QB)
        s = jnp.einsum('bgrqd,bgnd->bgrqn', qi, ck) * scale
        p_c = masked_softmax(s, cmp_end[None, :] <= t[:, None])
        o_c = jnp.einsum('bgrqn,bgnd->bgrqd', p_c.astype(dt), cv)
        imp = jnp.einsum('bgrqn,ns->bgqs', p_c, overlap)
        cur = t // SEL_BLOCK
        valid = sj[None, :] * SEL_BLOCK <= t[:, None]
        forced = (sj[None, :] == 0) | (sj[None, :] == cur[:, None]) | (sj[None, :] == cur[:, None] - 1)
        score = jnp.where(forced, jnp.inf, jnp.where(valid, imp, -jnp.inf))
        _, idx = lax.top_k(score, top)
        ksel = ks_blocks[bi, gi, idx].reshape(B, G, QB, top * SEL_BLOCK, d)
        vsel = vs_blocks[bi, gi, idx].reshape(B, G, QB, top * SEL_BLOCK, d)
        kpos = (idx[..., None] * SEL_BLOCK + jnp.arange(SEL_BLOCK)).reshape(B, G, QB, top * SEL_BLOCK)
        smask = kpos <= t[:, None]
        s = jnp.einsum('bgrqd,bgqkd->bgrqk', qi, ksel) * scale
        p_s = masked_softmax(s, smask[:, :, None])
        o_s = jnp.einsum('bgrqk,bgqkd->bgrqd', p_s.astype(dt), vsel)
        kwi = lax.dynamic_slice_in_dim(kw, t0, QB + WINDOW, axis=2)
        vwi = lax.dynamic_slice_in_dim(vw, t0, QB + WINDOW, axis=2)
        wpos = t0 - WINDOW + jnp.arange(QB + WINDOW)
        wmask = (wpos[None, :] >= 0) & (wpos[None, :] <= t[:, None]) & (wpos[None, :] > t[:, None] - WINDOW)
        s = jnp.einsum('bgrqd,bgkd->bgrqk', qi, kwi) * scale
        p_w = masked_softmax(s, wmask)
        o_w = jnp.einsum('bgrqk,bgkd->bgrqd', p_w.astype(dt), vwi)
        return o_c, o_s, o_w

    o_c, o_s, o_w = lax.map(block, (qb, jnp.arange(nqb)))
    unblock = lambda o: o.transpose(1, 0, 4, 2, 3, 5).reshape(B, S, Hq, d)
    o = (gates[..., 0:1] * unblock(o_c) + gates[..., 1:2] * unblock(o_s)
         + gates[..., 2:3] * unblock(o_w))
    return o.reshape(B, S, Hq * d)


def causal_dwconv(u, w, b):
    K = w.shape[0]
    S = u.shape[1]
    up = jnp.pad(u, ((0, 0), (K - 1, 0), (0, 0)))
    y = b
    for j in range(K):
        y = y + w[j] * up[:, j:j + S]
    return y


def setup_inputs(seed: int = 0) -> dict:
    key = jax.random.key(seed)
    ks = jax.random.split(key, 24)
    nrm = lambda k, shape, s: jax.random.normal(k, shape, jnp.float32) * s
    L = DEPTH
    inp = {}
    inp['x'] = nrm(ks[0], (BATCH, SEQ, D_MODEL), 1.0)
    inp['p'] = nrm(ks[1], (DEPTH, BATCH, SEQ, PLE_DIM), 1.0)
    offset = jax.random.randint(ks[2], (BATCH, 1), 0, 4096, dtype=jnp.int32)
    inp['positions'] = offset + jnp.arange(SEQ, dtype=jnp.int32)[None, :]
    inp['norm_mix_w'] = 1.0 + nrm(ks[3], (L, D_MODEL), 0.02)
    inp['w_in'] = nrm(ks[4], (L, D_MODEL, IN_WIDTH), D_MODEL ** -0.5)
    inp['ret_gn_w'] = 1.0 + nrm(ks[5], (L, RET_WIDTH), 0.02)
    inp['cmp_pos'] = nrm(ks[6], (L, CMP_BLOCK, HEAD_DIM), 0.02)
    inp['cmp_k_w1'] = nrm(ks[7], (L, CMP_BLOCK * HEAD_DIM, CMP_HIDDEN), (CMP_BLOCK * HEAD_DIM) ** -0.5)
    inp['cmp_k_w2'] = nrm(ks[8], (L, CMP_HIDDEN, HEAD_DIM), CMP_HIDDEN ** -0.5)
    inp['cmp_v_w1'] = nrm(ks[9], (L, CMP_BLOCK * HEAD_DIM, CMP_HIDDEN), (CMP_BLOCK * HEAD_DIM) ** -0.5)
    inp['cmp_v_w2'] = nrm(ks[10], (L, CMP_HIDDEN, HEAD_DIM), CMP_HIDDEN ** -0.5)
    inp['w_out'] = nrm(ks[11], (L, MIX_WIDTH, D_MODEL), MIX_WIDTH ** -0.5)
    inp['norm_ffn_w'] = 1.0 + nrm(ks[12], (L, D_MODEL), 0.02)
    inp['ffn_w_up'] = nrm(ks[13], (L, D_MODEL, 2 * D_FF), D_MODEL ** -0.5)
    inp['ffn_conv_w'] = nrm(ks[14], (L, CONV_WIDTH, D_FF), CONV_WIDTH ** -0.5)
    inp['ffn_conv_b'] = nrm(ks[15], (L, D_FF), 0.01)
    inp['ffn_w_down'] = nrm(ks[16], (L, D_FF, D_MODEL), D_FF ** -0.5)
    inp['ple_w'] = nrm(ks[17], (L, PLE_DIM, D_MODEL), PLE_DIM ** -0.5)
    inp['ple_gate_w'] = nrm(ks[18], (L, D_MODEL, D_MODEL), D_MODEL ** -0.5)
    inp['final_norm_w'] = 1.0 + nrm(ks[19], (D_MODEL,), 0.02)
    return inp


def reference(x, p, positions, norm_mix_w, w_in, ret_gn_w, cmp_pos, cmp_k_w1, cmp_k_w2,
              cmp_v_w1, cmp_v_w2, w_out, norm_ffn_w, ffn_w_up, ffn_conv_w, ffn_conv_b,
              ffn_w_down, ple_w, ple_gate_w, final_norm_w):
    B, S, D = x.shape
    h = x
    for i in range(DEPTH):
        xn = rms_norm(h, norm_mix_w[i])
        proj = xn @ w_in[i]
        (rq, rk, rv, rg, nq, kc, vc, ksl, vsl, kwn, vwn, ng) = jnp.split(proj, IN_SPLITS, axis=-1)
        heads = lambda t, n: t.reshape(B, S, n, HEAD_DIM)
        rq = rope(heads(rq, RET_HEADS), positions, HEAD_DIM, RET_THETA)
        rk = rope(heads(rk, RET_HEADS), positions, HEAD_DIM, RET_THETA) * (HEAD_DIM ** -0.5)
        ro = retention(rq, rk, heads(rv, RET_HEADS)).astype(jnp.float32)
        mu = jnp.mean(ro, axis=-1, keepdims=True)
        var = jnp.mean(jnp.square(ro - mu), axis=-1, keepdims=True)
        ro = ((ro - mu) * lax.rsqrt(var + EPS)).reshape(B, S, RET_WIDTH) * ret_gn_w[i].astype(jnp.float32)
        ret_out = ro.astype(h.dtype) * jax.nn.silu(rg)
        prope = lambda t, n: rope(heads(t, n), positions, ROPE_DIM, ROPE_THETA)
        nsa_out = nsa(prope(nq, NSA_Q_HEADS),
                      prope(kc, NSA_KV_HEADS), heads(vc, NSA_KV_HEADS),
                      prope(ksl, NSA_KV_HEADS), heads(vsl, NSA_KV_HEADS),
                      prope(kwn, NSA_KV_HEADS), heads(vwn, NSA_KV_HEADS),
                      jax.nn.sigmoid(ng.reshape(B, S, NSA_Q_HEADS, N_BRANCH)),
                      cmp_pos[i], cmp_k_w1[i], cmp_k_w2[i], cmp_v_w1[i], cmp_v_w2[i])
        h = h + jnp.concatenate([ret_out, nsa_out], axis=-1) @ w_out[i]
        hn = rms_norm(h, norm_ffn_w[i])
        gate_pre, val = jnp.split(hn @ ffn_w_up[i], 2, axis=-1)
        gate = causal_dwconv(gate_pre, ffn_conv_w[i], ffn_conv_b[i])
        h = h + (jax.nn.silu(gate) * val) @ ffn_w_down[i]
        h = h + jax.nn.sigmoid(h @ ple_gate_w[i]) * (p[i] @ ple_w[i])
    return rms_norm(h, final_norm_w)
```

```python
import functools

import numpy as np
import jax
import jax.numpy as jnp
from jax import lax
from jax.experimental import pallas as pl
from jax.experimental.pallas import tpu as pltpu

F32 = jnp.float32
BF16 = jnp.bfloat16

LANES = 128
HEAD_DIM = 64
HEADS_PER_VREG = LANES // HEAD_DIM
RET_HEADS = 8
NSA_Q_HEADS = 8
NSA_KV_HEADS = 2
NSA_GROUP = NSA_Q_HEADS // NSA_KV_HEADS
RET_WIDTH = RET_HEADS * HEAD_DIM
NSA_WIDTH = NSA_Q_HEADS * HEAD_DIM
KV_WIDTH = NSA_KV_HEADS * HEAD_DIM
RET_CHUNK = 128
RET_THETA = 10000.0
ROPE_THETA = 500000.0
ROPE_DIM = HEAD_DIM // 4
CMP_BLOCK = 32
CMP_STRIDE = 16
SEL_BLOCK = 64
SEL_TOPK = 16
WINDOW = 512
N_BRANCH = 3
EPS = 1e-6
QK_SCALE = HEAD_DIM ** -0.5
MASKED = -1e30

VMEM_LIMIT = 56 * 1024 * 1024


def _dot(a, b):
    return jnp.dot(a, b, preferred_element_type=F32)


def _dot_nt(a, b):
    return lax.dot_general(a, b, (((1,), (1,)), ((), ())), preferred_element_type=F32)


def _dot_tn(a, b):
    return lax.dot_general(a, b, (((0,), (0,)), ((), ())), preferred_element_type=F32)


def _params(*semantics):
    return pltpu.CompilerParams(dimension_semantics=semantics, vmem_limit_bytes=VMEM_LIMIT)


def _row_tile(n, pref):
    t = pref
    while n % t:
        t //= 2
    return t


def _rope(y, cos, sin_signed, first_half, half):
    nxt = pltpu.roll(y, LANES - half, 1)
    prv = pltpu.roll(y, half, 1)
    return y * cos + jnp.where(first_half, nxt, prv) * sin_signed


def _dup_heads(y, low):
    sw = pltpu.roll(y, HEAD_DIM, 1)
    return jnp.where(low, y, sw), jnp.where(low, sw, y)


def _in_proj_kernel(x_ref, pos_ref, nw_ref, w_ref, tab_ref,
                    ret_ref, rg_ref, nq_ref, cmp_ref, kv_ref, gate_ref):
    x = x_ref[...]
    tm = x.shape[0]
    ms = jnp.mean(x * x, axis=-1, keepdims=True)
    xn = (x * lax.rsqrt(ms + EPS) * nw_ref[...]).astype(BF16)

    pos = pos_ref[...].astype(F32)
    tab = tab_ref[...]
    lane = lax.broadcasted_iota(jnp.int32, (tm, LANES), 1)
    in_head = lane % HEAD_DIM
    low = lane < HEAD_DIM
    ang_r = pos * tab[0:1, :]
    cos_r, sin_r = jnp.cos(ang_r), jnp.sin(ang_r) * tab[1:2, :]
    first_r = in_head < HEAD_DIM // 2
    ang_n = pos * tab[2:3, :]
    cos_n, sin_n = jnp.cos(ang_n), jnp.sin(ang_n) * tab[3:4, :]
    first_n = in_head < ROPE_DIM // 2

    rope_r = lambda y: _rope(y, cos_r, sin_r, first_r, HEAD_DIM // 2)
    rope_n = lambda y: _rope(y, cos_n, sin_n, first_n, ROPE_DIM // 2)

    def halves(ci):
        y = _dot(xn, w_ref[:, ci * 256:(ci + 1) * 256])
        return y[:, :LANES], y[:, LANES:]

    for ci in range(2):
        for h, y in enumerate(halves(ci)):
            c0 = ci * 256 + h * LANES
            ret_ref[:, c0:c0 + LANES] = rope_r(y).astype(BF16)
    for ci in range(2):
        for h, y in enumerate(halves(2 + ci)):
            c0 = RET_WIDTH + ci * 256 + h * LANES
            ret_ref[:, c0:c0 + LANES] = (rope_r(y) * QK_SCALE).astype(BF16)
    for ci in range(2):
        for h, y in enumerate(halves(4 + ci)):
            c0 = 2 * RET_WIDTH + ci * 256 + h * LANES
            ret_ref[:, c0:c0 + LANES] = y.astype(BF16)
    for ci in range(2):
        for h, y in enumerate(halves(6 + ci)):
            c0 = ci * 256 + h * LANES
            rg_ref[:, c0:c0 + LANES] = y
    for ci in range(2):
        for h, y in enumerate(halves(8 + ci)):
            c0 = ci * 256 + h * LANES
            nq_ref[:, c0:c0 + LANES] = (rope_n(y) * QK_SCALE).astype(BF16)
    kc, vc = halves(10)
    cmp_ref[:, :LANES] = rope_n(kc).astype(BF16)
    cmp_ref[:, LANES:] = vc.astype(BF16)
    for ci in range(2):
        k, v = halves(11 + ci)
        k0, k1 = _dup_heads(rope_n(k), low)
        v0, v1 = _dup_heads(v, low)
        for n, y in enumerate((k0, k1, v0, v1)):
            c0 = ci * 4 * LANES + n * LANES
            kv_ref[:, c0:c0 + LANES] = y.astype(BF16)
    g = _dot(xn, w_ref[:, 13 * 256:13 * 256 + LANES])
    gate_ref[...] = jax.nn.sigmoid(g)


def _in_proj(x2, pos2, norm_w, w_pad, tab, tm):
    T, D = x2.shape
    n_in = w_pad.shape[1]
    row = lambda w: pl.BlockSpec((tm, w), lambda i: (i, 0))
    full = lambda a: pl.BlockSpec(a.shape, lambda i: (0,) * a.ndim)
    return pl.pallas_call(
        _in_proj_kernel,
        grid=(T // tm,),
        in_specs=[row(D), row(1), full(norm_w), full(w_pad), full(tab)],
        out_specs=[row(3 * RET_WIDTH), row(RET_WIDTH), row(NSA_WIDTH), row(2 * KV_WIDTH),
                   row(8 * LANES), row(LANES)],
        out_shape=[jax.ShapeDtypeStruct((T, 3 * RET_WIDTH), BF16),
                   jax.ShapeDtypeStruct((T, RET_WIDTH), F32),
                   jax.ShapeDtypeStruct((T, NSA_WIDTH), BF16),
                   jax.ShapeDtypeStruct((T, 2 * KV_WIDTH), BF16),
                   jax.ShapeDtypeStruct((T, 8 * LANES), BF16),
                   jax.ShapeDtypeStruct((T, LANES), F32)],
        compiler_params=_params("parallel"),
    )(x2, pos2, norm_w, w_pad, tab)


def _retention_kernel(q_ref, k_ref, v_ref, rg_ref, gnw_ref, dec_ref, qd_ref, kd_ref, cdm_ref,
                      o_ref, state_ref, *, n_chunks):
    C = RET_CHUNK
    lane = lax.broadcasted_iota(jnp.int32, (C, LANES), 1)
    low = lane < HEAD_DIM
    r_i = lax.broadcasted_iota(jnp.int32, (LANES, LANES), 0)
    c_i = lax.broadcasted_iota(jnp.int32, (LANES, LANES), 1)
    same_head = (r_i < HEAD_DIM) == (c_i < HEAD_DIM)
    state_ref[...] = jnp.zeros_like(state_ref)

    def chunk(n, carry):
        r0 = pl.multiple_of(n * C, C)
        q = q_ref[pl.ds(r0, C), :]
        k = k_ref[pl.ds(r0, C), :]
        v = v_ref[pl.ds(r0, C), :]
        zero = jnp.zeros_like(q)
        s0 = _dot_nt(jnp.where(low, q, zero), k) * dec_ref[0]
        s1 = _dot_nt(jnp.where(low, zero, q), k) * dec_ref[1]
        o_intra = jnp.where(low, _dot(s0.astype(BF16), v), _dot(s1.astype(BF16), v))
        state = state_ref[...]
        qd = (q.astype(F32) * qd_ref[...]).astype(BF16)
        o_cross = _dot(qd, state.astype(BF16))
        kd = (k.astype(F32) * kd_ref[...]).astype(BF16)
        kv = _dot_tn(kd, v)
        state_ref[...] = state * cdm_ref[0] + jnp.where(same_head, kv, 0.0)

        ro = o_intra + o_cross
        inv_d = 1.0 / HEAD_DIM
        sum0 = jnp.sum(jnp.where(low, ro, 0.0), axis=-1, keepdims=True)
        sum1 = jnp.sum(jnp.where(low, 0.0, ro), axis=-1, keepdims=True)
        d = ro - jnp.where(low, sum0, sum1) * inv_d
        dd = d * d
        var0 = jnp.sum(jnp.where(low, dd, 0.0), axis=-1, keepdims=True)
        var1 = jnp.sum(jnp.where(low, 0.0, dd), axis=-1, keepdims=True)
        var = jnp.where(low, var0, var1) * inv_d
        y = d * lax.rsqrt(var + EPS) * gnw_ref[...]
        g = rg_ref[pl.ds(r0, C), :]
        o_ref[pl.ds(r0, C), :] = (y * (g * jax.nn.sigmoid(g))).astype(BF16)
        return carry

    lax.fori_loop(0, n_chunks, chunk, 0)


def _retention(ret_qkv, rg, gn_w, dec, qd, kd, cdm, B, S):
    T = B * S
    n_pairs = RET_HEADS // HEADS_PER_VREG
    seq = lambda off: pl.BlockSpec((S, LANES), lambda b, hp: (b, off + hp))
    return pl.pallas_call(
        functools.partial(_retention_kernel, n_chunks=S // RET_CHUNK),
        grid=(B, n_pairs),
        in_specs=[seq(0), seq(n_pairs), seq(2 * n_pairs), seq(0),
                  pl.BlockSpec((1, LANES), lambda b, hp: (0, hp)),
                  pl.BlockSpec((HEADS_PER_VREG, RET_CHUNK, RET_CHUNK), lambda b, hp: (hp, 0, 0)),
                  pl.BlockSpec((RET_CHUNK, LANES), lambda b, hp: (0, hp)),
                  pl.BlockSpec((RET_CHUNK, LANES), lambda b, hp: (0, hp)),
                  pl.BlockSpec((1, LANES, LANES), lambda b, hp: (hp, 0, 0))],
        out_specs=seq(0),
        out_shape=jax.ShapeDtypeStruct((T, RET_WIDTH), BF16),
        scratch_shapes=[pltpu.VMEM((LANES, LANES), F32)],
        compiler_params=_params("parallel", "parallel"),
    )(ret_qkv, ret_qkv, ret_qkv, rg, gn_w, dec, qd, kd, cdm)


def _retention_constants():
    H, C = RET_HEADS, RET_CHUNK
    log_g = jnp.log1p(-jnp.exp2(-5.0 - jnp.arange(H, dtype=F32)))
    pos = jnp.arange(C, dtype=F32)
    diff = pos[:, None] - pos[None, :]
    dec = jnp.where(diff >= 0, jnp.exp(jnp.maximum(diff, 0.0) * log_g[:, None, None]), 0.0)
    q_decay = jnp.exp((pos + 1.0) * log_g[:, None]).T
    k_decay = jnp.exp((C - 1.0 - pos) * log_g[:, None]).T
    chunk_decay = jnp.exp(C * log_g)
    qd = jnp.repeat(q_decay, HEAD_DIM, axis=1)
    kd = jnp.repeat(k_decay, HEAD_DIM, axis=1)
    cd_rows = jnp.repeat(chunk_decay, HEAD_DIM).reshape(H // HEADS_PER_VREG, LANES, 1)
    head_of = jnp.arange(LANES) // HEAD_DIM
    same = (head_of[:, None] == head_of[None, :]).astype(F32)
    return dec, qd, kd, cd_rows * same[None]


def _compress_kernel(t_ref, pos_ref, w1_ref, w2_ref, o_ref):
    t = t_ref[0, 0, 0]
    half = t.shape[1]
    w1a = w1_ref[0, :half, :]
    w1b = w1_ref[0, half:, :]
    pos = jnp.broadcast_to(pos_ref[0], (8, 2 * half)).astype(BF16)
    bias = _dot(pos, w1_ref[0])[0:1, :]
    first = _dot(t, w1a)
    second = _dot(t, w1b)
    n = t.shape[0]
    hidden = first + pltpu.roll(second, n - 1, 0) + bias
    act = jax.nn.gelu(hidden).astype(BF16)
    o_ref[0, 0, 0] = _dot(act, w2_ref[0]).astype(BF16)


def _compress(cmp_tok, pos_flat, w1, w2x):
    B, _, G, n, width = cmp_tok.shape
    hid = w1.shape[2]
    return pl.pallas_call(
        _compress_kernel,
        grid=(B, 2, G),
        in_specs=[pl.BlockSpec((1, 1, 1, n, width), lambda b, a, g: (b, a, g, 0, 0)),
                  pl.BlockSpec((1, 1, 2 * width), lambda b, a, g: (0, 0, 0)),
                  pl.BlockSpec((1, 2 * width, hid), lambda b, a, g: (a, 0, 0)),
                  pl.BlockSpec((1, hid, LANES), lambda b, a, g: (a, 0, 0))],
        out_specs=pl.BlockSpec((1, 1, 1, n, LANES), lambda b, a, g: (b, a, g, 0, 0)),
        out_shape=jax.ShapeDtypeStruct((B, 2, G, n, LANES), BF16),
        compiler_params=_params("parallel", "parallel", "parallel"),
    )(cmp_tok, pos_flat, w1, w2x)


def _nsa_kernel(q_ref, ck_ref, cv_ref, ks_ref, vs_ref, kw_ref, vw_ref, gate_ref,
                ovt_ref, exp_ref, gx_ref, o_ref, *, tq, kc, seq, top):
    R = NSA_GROUP
    t0 = pl.program_id(2) * tq
    lane = lax.broadcasted_iota(jnp.int32, (tq, LANES), 1)
    low = lane < HEAD_DIM
    trow = t0 + lax.broadcasted_iota(jnp.int32, (tq, 1), 0)

    q = q_ref[...]
    zero = jnp.zeros((tq, LANES), BF16)
    qm = []
    for r in range(R):
        blk = q[:, (r // 2) * LANES:(r // 2 + 1) * LANES]
        qm.append(jnp.where(low, blk, zero) if r % 2 == 0 else jnp.where(low, zero, blk))

    def merge(parts):
        return jnp.concatenate([jnp.where(low, parts[0], parts[1]),
                                jnp.where(low, parts[2], parts[3])], axis=1)

    ck = ck_ref[0, 0, 0]
    cv = cv_ref[0, 0, 0]
    ncp = ck.shape[0]
    cmp_end = lax.broadcasted_iota(jnp.int32, (1, ncp), 1) * CMP_STRIDE + (CMP_BLOCK - 1)
    cmask = cmp_end <= trow
    o_c = []
    p_sum = jnp.zeros((tq, ncp), F32)
    for r in range(R):
        s = jnp.where(cmask, _dot_nt(qm[r], ck), -jnp.inf)
        m = jnp.max(s, axis=-1, keepdims=True)
        m = jnp.where(jnp.isfinite(m), m, 0.0)
        e = jnp.exp(s - m)
        p = e * (1.0 / jnp.maximum(jnp.sum(e, axis=-1, keepdims=True), 1e-30))
        o_c.append(_dot(p.astype(BF16), cv))
        p_sum = p_sum + p

    imp_t = _dot_nt(ovt_ref[...], p_sum.astype(BF16))
    n_sel = imp_t.shape[0]
    blk_i = lax.broadcasted_iota(jnp.int32, (n_sel, tq), 0)
    tok_i = t0 + lax.broadcasted_iota(jnp.int32, (n_sel, tq), 1)
    cur = tok_i // SEL_BLOCK
    valid = blk_i * SEL_BLOCK <= tok_i
    forced = (blk_i == 0) | (blk_i == cur) | (blk_i == cur - 1)
    score = jnp.where(forced, jnp.inf, jnp.where(valid, imp_t, -jnp.inf))
    beaten_by = jnp.zeros((n_sel, tq), F32)
    for kb in range(n_sel):
        other = score[kb:kb + 1, :]
        ge = jnp.where(other >= score, 1.0, 0.0)
        gt = jnp.where(other > score, 1.0, 0.0)
        beaten_by = beaten_by + jnp.where(blk_i > kb, ge, gt)
    sel_t = jnp.where(beaten_by < top, 1.0, 0.0).astype(BF16)
    eye = (lax.broadcasted_iota(jnp.int32, (tq, tq), 0)
           == lax.broadcasted_iota(jnp.int32, (tq, tq), 1))
    sel = _dot_nt(jnp.where(eye, 1.0, 0.0).astype(BF16), sel_t).astype(BF16)

    def sel_chunk(c, carry):
        ms, ls, accs = carry
        k0 = pl.multiple_of(c * kc, kc)
        kblk = ks_ref[0, pl.ds(k0, kc), :]
        vblk = vs_ref[0, pl.ds(k0, kc), :]
        picked = _dot(sel, exp_ref[:, pl.ds(k0, kc)])
        kpos = k0 + lax.broadcasted_iota(jnp.int32, (1, kc), 1)
        mask = (picked > 0.5) & (kpos <= trow)
        new_ms, new_ls, new_accs = [], [], []
        for r in range(R):
            s = jnp.where(mask, _dot_nt(qm[r], kblk), MASKED)
            m_new = jnp.maximum(ms[r], jnp.max(s, axis=-1, keepdims=True))
            alpha = jnp.exp(ms[r] - m_new)
            p = jnp.exp(s - m_new)
            new_ls.append(alpha * ls[r] + jnp.sum(p, axis=-1, keepdims=True))
            new_accs.append(alpha * accs[r] + _dot(p.astype(BF16), vblk))
            new_ms.append(m_new)
        return tuple(new_ms), tuple(new_ls), tuple(new_accs)

    init = (tuple(jnp.full((tq, 1), MASKED, F32) for _ in range(R)),
            tuple(jnp.zeros((tq, 1), F32) for _ in range(R)),
            tuple(jnp.zeros((tq, LANES), F32) for _ in range(R)))
    n_chunks = (t0 + tq + kc - 1) // kc
    _, ls, accs = lax.fori_loop(0, n_chunks, sel_chunk, init)
    o_s = [accs[r] * (1.0 / ls[r]) for r in range(R)]

    wlen = WINDOW + tq
    ws = pl.multiple_of(jnp.maximum(t0 - WINDOW, 0), tq)
    kwin = kw_ref[0, pl.ds(ws, wlen), :]
    vwin = vw_ref[0, pl.ds(ws, wlen), :]
    wpos = ws + lax.broadcasted_iota(jnp.int32, (1, wlen), 1)
    wmask = (wpos <= trow) & (wpos > trow - WINDOW)
    o_w = []
    for r in range(R):
        s = jnp.where(wmask, _dot_nt(qm[r], kwin), MASKED)
        p = jnp.exp(s - jnp.max(s, axis=-1, keepdims=True))
        l = jnp.sum(p, axis=-1, keepdims=True)
        o_w.append(_dot(p.astype(BF16), vwin) * (1.0 / l))

    gs = gate_ref[...]
    g_hi = gs.astype(BF16)
    g_lo = (gs - g_hi.astype(F32)).astype(BF16)
    out = jnp.zeros((tq, R * HEAD_DIM), F32)
    for br, parts in enumerate((o_c, o_s, o_w)):
        spread = gx_ref[0, br]
        out = out + (_dot(g_hi, spread) + _dot(g_lo, spread)) * merge(parts)
    o_ref[...] = out.astype(BF16)


def _nsa(nq, cmp_out, kv, gates, ovt, expand, gate_spread, B, S, tq, kc):
    T = B * S
    G = NSA_KV_HEADS
    nqt = S // tq
    ncp = cmp_out.shape[3]
    n_sel = S // SEL_BLOCK
    kv3 = kv.reshape(B, S, kv.shape[1])
    seq = lambda off: pl.BlockSpec((1, S, LANES), lambda b, g, i: (b, 0, off + g))
    cmp_spec = lambda a: pl.BlockSpec((1, 1, 1, ncp, LANES), lambda b, g, i: (b, a, g, 0, 0))
    return pl.pallas_call(
        functools.partial(_nsa_kernel, tq=tq, kc=kc, seq=S, top=min(SEL_TOPK, n_sel)),
        grid=(B, G, nqt),
        in_specs=[pl.BlockSpec((tq, NSA_GROUP * HEAD_DIM), lambda b, g, i: (b * nqt + i, g)),
                  cmp_spec(0), cmp_spec(1),
                  seq(0), seq(2), seq(4), seq(6),
                  pl.BlockSpec((tq, LANES), lambda b, g, i: (b * nqt + i, 0)),
                  pl.BlockSpec(ovt.shape, lambda b, g, i: (0, 0)),
                  pl.BlockSpec(expand.shape, lambda b, g, i: (0, 0)),
                  pl.BlockSpec((1, N_BRANCH, LANES, NSA_GROUP * HEAD_DIM),
                               lambda b, g, i: (g, 0, 0, 0))],
        out_specs=pl.BlockSpec((tq, NSA_GROUP * HEAD_DIM), lambda b, g, i: (b * nqt + i, g)),
        out_shape=jax.ShapeDtypeStruct((T, NSA_WIDTH), BF16),
        compiler_params=_params("parallel", "parallel", "parallel"),
    )(nq, cmp_out, cmp_out, kv3, kv3, kv3, kv3, gates, ovt, expand, gate_spread)


def _nsa_constants(S):
    n_cmp = (S - CMP_BLOCK) // CMP_STRIDE + 1
    ncp = S // CMP_STRIDE
    n_sel = S // SEL_BLOCK
    ci = np.arange(ncp)[None, :]
    sj = np.arange(n_sel)[:, None]
    ovt = ((ci * CMP_STRIDE < (sj + 1) * SEL_BLOCK)
           & (ci * CMP_STRIDE + CMP_BLOCK > sj * SEL_BLOCK) & (ci < n_cmp))
    expand = (np.arange(S)[None, :] // SEL_BLOCK) == sj
    spread = np.zeros((NSA_KV_HEADS, N_BRANCH, LANES, NSA_GROUP * HEAD_DIM), np.float32)
    for g in range(NSA_KV_HEADS):
        for br in range(N_BRANCH):
            for r in range(NSA_GROUP):
                col = (g * NSA_GROUP + r) * N_BRANCH + br
                spread[g, br, col, r * HEAD_DIM:(r + 1) * HEAD_DIM] = 1.0
    return (jnp.asarray(ovt, BF16), jnp.asarray(expand, BF16), jnp.asarray(spread, BF16))


def _out_proj_kernel(ret_ref, nsa_ref, x_ref, wo_ref, nw_ref, h_ref, hn_ref):
    mix = _dot(ret_ref[...], wo_ref[:RET_WIDTH, :]) + _dot(nsa_ref[...], wo_ref[RET_WIDTH:, :])
    h = x_ref[...] + mix
    h_ref[...] = h
    ms = jnp.mean(h * h, axis=-1, keepdims=True)
    hn_ref[...] = (h * lax.rsqrt(ms + EPS) * nw_ref[...]).astype(BF16)


def _out_proj(ret_out, nsa_out, x2, w_out, norm_w, tm):
    T, D = x2.shape
    row = lambda w: pl.BlockSpec((tm, w), lambda i: (i, 0))
    full = lambda a: pl.BlockSpec(a.shape, lambda i: (0,) * a.ndim)
    return pl.pallas_call(
        _out_proj_kernel,
        grid=(T // tm,),
        in_specs=[row(RET_WIDTH), row(NSA_WIDTH), row(D), full(w_out), full(norm_w)],
        out_specs=[row(D), row(D)],
        out_shape=[jax.ShapeDtypeStruct((T, D), F32), jax.ShapeDtypeStruct((T, D), BF16)],
        compiler_params=_params("parallel"),
    )(ret_out, nsa_out, x2, w_out, norm_w)


def _ffn_up_kernel(hn_ref, wg_ref, wv_ref, cw_ref, cb_ref, act_ref, tail_ref, *, fc):
    @pl.when(pl.program_id(1) == 0)
    def _():
        tail_ref[...] = jnp.zeros_like(tail_ref)

    hn = hn_ref[...]
    tm = hn.shape[0]
    row = lax.broadcasted_iota(jnp.int32, (tm, fc), 0)
    for c in range(wg_ref.shape[1] // fc):
        sl = slice(c * fc, (c + 1) * fc)
        g = _dot(hn, wg_ref[:, sl])
        val = _dot(hn, wv_ref[:, sl])
        tail = tail_ref[:, sl]
        g1 = jnp.where(row == 0, tail[7:8, :], pltpu.roll(g, 1, 0))
        g2 = jnp.where(row == 0, tail[6:7, :],
                       jnp.where(row == 1, tail[7:8, :], pltpu.roll(g, 2, 0)))
        tail_ref[:, sl] = g[tm - 8:, :]
        cw = cw_ref[:, sl]
        y = cb_ref[:, sl] + cw[0:1, :] * g2 + cw[1:2, :] * g1 + cw[2:3, :] * g
        act_ref[:, sl] = (y * jax.nn.sigmoid(y) * val).astype(BF16)


def _ffn_up(hn, wg, wv, conv_w, conv_b, B, S, tm):
    T, D = hn.shape
    F = wg.shape[1]
    nt = S // tm
    full = lambda a: pl.BlockSpec(a.shape, lambda b, j: (0,) * a.ndim)
    return pl.pallas_call(
        functools.partial(_ffn_up_kernel, fc=256),
        grid=(B, nt),
        in_specs=[pl.BlockSpec((tm, D), lambda b, j: (b * nt + j, 0)),
                  full(wg), full(wv), full(conv_w), full(conv_b)],
        out_specs=pl.BlockSpec((tm, F), lambda b, j: (b * nt + j, 0)),
        out_shape=jax.ShapeDtypeStruct((T, F), BF16),
        scratch_shapes=[pltpu.VMEM((8, F), F32)],
        compiler_params=_params("arbitrary", "arbitrary"),
    )(hn, wg, wv, conv_w, conv_b)


def _ffn_down_kernel(act_ref, h_ref, p_ref, wd_ref, pg_ref, pw_ref, fw_ref, o_ref, *, final):
    h = h_ref[...] + _dot(act_ref[...], wd_ref[...])
    gate = jax.nn.sigmoid(_dot(h.astype(BF16), pg_ref[...]))
    h = h + gate * _dot(p_ref[...].astype(BF16), pw_ref[...])
    if final:
        ms = jnp.mean(h * h, axis=-1, keepdims=True)
        h = h * lax.rsqrt(ms + EPS) * fw_ref[...]
    o_ref[...] = h


def _ffn_down(act, h1, p2, wd, pg, pw, fw, tm, final):
    T, D = h1.shape
    row = lambda w: pl.BlockSpec((tm, w), lambda i: (i, 0))
    full = lambda a: pl.BlockSpec(a.shape, lambda i: (0,) * a.ndim)
    return pl.pallas_call(
        functools.partial(_ffn_down_kernel, final=final),
        grid=(T // tm,),
        in_specs=[row(act.shape[1]), row(D), row(p2.shape[1]), full(wd), full(pg), full(pw), full(fw)],
        out_specs=row(D),
        out_shape=jax.ShapeDtypeStruct((T, D), F32),
        compiler_params=_params("parallel"),
    )(act, h1, p2, wd, pg, pw, fw)


def _rope_table():
    lane = np.arange(LANES) % HEAD_DIM
    half_r = HEAD_DIM // 2
    inv_r = jnp.power(jnp.float32(RET_THETA), -jnp.arange(half_r, dtype=F32) / half_r)
    half_n = ROPE_DIM // 2
    inv_n = jnp.power(jnp.float32(ROPE_THETA), -jnp.arange(half_n, dtype=F32) / half_n)
    rot_n = jnp.asarray(lane < ROPE_DIM, F32)
    rows = [inv_r[lane % half_r],
            jnp.asarray(np.where(lane < half_r, -1.0, 1.0), F32),
            inv_n[lane % half_n] * rot_n,
            jnp.asarray(np.where(lane < half_n, -1.0, 1.0), F32) * rot_n]
    return jnp.concatenate([jnp.stack(rows), jnp.zeros((4, LANES), F32)], axis=0)


def _layer(h2, p2, pos2, B, S, norm_mix_w, w_in, ret_gn_w, cmp_pos, cmp_k_w1, cmp_k_w2,
           cmp_v_w1, cmp_v_w2, w_out, norm_ffn_w, ffn_w_up, ffn_conv_w, ffn_conv_b,
           ffn_w_down, ple_w, ple_gate_w):
    T, D = h2.shape
    tm = _row_tile(S, 512)
    d_ff = ffn_w_down.shape[0]

    n_in = w_in.shape[1]
    w_pad = jnp.pad(w_in, ((0, 0), (0, 13 * 256 + LANES - n_in))).astype(BF16)
    ret_qkv, rg, nq, cmp_tok, kv, gates = _in_proj(
        h2, pos2, norm_mix_w.reshape(1, D), w_pad, _rope_table(), tm)

    dec, qd, kd, cdm = _retention_constants()
    ret_out = _retention(ret_qkv, rg, ret_gn_w.reshape(1, RET_WIDTH), dec, qd, kd, cdm, B, S)

    G = NSA_KV_HEADS
    n_grp = S // CMP_STRIDE
    cmp_tok = cmp_tok.reshape(B, n_grp, CMP_STRIDE, 2, G, HEAD_DIM)
    cmp_tok = cmp_tok.transpose(0, 3, 4, 1, 2, 5).reshape(B, 2, G, n_grp, CMP_STRIDE * HEAD_DIM)
    w1 = jnp.stack([cmp_k_w1, cmp_v_w1]).astype(BF16)
    w2x = jnp.tile(jnp.stack([cmp_k_w2, cmp_v_w2]), (1, 1, HEADS_PER_VREG)).astype(BF16)
    cmp_out = _compress(cmp_tok, cmp_pos.reshape(1, 1, CMP_BLOCK * HEAD_DIM), w1, w2x)

    ovt, expand, gate_spread = _nsa_constants(S)
    nsa_out = _nsa(nq, cmp_out, kv, gates, ovt, expand, gate_spread, B, S, tq=128, kc=256)

    h1, hn = _out_proj(ret_out, nsa_out, h2, w_out.astype(BF16), norm_ffn_w.reshape(1, D), tm)

    wg = ffn_w_up[:, :d_ff].astype(BF16)
    wv = ffn_w_up[:, d_ff:].astype(BF16)
    conv_w = jnp.pad(ffn_conv_w, ((0, 8 - ffn_conv_w.shape[0]), (0, 0)))
    act = _ffn_up(hn, wg, wv, conv_w, ffn_conv_b.reshape(1, d_ff), B, S, tm)
    return act, h1


def kernel(x, p, positions, norm_mix_w, w_in, ret_gn_w, cmp_pos, cmp_k_w1, cmp_k_w2, cmp_v_w1, cmp_v_w2, w_out, norm_ffn_w, ffn_w_up, ffn_conv_w, ffn_conv_b, ffn_w_down, ple_w, ple_gate_w, final_norm_w):
    B, S, D = x.shape
    T = B * S
    depth = w_in.shape[0]
    tm = _row_tile(S, 512)
    h = x.reshape(T, D)
    pos2 = positions.reshape(T, 1).astype(jnp.int32)
    for i in range(depth):
        act, h1 = _layer(h, p[i].reshape(T, -1), pos2, B, S, norm_mix_w[i], w_in[i], ret_gn_w[i],
                         cmp_pos[i], cmp_k_w1[i], cmp_k_w2[i], cmp_v_w1[i], cmp_v_w2[i], w_out[i],
                         norm_ffn_w[i], ffn_w_up[i], ffn_conv_w[i], ffn_conv_b[i],
                         ffn_w_down[i], ple_w[i], ple_gate_w[i])
        h = _ffn_down(act, h1, p[i].reshape(T, -1), ffn_w_down[i].astype(BF16),
                      ple_gate_w[i].astype(BF16), ple_w[i].astype(BF16),
                      final_norm_w.reshape(1, D), tm, final=i == depth - 1)
    return h.reshape(B, S, D)
```

```python
import functools

import numpy as np
import jax
import jax.numpy as jnp
from jax import lax
from jax.experimental import pallas as pl
from jax.experimental.pallas import tpu as pltpu

F32 = jnp.float32
BF16 = jnp.bfloat16

LANES = 128
BF16_ROWS = 16
HEAD_DIM = 64
HEADS_PER_VREG = LANES // HEAD_DIM
RET_HEADS = 8
NSA_Q_HEADS = 8
NSA_KV_HEADS = 2
NSA_GROUP = NSA_Q_HEADS // NSA_KV_HEADS
RET_WIDTH = RET_HEADS * HEAD_DIM
NSA_WIDTH = NSA_Q_HEADS * HEAD_DIM
KV_WIDTH = NSA_KV_HEADS * HEAD_DIM
RET_CHUNK = 128
RET_THETA = 10000.0
ROPE_THETA = 500000.0
ROPE_DIM = HEAD_DIM // 4
CMP_BLOCK = 32
CMP_STRIDE = 16
SEL_BLOCK = 64
SEL_TOPK = 16
WINDOW = 512
N_BRANCH = 3
GATE_ROWS = 16
EPS = 1e-6
QK_SCALE = HEAD_DIM ** -0.5
MASKED = -1e30

_OFF = np.cumsum([0, RET_WIDTH, RET_WIDTH, RET_WIDTH, RET_WIDTH, NSA_WIDTH,
                  KV_WIDTH, KV_WIDTH, KV_WIDTH, KV_WIDTH, KV_WIDTH, KV_WIDTH])
(_RQ, _RK, _RV, _RG, _NQ, _KC, _VC, _KSL, _VSL, _KWN, _VWN, _NG) = (int(v) for v in _OFF)

VMEM_LIMIT = 56 * 1024 * 1024


def _dot(a, b):
    return jnp.dot(a, b, preferred_element_type=F32)


def _dot_nt(a, b):
    return lax.dot_general(a, b, (((1,), (1,)), ((), ())), preferred_element_type=F32)


def _dot_tn(a, b):
    return lax.dot_general(a, b, (((0,), (0,)), ((), ())), preferred_element_type=F32)


def _params(*semantics):
    return pltpu.CompilerParams(dimension_semantics=semantics, vmem_limit_bytes=VMEM_LIMIT)


def _row_tile(n, pref):
    t = pref
    while n % t:
        t //= 2
    return t


def _rope(y, cos, sin_signed, first_half, half):
    nxt = pltpu.roll(y, LANES - half, 1)
    prv = pltpu.roll(y, half, 1)
    return y * cos + jnp.where(first_half, nxt, prv) * sin_signed


def _in_proj_kernel(x_ref, pos_ref, nw_ref, w_ref, wt_ref, tab_ref,
                    ret_ref, rg_ref, nq_ref, cmp_ref, kdup_ref, vt_ref, gt_ref):
    x = x_ref[...]
    tm = x.shape[0]
    ms = jnp.mean(x * x, axis=-1, keepdims=True)
    xn = (x * lax.rsqrt(ms + EPS) * nw_ref[...]).astype(BF16)

    pos = pos_ref[...].astype(F32)
    tab = tab_ref[...]
    lane = lax.broadcasted_iota(jnp.int32, (tm, LANES), 1)
    in_head = lane % HEAD_DIM
    low = lane < HEAD_DIM
    ang_r = pos * tab[0:1, :]
    cos_r, sin_r = jnp.cos(ang_r), jnp.sin(ang_r) * tab[1:2, :]
    first_r = in_head < HEAD_DIM // 2
    ang_n = pos * tab[2:3, :]
    cos_n, sin_n = jnp.cos(ang_n), jnp.sin(ang_n) * tab[3:4, :]
    first_n = in_head < ROPE_DIM // 2

    rope_r = lambda y: _rope(y, cos_r, sin_r, first_r, HEAD_DIM // 2)
    rope_n = lambda y: _rope(y, cos_n, sin_n, first_n, ROPE_DIM // 2)

    def halves(ci):
        y = _dot(xn, w_ref[:, ci * 256:(ci + 1) * 256])
        return y[:, :LANES], y[:, LANES:]

    for ci in range(2):
        for h, y in enumerate(halves(ci)):
            c0 = ci * 256 + h * LANES
            ret_ref[:, c0:c0 + LANES] = rope_r(y).astype(BF16)
    for ci in range(2):
        for h, y in enumerate(halves(2 + ci)):
            c0 = RET_WIDTH + ci * 256 + h * LANES
            ret_ref[:, c0:c0 + LANES] = (rope_r(y) * QK_SCALE).astype(BF16)
    for ci in range(2):
        for h, y in enumerate(halves(4 + ci)):
            c0 = 2 * RET_WIDTH + ci * 256 + h * LANES
            ret_ref[:, c0:c0 + LANES] = y.astype(BF16)
    for ci in range(2):
        for h, y in enumerate(halves(6 + ci)):
            c0 = ci * 256 + h * LANES
            rg_ref[:, c0:c0 + LANES] = y
    for ci in range(2):
        for h, y in enumerate(halves(8 + ci)):
            y = rope_n(y) * QK_SCALE
            c0 = (ci * 2 + h) * 2 * LANES
            nq_ref[:, c0:c0 + LANES] = jnp.where(low, y, 0.0).astype(BF16)
            nq_ref[:, c0 + LANES:c0 + 2 * LANES] = jnp.where(
                low, pltpu.roll(y, HEAD_DIM, 1), 0.0).astype(BF16)
    kc, vc = halves(10)
    cmp_ref[:, :LANES] = rope_n(kc).astype(BF16)
    cmp_ref[:, LANES:] = vc.astype(BF16)
    for n, y in enumerate(halves(11)):
        y = rope_n(y)
        sw = pltpu.roll(y, HEAD_DIM, 1)
        kdup_ref[:, (2 * n) * LANES:(2 * n + 1) * LANES] = jnp.where(low, y, sw).astype(BF16)
        kdup_ref[:, (2 * n + 1) * LANES:(2 * n + 2) * LANES] = jnp.where(low, sw, y).astype(BF16)
    t = _dot_nt(wt_ref[...], xn)
    nv = vt_ref.shape[0]
    vt_ref[...] = t[:nv, :].astype(BF16)
    gt_ref[...] = jax.nn.sigmoid(t[nv:, :])


def _in_proj(x2, pos2, norm_w, w_main, w_t, tab, tm):
    T, D = x2.shape
    nv = 2 * KV_WIDTH
    ng = w_t.shape[0] - nv
    row = lambda w: pl.BlockSpec((tm, w), lambda i: (i, 0))
    col = lambda h: pl.BlockSpec((h, tm), lambda i: (0, i))
    full = lambda a: pl.BlockSpec(a.shape, lambda i: (0,) * a.ndim)
    return pl.pallas_call(
        _in_proj_kernel,
        grid=(T // tm,),
        in_specs=[row(D), row(1), full(norm_w), full(w_main), full(w_t), full(tab)],
        out_specs=[row(3 * RET_WIDTH), row(RET_WIDTH), row(NSA_Q_HEADS * LANES), row(2 * KV_WIDTH),
                   row(4 * LANES), col(nv), col(ng)],
        out_shape=[jax.ShapeDtypeStruct((T, 3 * RET_WIDTH), BF16),
                   jax.ShapeDtypeStruct((T, RET_WIDTH), F32),
                   jax.ShapeDtypeStruct((T, NSA_Q_HEADS * LANES), BF16),
                   jax.ShapeDtypeStruct((T, 2 * KV_WIDTH), BF16),
                   jax.ShapeDtypeStruct((T, 4 * LANES), BF16),
                   jax.ShapeDtypeStruct((nv, T), BF16),
                   jax.ShapeDtypeStruct((ng, T), F32)],
        compiler_params=_params("parallel"),
    )(x2, pos2, norm_w, w_main, w_t, tab)


def _in_proj_weights(w_in):
    w_main = jnp.concatenate([w_in[:, :_KSL], w_in[:, _KSL:_VSL], w_in[:, _KWN:_VWN]], axis=1)
    gate_cols = np.full((NSA_KV_HEADS * GATE_ROWS,), -1)
    for g in range(NSA_KV_HEADS):
        for r in range(NSA_GROUP):
            for br in range(N_BRANCH):
                gate_cols[g * GATE_ROWS + r * N_BRANCH + br] = _NG + (g * NSA_GROUP + r) * N_BRANCH + br
    w_gate = jnp.where(gate_cols[None, :] >= 0, w_in[:, np.maximum(gate_cols, 0)], 0.0)
    w_t = jnp.concatenate([w_in[:, _VSL:_KWN], w_in[:, _VWN:_NG], w_gate], axis=1).T
    return w_main.astype(BF16), w_t.astype(BF16)


def _retention_kernel(q_ref, k_ref, v_ref, rg_ref, gnw_ref, dec_ref, qd_ref, kd_ref, cdm_ref,
                      o_ref, state_ref, *, n_chunks):
    C = RET_CHUNK
    lane = lax.broadcasted_iota(jnp.int32, (C, LANES), 1)
    low = lane < HEAD_DIM
    r_i = lax.broadcasted_iota(jnp.int32, (LANES, LANES), 0)
    c_i = lax.broadcasted_iota(jnp.int32, (LANES, LANES), 1)
    same_head = (r_i < HEAD_DIM) == (c_i < HEAD_DIM)
    state_ref[...] = jnp.zeros_like(state_ref)

    def chunk(n, carry):
        r0 = pl.multiple_of(n * C, C)
        q = q_ref[pl.ds(r0, C), :]
        k = k_ref[pl.ds(r0, C), :]
        v = v_ref[pl.ds(r0, C), :]
        zero = jnp.zeros_like(q)
        s0 = _dot_nt(jnp.where(low, q, zero), k) * dec_ref[0]
        s1 = _dot_nt(jnp.where(low, zero, q), k) * dec_ref[1]
        o_intra = jnp.where(low, _dot(s0.astype(BF16), v), _dot(s1.astype(BF16), v))
        state = state_ref[...]
        qd = (q.astype(F32) * qd_ref[...]).astype(BF16)
        o_cross = _dot(qd, state.astype(BF16))
        kd = (k.astype(F32) * kd_ref[...]).astype(BF16)
        kv = _dot_tn(kd, v)
        state_ref[...] = state * cdm_ref[0] + jnp.where(same_head, kv, 0.0)

        ro = o_intra + o_cross
        inv_d = 1.0 / HEAD_DIM
        sum0 = jnp.sum(jnp.where(low, ro, 0.0), axis=-1, keepdims=True)
        sum1 = jnp.sum(jnp.where(low, 0.0, ro), axis=-1, keepdims=True)
        d = ro - jnp.where(low, sum0, sum1) * inv_d
        dd = d * d
        var0 = jnp.sum(jnp.where(low, dd, 0.0), axis=-1, keepdims=True)
        var1 = jnp.sum(jnp.where(low, 0.0, dd), axis=-1, keepdims=True)
        var = jnp.where(low, var0, var1) * inv_d
        y = d * lax.rsqrt(var + EPS) * gnw_ref[...]
        g = rg_ref[pl.ds(r0, C), :]
        o_ref[pl.ds(r0, C), :] = (y * (g * jax.nn.sigmoid(g))).astype(BF16)
        return carry

    lax.fori_loop(0, n_chunks, chunk, 0)


def _retention(ret_qkv, rg, gn_w, dec, qd, kd, cdm, B, S):
    T = B * S
    n_pairs = RET_HEADS // HEADS_PER_VREG
    seq = lambda off: pl.BlockSpec((S, LANES), lambda b, hp: (b, off + hp))
    return pl.pallas_call(
        functools.partial(_retention_kernel, n_chunks=S // RET_CHUNK),
        grid=(B, n_pairs),
        in_specs=[seq(0), seq(n_pairs), seq(2 * n_pairs), seq(0),
                  pl.BlockSpec((1, LANES), lambda b, hp: (0, hp)),
                  pl.BlockSpec((HEADS_PER_VREG, RET_CHUNK, RET_CHUNK), lambda b, hp: (hp, 0, 0)),
                  pl.BlockSpec((RET_CHUNK, LANES), lambda b, hp: (0, hp)),
                  pl.BlockSpec((RET_CHUNK, LANES), lambda b, hp: (0, hp)),
                  pl.BlockSpec((1, LANES, LANES), lambda b, hp: (hp, 0, 0))],
        out_specs=seq(0),
        out_shape=jax.ShapeDtypeStruct((T, RET_WIDTH), BF16),
        scratch_shapes=[pltpu.VMEM((LANES, LANES), F32)],
        compiler_params=_params("parallel", "parallel"),
    )(ret_qkv, ret_qkv, ret_qkv, rg, gn_w, dec, qd, kd, cdm)


def _retention_constants():
    H, C = RET_HEADS, RET_CHUNK
    log_g = jnp.log1p(-jnp.exp2(-5.0 - jnp.arange(H, dtype=F32)))
    pos = jnp.arange(C, dtype=F32)
    diff = pos[:, None] - pos[None, :]
    dec = jnp.where(diff >= 0, jnp.exp(jnp.maximum(diff, 0.0) * log_g[:, None, None]), 0.0)
    q_decay = jnp.exp((pos + 1.0) * log_g[:, None]).T
    k_decay = jnp.exp((C - 1.0 - pos) * log_g[:, None]).T
    chunk_decay = jnp.exp(C * log_g)
    qd = jnp.repeat(q_decay, HEAD_DIM, axis=1)
    kd = jnp.repeat(k_decay, HEAD_DIM, axis=1)
    cd_rows = jnp.repeat(chunk_decay, HEAD_DIM).reshape(H // HEADS_PER_VREG, LANES, 1)
    head_of = jnp.arange(LANES) // HEAD_DIM
    same = (head_of[:, None] == head_of[None, :]).astype(F32)
    return dec, qd, kd, cd_rows * same[None]


def _compress_kernel(tk_ref, tv_ref, pos_ref, w1_ref, w2k_ref, w2vt_ref, ck_ref, cvt_ref):
    pos = jnp.broadcast_to(pos_ref[...], (8, pos_ref.shape[1])).astype(BF16)

    def hidden(t, w1_ref_a):
        half = t.shape[1]
        bias = _dot(pos, w1_ref_a[...])[0:1, :]
        first = _dot(t, w1_ref_a[:half, :])
        second = _dot(t, w1_ref_a[half:, :])
        n = t.shape[0]
        h = first + pltpu.roll(second, n - 1, 0) + bias
        return jax.nn.gelu(h).astype(BF16)

    ck_ref[0, 0] = _dot(hidden(tk_ref[0, 0, 0], w1_ref.at[0]), w2k_ref[...]).astype(BF16)
    cvt_ref[0, 0] = _dot_nt(w2vt_ref[...], hidden(tv_ref[0, 0, 0], w1_ref.at[1])).astype(BF16)


def _compress(cmp_tok, pos_flat, w1, w2k, w2vt):
    B, _, G, n, width = cmp_tok.shape
    tok = lambda a: pl.BlockSpec((1, 1, 1, n, width), lambda b, g: (b, a, g, 0, 0))
    full = lambda a: pl.BlockSpec(a.shape, lambda b, g: (0,) * a.ndim)
    return pl.pallas_call(
        _compress_kernel,
        grid=(B, G),
        in_specs=[tok(0), tok(1), full(pos_flat), full(w1), full(w2k), full(w2vt)],
        out_specs=[pl.BlockSpec((1, 1, n, LANES), lambda b, g: (b, g, 0, 0)),
                   pl.BlockSpec((1, 1, HEAD_DIM, n), lambda b, g: (b, g, 0, 0))],
        out_shape=[jax.ShapeDtypeStruct((B, G, n, LANES), BF16),
                   jax.ShapeDtypeStruct((B, G, HEAD_DIM, n), BF16)],
        compiler_params=_params("parallel", "parallel"),
    )(cmp_tok, cmp_tok, pos_flat, w1, w2k, w2vt)


def _nsa_kernel(q_ref, ck_ref, cvt_ref, ks_ref, kw_ref, vst_ref, vwt_ref, gt_ref, ovt_ref, eb_ref,
                o_ref, m_ref, acc_ref, out_ref, score_ref, *, tq, top):
    R = NSA_GROUP
    D = HEAD_DIM
    t0 = pl.multiple_of(pl.program_id(2) * tq, tq)
    t_off = lax.broadcasted_iota(jnp.int32, (1, tq), 1)
    tcol = t0 + t_off
    k_off = lax.broadcasted_iota(jnp.int32, (tq, 1), 0)
    low = lax.broadcasted_iota(jnp.int32, (tq, LANES), 1) < D
    ones_rows = jnp.ones((BF16_ROWS, tq), BF16)
    gate = lambda r, br: gt_ref[r * N_BRANCH + br:r * N_BRANCH + br + 1, :]

    q =[q_ref[:, r * LANES:(r + 1) * LANES] for r in range(R)]

    def value_rows(vt_ref_, start):
        return jnp.concatenate([vt_ref_[:, pl.ds(start, tq)], ones_rows], axis=0)

    def first(r, s_t, vrows):
        m = jnp.max(s_t, axis=0, keepdims=True)
        m_ref[r] = m
        acc_ref[r] = _dot(vrows, jnp.exp(s_t - m).astype(BF16))

    def more(r, s_t, vrows):
        m_old = m_ref[r]
        m_new = jnp.maximum(m_old, jnp.max(s_t, axis=0, keepdims=True))
        m_ref[r] = m_new
        acc_ref[r] = (jnp.exp(m_old - m_new) * acc_ref[r]
                      + _dot(vrows, jnp.exp(s_t - m_new).astype(BF16)))

    def mix(r, br):
        o = acc_ref[r, :D, :] * ((1.0 / acc_ref[r, D:D + 1, :]) * gate(r, br))
        out_ref[r * D:(r + 1) * D, :] = out_ref[r * D:(r + 1) * D, :] + o

    ck = ck_ref[0, 0]
    cvt = cvt_ref[0, 0]
    ncp = ck.shape[0]
    cmp_end = lax.broadcasted_iota(jnp.int32, (ncp, 1), 0) * CMP_STRIDE + (CMP_BLOCK - 1)
    cmask = cmp_end <= tcol
    p_sum = jnp.zeros((ncp, tq), F32)
    scores = [_dot_nt(ck, q[r]) for r in range(R)]
    for r in range(R):
        s = jnp.where(cmask, scores[r], -jnp.inf)
        m = jnp.max(s, axis=0, keepdims=True)
        m = jnp.where(jnp.isfinite(m), m, 0.0)
        e = jnp.exp(s - m)
        p = e * (1.0 / jnp.maximum(jnp.sum(e, axis=0, keepdims=True), 1e-30))
        out_ref[r * D:(r + 1) * D, :] = _dot(cvt, p.astype(BF16)) * gate(r, 0)
        p_sum = p_sum + p

    imp_t = _dot(ovt_ref[...], p_sum.astype(BF16))
    n_sel = imp_t.shape[0]
    blk_i = lax.broadcasted_iota(jnp.int32, (n_sel, tq), 0)
    cur = tcol // SEL_BLOCK
    valid = blk_i * SEL_BLOCK <= tcol
    forced = (blk_i == 0) | (blk_i == cur) | (blk_i == cur - 1)
    score_ref[...] = jnp.where(forced, jnp.inf, jnp.where(valid, imp_t, -jnp.inf))
    not_sel = []
    for jg in range(n_sel // 8):
        mine = score_ref[jg * 8:(jg + 1) * 8, :]
        blk_g = jg * 8 + lax.broadcasted_iota(jnp.int32, (8, tq), 0)
        ahead = jnp.zeros((8, tq), F32)
        for kb in range(n_sel):
            other = score_ref[kb:kb + 1, :]
            if kb < jg * 8:
                ahead = ahead + jnp.where(other >= mine, 1.0, 0.0)
            elif kb >= (jg + 1) * 8:
                ahead = ahead + jnp.where(other > mine, 1.0, 0.0)
            else:
                ahead = ahead + jnp.where(blk_g > kb, jnp.where(other >= mine, 1.0, 0.0),
                                          jnp.where(other > mine, 1.0, 0.0))
        not_sel.append(jnp.where(ahead < top, 0.0, 1.0))
    pad_rows = LANES - D - n_sel
    not_sel_t = jnp.concatenate([jnp.zeros((D, tq), F32)] + not_sel
                                + ([jnp.zeros((pad_rows, tq), F32)] if pad_rows else []), axis=0)
    not_sel_q = not_sel_t.T.astype(BF16)
    qa = [jnp.where(low, q[r], not_sel_q) for r in range(R)]

    def keys_aug(start):
        return jnp.where(low, ks_ref[0, pl.ds(start, tq), :], eb_ref[pl.ds(start, tq), :])

    causal = k_off <= t_off
    ka = keys_aug(t0)
    vr = value_rows(vst_ref, t0)
    scores = [_dot_nt(ka, qa[r]) for r in range(R)]
    for r in range(R):
        first(r, jnp.where(causal, scores[r], MASKED), vr)

    def sel_chunk(c, carry):
        start = pl.multiple_of(c * tq, tq)
        ka = keys_aug(start)
        vr = value_rows(vst_ref, start)
        scores = [_dot_nt(ka, qa[r]) for r in range(R)]
        for r in range(R):
            more(r, scores[r], vr)
        return carry

    lax.fori_loop(0, pl.program_id(2), sel_chunk, 0)
    for r in range(R):
        mix(r, 1)

    kw = kw_ref[0, pl.ds(t0, tq), :]
    vr = value_rows(vwt_ref, t0)
    scores = [_dot_nt(kw, q[r]) for r in range(R)]
    for r in range(R):
        first(r, jnp.where(causal, scores[r], MASKED), vr)
    n_back = WINDOW // tq
    for j in range(1, n_back + 1):
        @pl.when(t0 >= j * tq)
        def _():
            start = pl.multiple_of(t0 - j * tq, tq)
            kw = kw_ref[0, pl.ds(start, tq), :]
            vr = value_rows(vwt_ref, start)
            scores = [_dot_nt(kw, q[r]) for r in range(R)]
            for r in range(R):
                s_t = scores[r]
                if j == n_back:
                    s_t = jnp.where(k_off > t_off, s_t, MASKED)
                more(r, s_t, vr)
    for r in range(R):
        mix(r, 2)

    o_ref[...] = out_ref[...].T.astype(BF16)


def _nsa(nq, ck, cvt, kdup, vt, gt, ovt, eb, B, S, tq):
    T = B * S
    G = NSA_KV_HEADS
    assert S % tq == 0 and WINDOW % tq == 0 and S // SEL_BLOCK <= LANES - HEAD_DIM
    nqt = S // tq
    ncp = ck.shape[2]
    kdup3 = kdup.reshape(B, S, kdup.shape[1])
    acc_rows = HEAD_DIM + BF16_ROWS
    keys = lambda off: pl.BlockSpec((1, S, LANES), lambda b, g, i: (b, 0, off + g))
    vals = lambda off: pl.BlockSpec((HEAD_DIM, S), lambda b, g, i: (off + g, b))
    full = lambda a: pl.BlockSpec(a.shape, lambda b, g, i: (0,) * a.ndim)
    return pl.pallas_call(
        functools.partial(_nsa_kernel, tq=tq, top=min(SEL_TOPK, S // SEL_BLOCK)),
        grid=(B, G, nqt),
        in_specs=[pl.BlockSpec((tq, NSA_GROUP * LANES), lambda b, g, i: (b * nqt + i, g)),
                  pl.BlockSpec((1, 1, ncp, LANES), lambda b, g, i: (b, g, 0, 0)),
                  pl.BlockSpec((1, 1, HEAD_DIM, ncp), lambda b, g, i: (b, g, 0, 0)),
                  keys(0), keys(G), vals(0), vals(G),
                  pl.BlockSpec((GATE_ROWS, tq), lambda b, g, i: (g, b * nqt + i)),
                  full(ovt), full(eb)],
        out_specs=pl.BlockSpec((tq, NSA_GROUP * HEAD_DIM), lambda b, g, i: (b * nqt + i, g)),
        out_shape=jax.ShapeDtypeStruct((T, NSA_WIDTH), BF16),
        scratch_shapes=[pltpu.VMEM((NSA_GROUP, 1, tq), F32),
                        pltpu.VMEM((NSA_GROUP, acc_rows, tq), F32),
                        pltpu.VMEM((NSA_GROUP * HEAD_DIM, tq), F32),
                        pltpu.VMEM((S // SEL_BLOCK, tq), F32)],
        compiler_params=_params("parallel", "parallel", "parallel"),
    )(nq, ck, cvt, kdup3, kdup3, vt, vt, gt, ovt, eb)


def _nsa_constants(S):
    n_cmp = (S - CMP_BLOCK) // CMP_STRIDE + 1
    ncp = S // CMP_STRIDE
    n_sel = S // SEL_BLOCK
    ci = np.arange(ncp)[None, :]
    sj = np.arange(n_sel)[:, None]
    ovt = ((ci * CMP_STRIDE < (sj + 1) * SEL_BLOCK)
           & (ci * CMP_STRIDE + CMP_BLOCK > sj * SEL_BLOCK) & (ci < n_cmp))
    eb = np.zeros((S, LANES), np.float32)
    eb[np.arange(S), HEAD_DIM + np.arange(S) // SEL_BLOCK] = MASKED
    return jnp.asarray(ovt, BF16), jnp.asarray(eb, BF16)


def _out_proj_kernel(ret_ref, nsa_ref, x_ref, wo_ref, nw_ref, h_ref, hn_ref):
    mix = _dot(ret_ref[...], wo_ref[:RET_WIDTH, :]) + _dot(nsa_ref[...], wo_ref[RET_WIDTH:, :])
    h = x_ref[...] + mix
    h_ref[...] = h
    ms = jnp.mean(h * h, axis=-1, keepdims=True)
    hn_ref[...] = (h * lax.rsqrt(ms + EPS) * nw_ref[...]).astype(BF16)


def _out_proj(ret_out, nsa_out, x2, w_out, norm_w, tm):
    T, D = x2.shape
    row = lambda w: pl.BlockSpec((tm, w), lambda i: (i, 0))
    full = lambda a: pl.BlockSpec(a.shape, lambda i: (0,) * a.ndim)
    return pl.pallas_call(
        _out_proj_kernel,
        grid=(T // tm,),
        in_specs=[row(RET_WIDTH), row(NSA_WIDTH), row(D), full(w_out), full(norm_w)],
        out_specs=[row(D), row(D)],
        out_shape=[jax.ShapeDtypeStruct((T, D), F32), jax.ShapeDtypeStruct((T, D), BF16)],
        compiler_params=_params("parallel"),
    )(ret_out, nsa_out, x2, w_out, norm_w)


def _ffn_up_kernel(hn_ref, wg_ref, wv_ref, cw_ref, cb_ref, act_ref, tail_ref, *, fc):
    @pl.when(pl.program_id(1) == 0)
    def _():
        tail_ref[...] = jnp.zeros_like(tail_ref)

    hn = hn_ref[...]
    tm = hn.shape[0]
    row = lax.broadcasted_iota(jnp.int32, (tm, fc), 0)
    for c in range(wg_ref.shape[1] // fc):
        sl = slice(c * fc, (c + 1) * fc)
        g = _dot(hn, wg_ref[:, sl])
        val = _dot(hn, wv_ref[:, sl])
        tail = tail_ref[:, sl]
        g1 = jnp.where(row == 0, tail[7:8, :], pltpu.roll(g, 1, 0))
        g2 = jnp.where(row == 0, tail[6:7, :],
                       jnp.where(row == 1, tail[7:8, :], pltpu.roll(g, 2, 0)))
        tail_ref[:, sl] = g[tm - 8:, :]
        cw = cw_ref[:, sl]
        y = cb_ref[:, sl] + cw[0:1, :] * g2 + cw[1:2, :] * g1 + cw[2:3, :] * g
        act_ref[:, sl] = (y * jax.nn.sigmoid(y) * val).astype(BF16)


def _ffn_up(hn, wg, wv, conv_w, conv_b, B, S, tm):
    T, D = hn.shape
    F = wg.shape[1]
    nt = S // tm
    full = lambda a: pl.BlockSpec(a.shape, lambda b, j: (0,) * a.ndim)
    return pl.pallas_call(
        functools.partial(_ffn_up_kernel, fc=256),
        grid=(B, nt),
        in_specs=[pl.BlockSpec((tm, D), lambda b, j: (b * nt + j, 0)),
                  full(wg), full(wv), full(conv_w), full(conv_b)],
        out_specs=pl.BlockSpec((tm, F), lambda b, j: (b * nt + j, 0)),
        out_shape=jax.ShapeDtypeStruct((T, F), BF16),
        scratch_shapes=[pltpu.VMEM((8, F), F32)],
        compiler_params=_params("arbitrary", "arbitrary"),
    )(hn, wg, wv, conv_w, conv_b)


def _ffn_down_kernel(act_ref, h_ref, p_ref, wd_ref, pg_ref, pw_ref, fw_ref, o_ref, *, final):
    h = h_ref[...] + _dot(act_ref[...], wd_ref[...])
    gate = jax.nn.sigmoid(_dot(h.astype(BF16), pg_ref[...]))
    h = h + gate * _dot(p_ref[...].astype(BF16), pw_ref[...])
    if final:
        ms = jnp.mean(h * h, axis=-1, keepdims=True)
        h = h * lax.rsqrt(ms + EPS) * fw_ref[...]
    o_ref[...] = h


def _ffn_down(act, h1, p2, wd, pg, pw, fw, tm, final):
    T, D = h1.shape
    row = lambda w: pl.BlockSpec((tm, w), lambda i: (i, 0))
    full = lambda a: pl.BlockSpec(a.shape, lambda i: (0,) * a.ndim)
    return pl.pallas_call(
        functools.partial(_ffn_down_kernel, final=final),
        grid=(T // tm,),
        in_specs=[row(act.shape[1]), row(D), row(p2.shape[1]), full(wd), full(pg), full(pw), full(fw)],
        out_specs=row(D),
        out_shape=jax.ShapeDtypeStruct((T, D), F32),
        compiler_params=_params("parallel"),
    )(act, h1, p2, wd, pg, pw, fw)


def _rope_table():
    lane = np.arange(LANES) % HEAD_DIM
    half_r = HEAD_DIM // 2
    inv_r = jnp.power(jnp.float32(RET_THETA), -jnp.arange(half_r, dtype=F32) / half_r)
    half_n = ROPE_DIM // 2
    inv_n = jnp.power(jnp.float32(ROPE_THETA), -jnp.arange(half_n, dtype=F32) / half_n)
    rot_n = jnp.asarray(lane < ROPE_DIM, F32)
    rows = [inv_r[lane % half_r],
            jnp.asarray(np.where(lane < half_r, -1.0, 1.0), F32),
            inv_n[lane % half_n] * rot_n,
            jnp.asarray(np.where(lane < half_n, -1.0, 1.0), F32) * rot_n]
    return jnp.concatenate([jnp.stack(rows), jnp.zeros((4, LANES), F32)], axis=0)


def _layer(h2, pos2, B, S, norm_mix_w, w_in, ret_gn_w, cmp_pos, cmp_k_w1, cmp_k_w2,
           cmp_v_w1, cmp_v_w2, w_out, norm_ffn_w, ffn_w_up, ffn_conv_w, ffn_conv_b, d_ff):
    T, D = h2.shape
    tm = _row_tile(S, 512)

    w_main, w_t = _in_proj_weights(w_in)
    ret_qkv, rg, nq, cmp_tok, kdup, vt, gt = _in_proj(
        h2, pos2, norm_mix_w.reshape(1, D), w_main, w_t, _rope_table(), tm)

    dec, qd, kd, cdm = _retention_constants()
    ret_out = _retention(ret_qkv, rg, ret_gn_w.reshape(1, RET_WIDTH), dec, qd, kd, cdm, B, S)

    G = NSA_KV_HEADS
    n_grp = S // CMP_STRIDE
    cmp_tok = cmp_tok.reshape(B, n_grp, CMP_STRIDE, 2, G, HEAD_DIM)
    cmp_tok = cmp_tok.transpose(0, 3, 4, 1, 2, 5).reshape(B, 2, G, n_grp, CMP_STRIDE * HEAD_DIM)
    w1 = jnp.stack([cmp_k_w1, cmp_v_w1]).astype(BF16)
    w2k = jnp.tile(cmp_k_w2, (1, HEADS_PER_VREG)).astype(BF16)
    ck, cvt = _compress(cmp_tok, cmp_pos.reshape(1, CMP_BLOCK * HEAD_DIM), w1, w2k,
                        cmp_v_w2.T.astype(BF16))

    ovt, eb = _nsa_constants(S)
    nsa_out = _nsa(nq, ck, cvt, kdup, vt, gt, ovt, eb, B, S, tq=256)

    h1, hn = _out_proj(ret_out, nsa_out, h2, w_out.astype(BF16), norm_ffn_w.reshape(1, D), tm)

    wg = ffn_w_up[:, :d_ff].astype(BF16)
    wv = ffn_w_up[:, d_ff:].astype(BF16)
    conv_w = jnp.pad(ffn_conv_w, ((0, 8 - ffn_conv_w.shape[0]), (0, 0)))
    act = _ffn_up(hn, wg, wv, conv_w, ffn_conv_b.reshape(1, d_ff), B, S, tm)
    return act, h1


def kernel(x, p, positions, norm_mix_w, w_in, ret_gn_w, cmp_pos, cmp_k_w1, cmp_k_w2, cmp_v_w1, cmp_v_w2, w_out, norm_ffn_w, ffn_w_up, ffn_conv_w, ffn_conv_b, ffn_w_down, ple_w, ple_gate_w, final_norm_w):
    B, S, D = x.shape
    T = B * S
    depth = w_in.shape[0]
    tm = _row_tile(S, 512)
    h = x.reshape(T, D)
    pos2 = positions.reshape(T, 1).astype(jnp.int32)
    for i in range(depth):
        act, h1 = _layer(h, pos2, B, S, norm_mix_w[i], w_in[i], ret_gn_w[i],
                         cmp_pos[i], cmp_k_w1[i], cmp_k_w2[i], cmp_v_w1[i], cmp_v_w2[i], w_out[i],
                         norm_ffn_w[i], ffn_w_up[i], ffn_conv_w[i], ffn_conv_b[i],
                         ffn_w_down.shape[1])
        h = _ffn_down(act, h1, p[i].reshape(T, -1), ffn_w_down[i].astype(BF16),
                      ple_gate_w[i].astype(BF16), ple_w[i].astype(BF16),
                      final_norm_w.reshape(1, D), tm, final=i == depth - 1)
    return h.reshape(B, S, D)
```

```python
import functools

import numpy as np
import jax
import jax.numpy as jnp
from jax import lax
from jax.experimental import pallas as pl
from jax.experimental.pallas import tpu as pltpu

F32 = jnp.float32
BF16 = jnp.bfloat16

LANES = 128
BF16_ROWS = 16
HEAD_DIM = 64
HEADS_PER_VREG = LANES // HEAD_DIM
RET_HEADS = 8
NSA_Q_HEADS = 8
NSA_KV_HEADS = 2
NSA_GROUP = NSA_Q_HEADS // NSA_KV_HEADS
RET_WIDTH = RET_HEADS * HEAD_DIM
NSA_WIDTH = NSA_Q_HEADS * HEAD_DIM
KV_WIDTH = NSA_KV_HEADS * HEAD_DIM
RET_CHUNK = 128
RET_THETA = 10000.0
ROPE_THETA = 500000.0
ROPE_DIM = HEAD_DIM // 4
CMP_BLOCK = 32
CMP_STRIDE = 16
SEL_BLOCK = 64
SEL_TOPK = 16
WINDOW = 512
N_BRANCH = 3
GATE_ROWS = 16
EPS = 1e-6
QK_SCALE = HEAD_DIM ** -0.5
LOG2_E = 1.4426950408889634
MASKED = -1e30
LOWEST = -3e38

_OFF = np.cumsum([0, RET_WIDTH, RET_WIDTH, RET_WIDTH, RET_WIDTH, NSA_WIDTH,
                  KV_WIDTH, KV_WIDTH, KV_WIDTH, KV_WIDTH, KV_WIDTH, KV_WIDTH])
(_RQ, _RK, _RV, _RG, _NQ, _KC, _VC, _KSL, _VSL, _KWN, _VWN, _NG) = (int(v) for v in _OFF)

VMEM_LIMIT = 56 * 1024 * 1024


def _dot(a, b):
    return jnp.dot(a, b, preferred_element_type=F32)


def _dot_nt(a, b):
    return lax.dot_general(a, b, (((1,), (1,)), ((), ())), preferred_element_type=F32)


def _dot_tn(a, b):
    return lax.dot_general(a, b, (((0,), (0,)), ((), ())), preferred_element_type=F32)


def _params(*semantics):
    return pltpu.CompilerParams(dimension_semantics=semantics, vmem_limit_bytes=VMEM_LIMIT)


def _row_tile(n, pref):
    t = pref
    while n % t:
        t //= 2
    return t


def _rope(y, cos, sin_signed, first_half, half):
    nxt = pltpu.roll(y, LANES - half, 1)
    prv = pltpu.roll(y, half, 1)
    return y * cos + jnp.where(first_half, nxt, prv) * sin_signed


def _in_proj_kernel(x_ref, pos_ref, nw_ref, w_ref, wt_ref, tab_ref,
                    ret_ref, rg_ref, nq_ref, cmp_ref, kdup_ref, vt_ref, gt_ref, stage_ref):
    x = x_ref[...]
    tm = x.shape[0]
    ms = jnp.mean(x * x, axis=-1, keepdims=True)
    xn = (x * lax.rsqrt(ms + EPS) * nw_ref[...]).astype(BF16)

    pos = pos_ref[...].astype(F32)
    tab = tab_ref[...]
    lane = lax.broadcasted_iota(jnp.int32, (tm, LANES), 1)
    in_head = lane % HEAD_DIM
    low = lane < HEAD_DIM
    half_r, half_n = HEAD_DIM // 2, ROPE_DIM // 2
    ang = pos * tab[0:1, :]
    first_r = in_head < half_r
    first_n = in_head < half_n
    rot_n = in_head < ROPE_DIM

    def spread_r(c):
        c = jnp.where(lane < half_r, c, pltpu.roll(c, half_r, 1))
        return jnp.where(low, c, pltpu.roll(c, HEAD_DIM, 1))

    def spread_n(c, rest):
        c = pltpu.roll(c, LANES - half_r, 1)
        c = jnp.where(lane < half_n, c, pltpu.roll(c, half_n, 1))
        c = jnp.where(low, c, pltpu.roll(c, HEAD_DIM, 1))
        return jnp.where(rot_n, c, rest)

    cos_a, sin_a = jnp.cos(ang), jnp.sin(ang)
    cos_r, sin_r = spread_r(cos_a), spread_r(sin_a) * tab[1:2, :]
    cos_n, sin_n = spread_n(cos_a, 1.0), spread_n(sin_a, 0.0) * tab[2:3, :]

    rope_r = lambda y: _rope(y, cos_r, sin_r, first_r, HEAD_DIM // 2)
    rope_n = lambda y: _rope(y, cos_n, sin_n, first_n, ROPE_DIM // 2)

    def halves(ci):
        y = _dot(xn, w_ref[:, ci * 256:(ci + 1) * 256])
        return y[:, :LANES], y[:, LANES:]

    for ci in range(2):
        for h, y in enumerate(halves(4 + ci)):
            c0 = 2 * RET_WIDTH + ci * 256 + h * LANES
            ret_ref[:, c0:c0 + LANES] = y.astype(BF16)
    t = _dot_nt(wt_ref[...], xn)
    nv = vt_ref.shape[0]
    vt_ref[...] = t[:nv, :].astype(BF16)
    gt_ref[...] = jax.nn.sigmoid(t[nv:, :])
    for ci in range(2):
        for h, y in enumerate(halves(6 + ci)):
            c0 = ci * 256 + h * LANES
            rg_ref[:, c0:c0 + LANES] = y
    for ci in range(2):
        for h, y in enumerate(halves(ci)):
            c0 = ci * 256 + h * LANES
            ret_ref[:, c0:c0 + LANES] = rope_r(y).astype(BF16)
    for ci in range(2):
        for h, y in enumerate(halves(2 + ci)):
            c0 = RET_WIDTH + ci * 256 + h * LANES
            ret_ref[:, c0:c0 + LANES] = (rope_r(y) * QK_SCALE).astype(BF16)
    for ci in range(2):
        for h, y in enumerate(halves(8 + ci)):
            y = rope_n(y) * (QK_SCALE * LOG2_E)
            c0 = (ci * 2 + h) * 2 * LANES
            nq_ref[:, c0:c0 + LANES] = jnp.where(low, y, 0.0).astype(BF16)
            nq_ref[:, c0 + LANES:c0 + 2 * LANES] = jnp.where(
                low, pltpu.roll(y, HEAD_DIM, 1), 0.0).astype(BF16)
    kc, vc = halves(10)
    for a, y in enumerate((rope_n(kc), vc)):
        stage_ref[2 * a] = y
        stage_ref[2 * a + 1] = pltpu.roll(y, HEAD_DIM, 1)
    n_grp = tm // CMP_STRIDE
    low_g = lax.broadcasted_iota(jnp.int32, (n_grp, LANES), 1) < HEAD_DIM
    for a in range(2):
        for j in range(CMP_STRIDE // 2):
            token = lambda l, s: stage_ref[2 * a + s, pl.ds(l, n_grp, stride=CMP_STRIDE), :]
            even, odd = 2 * j, 2 * j + 1
            cmp_ref[a, 0, :, j * LANES:(j + 1) * LANES] = jnp.where(
                low_g, token(even, 0), token(odd, 1)).astype(BF16)
            cmp_ref[a, 1, :, j * LANES:(j + 1) * LANES] = jnp.where(
                low_g, token(even, 1), token(odd, 0)).astype(BF16)
    for n, y in enumerate(halves(11)):
        y = rope_n(y)
        sw = pltpu.roll(y, HEAD_DIM, 1)
        kdup_ref[:, (2 * n) * LANES:(2 * n + 1) * LANES] = jnp.where(low, y, sw).astype(BF16)
        kdup_ref[:, (2 * n + 1) * LANES:(2 * n + 2) * LANES] = jnp.where(low, sw, y).astype(BF16)


def _in_proj(x2, pos2, norm_w, w_main, w_t, tab, tm):
    T, D = x2.shape
    nv = 2 * KV_WIDTH
    ng = w_t.shape[0] - nv
    grp_w = CMP_STRIDE * HEAD_DIM
    row = lambda w: pl.BlockSpec((tm, w), lambda i: (i, 0))
    col = lambda h: pl.BlockSpec((h, tm), lambda i: (0, i))
    full = lambda a: pl.BlockSpec(a.shape, lambda i: (0,) * a.ndim)
    return pl.pallas_call(
        _in_proj_kernel,
        grid=(T // tm,),
        in_specs=[row(D), row(1), full(norm_w), full(w_main), full(w_t), full(tab)],
        out_specs=[row(3 * RET_WIDTH), row(RET_WIDTH), row(NSA_Q_HEADS * LANES),
                   pl.BlockSpec((2, NSA_KV_HEADS, tm // CMP_STRIDE, grp_w), lambda i: (0, 0, i, 0)),
                   row(4 * LANES), col(nv), col(ng)],
        out_shape=[jax.ShapeDtypeStruct((T, 3 * RET_WIDTH), BF16),
                   jax.ShapeDtypeStruct((T, RET_WIDTH), F32),
                   jax.ShapeDtypeStruct((T, NSA_Q_HEADS * LANES), BF16),
                   jax.ShapeDtypeStruct((2, NSA_KV_HEADS, T // CMP_STRIDE, grp_w), BF16),
                   jax.ShapeDtypeStruct((T, 4 * LANES), BF16),
                   jax.ShapeDtypeStruct((nv, T), BF16),
                   jax.ShapeDtypeStruct((ng, T), F32)],
        scratch_shapes=[pltpu.VMEM((4, tm, LANES), F32)],
        compiler_params=_params("parallel"),
    )(x2, pos2, norm_w, w_main, w_t, tab)


def _in_proj_weights(w_in):
    w_main = jnp.concatenate([w_in[:, :_KSL], w_in[:, _KSL:_VSL], w_in[:, _KWN:_VWN]], axis=1)
    gate_cols = np.full((NSA_KV_HEADS * GATE_ROWS,), -1)
    for g in range(NSA_KV_HEADS):
        for r in range(NSA_GROUP):
            for br in range(N_BRANCH):
                gate_cols[g * GATE_ROWS + r * N_BRANCH + br] = _NG + (g * NSA_GROUP + r) * N_BRANCH + br
    w_gate = jnp.where(gate_cols[None, :] >= 0, w_in[:, np.maximum(gate_cols, 0)], 0.0)
    w_t = jnp.concatenate([w_in[:, _VSL:_KWN], w_in[:, _VWN:_NG], w_gate], axis=1).T
    return w_main.astype(BF16), w_t.astype(BF16)


def _retention_kernel(q_ref, k_ref, v_ref, rg_ref, gnw_ref, dec_ref, qd_ref, kd_ref, cdm_ref,
                      o_ref, state_ref, *, n_chunks, unroll):
    C = RET_CHUNK
    lane = lax.broadcasted_iota(jnp.int32, (C, LANES), 1)
    low = lane < HEAD_DIM
    r_i = lax.broadcasted_iota(jnp.int32, (LANES, LANES), 0)
    c_i = lax.broadcasted_iota(jnp.int32, (LANES, LANES), 1)
    same_head = (r_i < HEAD_DIM) == (c_i < HEAD_DIM)
    state_ref[...] = jnp.zeros_like(state_ref)

    def group(gi, carry):
        rows = [pl.multiple_of((gi * unroll + u) * C, C) for u in range(unroll)]
        q = [q_ref[pl.ds(r0, C), :] for r0 in rows]
        k = [k_ref[pl.ds(r0, C), :] for r0 in rows]
        v = [v_ref[pl.ds(r0, C), :] for r0 in rows]
        zero = jnp.zeros_like(q[0])
        s0 = [_dot_nt(jnp.where(low, q[u], zero), k[u]) for u in range(unroll)]
        s1 = [_dot_nt(jnp.where(low, zero, q[u]), k[u]) for u in range(unroll)]
        kv = [_dot_tn((k[u].astype(F32) * kd_ref[...]).astype(BF16), v[u]) for u in range(unroll)]

        p0 = [(s0[u] * dec_ref[0]).astype(BF16) for u in range(unroll)]
        p1 = [(s1[u] * dec_ref[1]).astype(BF16) for u in range(unroll)]
        states = [state_ref[...]]
        for u in range(unroll):
            states.append(states[u] * cdm_ref[0] + jnp.where(same_head, kv[u], 0.0))
        state_ref[...] = states[unroll]

        o0 = [_dot(p0[u], v[u]) for u in range(unroll)]
        o1 = [_dot(p1[u], v[u]) for u in range(unroll)]
        oc = [_dot((q[u].astype(F32) * qd_ref[...]).astype(BF16), states[u].astype(BF16))
              for u in range(unroll)]

        inv_d = 1.0 / HEAD_DIM
        for u in range(unroll):
            ro = jnp.where(low, o0[u], o1[u]) + oc[u]
            sum0 = jnp.sum(jnp.where(low, ro, 0.0), axis=-1, keepdims=True)
            sum1 = jnp.sum(jnp.where(low, 0.0, ro), axis=-1, keepdims=True)
            d = ro - jnp.where(low, sum0, sum1) * inv_d
            dd = d * d
            var0 = jnp.sum(jnp.where(low, dd, 0.0), axis=-1, keepdims=True)
            var1 = jnp.sum(jnp.where(low, 0.0, dd), axis=-1, keepdims=True)
            var = jnp.where(low, var0, var1) * inv_d
            y = d * lax.rsqrt(var + EPS) * gnw_ref[...]
            g = rg_ref[pl.ds(rows[u], C), :]
            o_ref[pl.ds(rows[u], C), :] = (y * (g * jax.nn.sigmoid(g))).astype(BF16)
        return carry

    lax.fori_loop(0, n_chunks // unroll, group, 0)


def _retention(ret_qkv, rg, gn_w, dec, qd, kd, cdm, B, S):
    T = B * S
    n_pairs = RET_HEADS // HEADS_PER_VREG
    seq = lambda off: pl.BlockSpec((S, LANES), lambda b, hp: (b, off + hp))
    return pl.pallas_call(
        functools.partial(_retention_kernel, n_chunks=S // RET_CHUNK,
                          unroll=_row_tile(S // RET_CHUNK, 4)),
        grid=(B, n_pairs),
        in_specs=[seq(0), seq(n_pairs), seq(2 * n_pairs), seq(0),
                  pl.BlockSpec((1, LANES), lambda b, hp: (0, hp)),
                  pl.BlockSpec((HEADS_PER_VREG, RET_CHUNK, RET_CHUNK), lambda b, hp: (hp, 0, 0)),
                  pl.BlockSpec((RET_CHUNK, LANES), lambda b, hp: (0, hp)),
                  pl.BlockSpec((RET_CHUNK, LANES), lambda b, hp: (0, hp)),
                  pl.BlockSpec((1, LANES, LANES), lambda b, hp: (hp, 0, 0))],
        out_specs=seq(0),
        out_shape=jax.ShapeDtypeStruct((T, RET_WIDTH), BF16),
        scratch_shapes=[pltpu.VMEM((LANES, LANES), F32)],
        compiler_params=_params("parallel", "parallel"),
    )(ret_qkv, ret_qkv, ret_qkv, rg, gn_w, dec, qd, kd, cdm)


def _retention_constants():
    H, C = RET_HEADS, RET_CHUNK
    log_g = jnp.log1p(-jnp.exp2(-5.0 - jnp.arange(H, dtype=F32)))
    pos = jnp.arange(C, dtype=F32)
    diff = pos[:, None] - pos[None, :]
    dec = jnp.where(diff >= 0, jnp.exp(jnp.maximum(diff, 0.0) * log_g[:, None, None]), 0.0)
    q_decay = jnp.exp((pos + 1.0) * log_g[:, None]).T
    k_decay = jnp.exp((C - 1.0 - pos) * log_g[:, None]).T
    chunk_decay = jnp.exp(C * log_g)
    qd = jnp.repeat(q_decay, HEAD_DIM, axis=1)
    kd = jnp.repeat(k_decay, HEAD_DIM, axis=1)
    cd_rows = jnp.repeat(chunk_decay, HEAD_DIM).reshape(H // HEADS_PER_VREG, LANES, 1)
    head_of = jnp.arange(LANES) // HEAD_DIM
    same = (head_of[:, None] == head_of[None, :]).astype(F32)
    return dec, qd, kd, cd_rows * same[None]


def _compress_kernel(tk_ref, tv_ref, pos_ref, w1_ref, w2k_ref, w2vt_ref, ck_ref, cvt_ref):
    pos = jnp.broadcast_to(pos_ref[...], (8, pos_ref.shape[1])).astype(BF16)

    def hidden(t, w1_ref_a):
        half = t.shape[1]
        bias = _dot(pos, w1_ref_a[...])[0:1, :]
        first = _dot(t, w1_ref_a[:half, :])
        second = _dot(t, w1_ref_a[half:, :])
        n = t.shape[0]
        h = first + pltpu.roll(second, n - 1, 0) + bias
        return jax.nn.gelu(h).astype(BF16)

    ck_ref[0, 0] = _dot(hidden(tk_ref[0, 0], w1_ref.at[0]), w2k_ref[...]).astype(BF16)
    cvt_ref[0, 0] = _dot_nt(w2vt_ref[...], hidden(tv_ref[0, 0], w1_ref.at[1])).astype(BF16)


def _compress(cmp_tok, pos_flat, w1, w2k, w2vt, B):
    _, G, rows, width = cmp_tok.shape
    n = rows // B
    tok = lambda a: pl.BlockSpec((1, 1, n, width), lambda b, g: (a, g, b, 0))
    full = lambda a: pl.BlockSpec(a.shape, lambda b, g: (0,) * a.ndim)
    return pl.pallas_call(
        _compress_kernel,
        grid=(B, G),
        in_specs=[tok(0), tok(1), full(pos_flat), full(w1), full(w2k), full(w2vt)],
        out_specs=[pl.BlockSpec((1, 1, n, LANES), lambda b, g: (b, g, 0, 0)),
                   pl.BlockSpec((1, 1, HEAD_DIM, n), lambda b, g: (b, g, 0, 0))],
        out_shape=[jax.ShapeDtypeStruct((B, G, n, LANES), BF16),
                   jax.ShapeDtypeStruct((B, G, HEAD_DIM, n), BF16)],
        compiler_params=_params("parallel", "parallel"),
    )(cmp_tok, cmp_tok, pos_flat, w1, w2k, w2vt)


def _nsa_kernel(q_ref, ck_ref, cvt_ref, ks_ref, kw_ref, vst_ref, vwt_ref, gt_ref, ovt_ref, eb_ref,
                o_ref, m_ref, acc_ref, out_ref, score_ref, sa_ref, sb_ref, *, tq, top):
    R = NSA_GROUP
    D = HEAD_DIM
    t0 = pl.multiple_of(pl.program_id(2) * tq, tq)
    t_off = lax.broadcasted_iota(jnp.int32, (1, tq), 1)
    tcol = t0 + t_off
    k_off = lax.broadcasted_iota(jnp.int32, (tq, 1), 0)
    low = lax.broadcasted_iota(jnp.int32, (tq, LANES), 1) < D
    ones_rows = jnp.ones((BF16_ROWS, tq), BF16)
    gate = lambda r, br: gt_ref[r * N_BRANCH + br:r * N_BRANCH + br + 1, :]

    q =[q_ref[:, r * LANES:(r + 1) * LANES] for r in range(R)]

    def value_rows(vt_ref_, start):
        return jnp.concatenate([vt_ref_[:, pl.ds(start, tq)], ones_rows], axis=0)

    def more(r, s_t, vrows):
        m_old = m_ref[r]
        m_new = jnp.maximum(m_old, jnp.max(s_t, axis=0, keepdims=True))
        m_ref[r] = m_new
        acc_ref[r] = (jnp.exp2(m_old - m_new) * acc_ref[r]
                      + _dot(vrows, jnp.exp2(s_t - m_new).astype(BF16)))

    def mix(r, br):
        o = acc_ref[r, :D, :] * ((1.0 / acc_ref[r, D:D + 1, :]) * gate(r, br))
        out_ref[r * D:(r + 1) * D, :] = out_ref[r * D:(r + 1) * D, :] + o

    ck = ck_ref[0, 0]
    ncp = ck.shape[0]
    cvt = jnp.concatenate([cvt_ref[0, 0], jnp.ones((BF16_ROWS, ncp), BF16)], axis=0)
    cmp_end = lax.broadcasted_iota(jnp.int32, (ncp, 1), 0) * CMP_STRIDE + (CMP_BLOCK - 1)
    cmask = cmp_end <= tcol
    sees_any = jnp.where(tcol >= CMP_BLOCK - 1, 1.0, 0.0)
    n_sel = ovt_ref.shape[0]
    imp_t = jnp.zeros((n_sel, tq), F32)
    scores = [_dot_nt(ck, q[r]) for r in range(R)]
    for r in range(R):
        s = jnp.where(cmask, scores[r], MASKED)
        e = jnp.exp2(s - jnp.max(s, axis=0, keepdims=True)).astype(BF16)
        acc = _dot(cvt, e)
        inv_l = sees_any * (1.0 / acc[D:D + 1, :])
        out_ref[r * D:(r + 1) * D, :] = acc[:D, :] * (inv_l * gate(r, 0))
        imp_t = imp_t + _dot(ovt_ref[...], e) * inv_l

    blk_i = lax.broadcasted_iota(jnp.int32, (n_sel, tq), 0)
    cur = tcol // SEL_BLOCK
    valid = blk_i * SEL_BLOCK <= tcol
    forced = (blk_i == 0) | (blk_i == cur) | (blk_i == cur - 1)
    score_ref[...] = jnp.where(forced, jnp.inf, jnp.where(valid, imp_t, -jnp.inf))
    not_sel = []
    for jg in range(n_sel // 8):
        mine = score_ref[jg * 8:(jg + 1) * 8, :]
        blk_g = jg * 8 + lax.broadcasted_iota(jnp.int32, (8, tq), 0)
        ahead = jnp.zeros((8, tq), F32)
        for kb in range(n_sel):
            other = score_ref[kb:kb + 1, :]
            if kb < jg * 8:
                ahead = ahead + jnp.where(other >= mine, 1.0, 0.0)
            elif kb >= (jg + 1) * 8:
                ahead = ahead + jnp.where(other > mine, 1.0, 0.0)
            else:
                ahead = ahead + jnp.where(blk_g > kb, jnp.where(other >= mine, 1.0, 0.0),
                                          jnp.where(other > mine, 1.0, 0.0))
        not_sel.append(jnp.where(ahead < top, 0.0, 1.0))
    pad_rows = LANES - D - n_sel
    not_sel_t = jnp.concatenate([jnp.zeros((D, tq), F32)] + not_sel
                                + ([jnp.zeros((pad_rows, tq), F32)] if pad_rows else []), axis=0)
    not_sel_q = not_sel_t.T.astype(BF16)
    qa = [jnp.where(low, q[r], not_sel_q) for r in range(R)]

    def keys_aug(start):
        return jnp.where(low, ks_ref[0, pl.ds(start, tq), :], eb_ref[pl.ds(start, tq), :])

    def issue(c, buf_ref):
        ka = keys_aug(pl.multiple_of(c * tq, tq))
        for r in range(R):
            buf_ref[r] = _dot_nt(ka, qa[r])

    def consume(c, buf_ref, own):
        vr = value_rows(vst_ref, pl.multiple_of(c * tq, tq))
        for r in range(R):
            more(r, jnp.where(causal, buf_ref[r], MASKED) if own else buf_ref[r], vr)

    causal = k_off <= t_off
    n_plain = pl.program_id(2)
    for r in range(R):
        m_ref[r] = jnp.full((1, tq), LOWEST, F32)
        acc_ref[r] = jnp.zeros(acc_ref.shape[1:], F32)
    issue(0, sa_ref)

    def sel_pair(k, carry):
        issue(2 * k + 1, sb_ref)
        consume(2 * k, sa_ref, False)
        issue(2 * k + 2, sa_ref)
        consume(2 * k + 1, sb_ref, False)
        return carry

    lax.fori_loop(0, n_plain // 2, sel_pair, 0)

    @pl.when(n_plain % 2 == 1)
    def _():
        issue(n_plain, sb_ref)
        consume(n_plain - 1, sa_ref, False)
        consume(n_plain, sb_ref, True)

    @pl.when(n_plain % 2 == 0)
    def _():
        consume(n_plain, sa_ref, True)

    for r in range(R):
        mix(r, 1)

    flag_lane = lax.broadcasted_iota(jnp.int32, (tq, LANES), 1) == D
    one = jnp.ones((tq, LANES), BF16)
    qw = [jnp.where(flag_lane, one, q[r]) for r in range(R)]
    n_back = WINDOW // tq
    starts, scores = [], []
    for j in range(n_back + 1):
        starts.append(pl.multiple_of(jnp.maximum(t0 - j * tq, 0), tq))
        off = jnp.full((tq, LANES), jnp.where(t0 >= j * tq, 0.0, MASKED), F32).astype(BF16)
        kw = jnp.where(flag_lane, off, kw_ref[0, pl.ds(starts[j], tq), :])
        scores.append([_dot_nt(kw, qw[r]) for r in range(R)])
    for r in range(R):
        m_ref[r] = jnp.full((1, tq), LOWEST, F32)
        acc_ref[r] = jnp.zeros(acc_ref.shape[1:], F32)
    for j in range(n_back + 1):
        vr = value_rows(vwt_ref, starts[j])
        for r in range(R):
            s_t = scores[j][r]
            if j == 0:
                s_t = jnp.where(causal, s_t, MASKED)
            elif j == n_back:
                s_t = jnp.where(k_off > t_off, s_t, MASKED)
            more(r, s_t, vr)
    for r in range(R):
        mix(r, 2)

    o_ref[...] = out_ref[...].T.astype(BF16)


def _nsa(nq, ck, cvt, kdup, vt, gt, ovt, eb, B, S, tq):
    T = B * S
    G = NSA_KV_HEADS
    assert S % tq == 0 and WINDOW % tq == 0 and S // SEL_BLOCK <= LANES - HEAD_DIM
    nqt = S // tq
    ncp = ck.shape[2]
    kdup3 = kdup.reshape(B, S, kdup.shape[1])
    acc_rows = HEAD_DIM + BF16_ROWS
    keys = lambda off: pl.BlockSpec((1, S, LANES), lambda b, g, i: (b, 0, off + g))
    vals = lambda off: pl.BlockSpec((HEAD_DIM, S), lambda b, g, i: (off + g, b))
    full = lambda a: pl.BlockSpec(a.shape, lambda b, g, i: (0,) * a.ndim)
    return pl.pallas_call(
        functools.partial(_nsa_kernel, tq=tq, top=min(SEL_TOPK, S // SEL_BLOCK)),
        grid=(B, G, nqt),
        in_specs=[pl.BlockSpec((tq, NSA_GROUP * LANES), lambda b, g, i: (b * nqt + i, g)),
                  pl.BlockSpec((1, 1, ncp, LANES), lambda b, g, i: (b, g, 0, 0)),
                  pl.BlockSpec((1, 1, HEAD_DIM, ncp), lambda b, g, i: (b, g, 0, 0)),
                  keys(0), keys(G), vals(0), vals(G),
                  pl.BlockSpec((GATE_ROWS, tq), lambda b, g, i: (g, b * nqt + i)),
                  full(ovt), full(eb)],
        out_specs=pl.BlockSpec((tq, NSA_GROUP * HEAD_DIM), lambda b, g, i: (b * nqt + i, g)),
        out_shape=jax.ShapeDtypeStruct((T, NSA_WIDTH), BF16),
        scratch_shapes=[pltpu.VMEM((NSA_GROUP, 1, tq), F32),
                        pltpu.VMEM((NSA_GROUP, acc_rows, tq), F32),
                        pltpu.VMEM((NSA_GROUP * HEAD_DIM, tq), F32),
                        pltpu.VMEM((S // SEL_BLOCK, tq), F32),
                        pltpu.VMEM((NSA_GROUP, tq, tq), F32),
                        pltpu.VMEM((NSA_GROUP, tq, tq), F32)],
        compiler_params=_params("parallel", "parallel", "parallel"),
    )(nq, ck, cvt, kdup3, kdup3, vt, vt, gt, ovt, eb)


def _nsa_constants(S):
    n_cmp = (S - CMP_BLOCK) // CMP_STRIDE + 1
    ncp = S // CMP_STRIDE
    n_sel = S // SEL_BLOCK
    ci = np.arange(ncp)[None, :]
    sj = np.arange(n_sel)[:, None]
    ovt = ((ci * CMP_STRIDE < (sj + 1) * SEL_BLOCK)
           & (ci * CMP_STRIDE + CMP_BLOCK > sj * SEL_BLOCK) & (ci < n_cmp))
    eb = np.zeros((S, LANES), np.float32)
    eb[np.arange(S), HEAD_DIM + np.arange(S) // SEL_BLOCK] = MASKED
    return jnp.asarray(ovt, BF16), jnp.asarray(eb, BF16)


def _out_proj_kernel(ret_ref, nsa_ref, x_ref, wo_ref, nw_ref, h_ref, hn_ref):
    mix = _dot(ret_ref[...], wo_ref[:RET_WIDTH, :]) + _dot(nsa_ref[...], wo_ref[RET_WIDTH:, :])
    h = x_ref[...] + mix
    h_ref[...] = h
    ms = jnp.mean(h * h, axis=-1, keepdims=True)
    hn_ref[...] = (h * lax.rsqrt(ms + EPS) * nw_ref[...]).astype(BF16)


def _out_proj(ret_out, nsa_out, x2, w_out, norm_w, tm):
    T, D = x2.shape
    row = lambda w: pl.BlockSpec((tm, w), lambda i: (i, 0))
    full = lambda a: pl.BlockSpec(a.shape, lambda i: (0,) * a.ndim)
    return pl.pallas_call(
        _out_proj_kernel,
        grid=(T // tm,),
        in_specs=[row(RET_WIDTH), row(NSA_WIDTH), row(D), full(w_out), full(norm_w)],
        out_specs=[row(D), row(D)],
        out_shape=[jax.ShapeDtypeStruct((T, D), F32), jax.ShapeDtypeStruct((T, D), BF16)],
        compiler_params=_params("parallel"),
    )(ret_out, nsa_out, x2, w_out, norm_w)


def _ffn_up_kernel(hn_ref, wg_ref, wv_ref, cw_ref, cb_ref, act_ref, tail_ref, *, fc):
    @pl.when(pl.program_id(1) == 0)
    def _():
        tail_ref[...] = jnp.zeros_like(tail_ref)

    hn = hn_ref[...]
    tm = hn.shape[0]
    row = lax.broadcasted_iota(jnp.int32, (tm, fc), 0)
    for c in range(wg_ref.shape[1] // fc):
        sl = slice(c * fc, (c + 1) * fc)
        g = _dot(hn, wg_ref[:, sl])
        val = _dot(hn, wv_ref[:, sl])
        tail = tail_ref[:, sl]
        g1 = jnp.where(row == 0, tail[7:8, :], pltpu.roll(g, 1, 0))
        g2 = jnp.where(row == 0, tail[6:7, :],
                       jnp.where(row == 1, tail[7:8, :], pltpu.roll(g, 2, 0)))
        tail_ref[:, sl] = g[tm - 8:, :]
        cw = cw_ref[:, sl]
        y = cb_ref[:, sl] + cw[0:1, :] * g2 + cw[1:2, :] * g1 + cw[2:3, :] * g
        act_ref[:, sl] = (y * jax.nn.sigmoid(y) * val).astype(BF16)


def _ffn_up(hn, wg, wv, conv_w, conv_b, B, S, tm):
    T, D = hn.shape
    F = wg.shape[1]
    nt = S // tm
    full = lambda a: pl.BlockSpec(a.shape, lambda b, j: (0,) * a.ndim)
    return pl.pallas_call(
        functools.partial(_ffn_up_kernel, fc=256),
        grid=(B, nt),
        in_specs=[pl.BlockSpec((tm, D), lambda b, j: (b * nt + j, 0)),
                  full(wg), full(wv), full(conv_w), full(conv_b)],
        out_specs=pl.BlockSpec((tm, F), lambda b, j: (b * nt + j, 0)),
        out_shape=jax.ShapeDtypeStruct((T, F), BF16),
        scratch_shapes=[pltpu.VMEM((8, F), F32)],
        compiler_params=_params("arbitrary", "arbitrary"),
    )(hn, wg, wv, conv_w, conv_b)


def _ffn_down_kernel(act_ref, h_ref, p_ref, wd_ref, pg_ref, pw_ref, fw_ref, o_ref, *, final):
    h = h_ref[...] + _dot(act_ref[...], wd_ref[...])
    gate = jax.nn.sigmoid(_dot(h.astype(BF16), pg_ref[...]))
    h = h + gate * _dot(p_ref[...].astype(BF16), pw_ref[...])
    if final:
        ms = jnp.mean(h * h, axis=-1, keepdims=True)
        h = h * lax.rsqrt(ms + EPS) * fw_ref[...]
    o_ref[...] = h


def _ffn_down(act, h1, p2, wd, pg, pw, fw, tm, final):
    T, D = h1.shape
    row = lambda w: pl.BlockSpec((tm, w), lambda i: (i, 0))
    full = lambda a: pl.BlockSpec(a.shape, lambda i: (0,) * a.ndim)
    return pl.pallas_call(
        functools.partial(_ffn_down_kernel, final=final),
        grid=(T // tm,),
        in_specs=[row(act.shape[1]), row(D), row(p2.shape[1]), full(wd), full(pg), full(pw), full(fw)],
        out_specs=row(D),
        out_shape=jax.ShapeDtypeStruct((T, D), F32),
        compiler_params=_params("parallel"),
    )(act, h1, p2, wd, pg, pw, fw)


def _rope_table():
    lane = np.arange(LANES) % HEAD_DIM
    half_r = HEAD_DIM // 2
    inv_r = jnp.power(jnp.float32(RET_THETA), -jnp.arange(half_r, dtype=F32) / half_r)
    half_n = ROPE_DIM // 2
    inv_n = jnp.power(jnp.float32(ROPE_THETA), -jnp.arange(half_n, dtype=F32) / half_n)
    freq = jnp.concatenate([inv_r, inv_n, jnp.zeros((LANES - half_r - half_n,), F32)])
    rows = [freq,
            jnp.asarray(np.where(lane < half_r, -1.0, 1.0), F32),
            jnp.asarray(np.where(lane < half_n, -1.0, 1.0), F32)]
    return jnp.concatenate([jnp.stack(rows), jnp.zeros((5, LANES), F32)], axis=0)


def _layer(h2, pos2, B, S, norm_mix_w, w_in, ret_gn_w, cmp_pos, cmp_k_w1, cmp_k_w2,
           cmp_v_w1, cmp_v_w2, w_out, norm_ffn_w, ffn_w_up, ffn_conv_w, ffn_conv_b, d_ff):
    T, D = h2.shape
    tm = _row_tile(S, 512)

    w_main, w_t = _in_proj_weights(w_in)
    ret_qkv, rg, nq, cmp_tok, kdup, vt, gt = _in_proj(
        h2, pos2, norm_mix_w.reshape(1, D), w_main, w_t, _rope_table(), tm)

    dec, qd, kd, cdm = _retention_constants()
    ret_out = _retention(ret_qkv, rg, ret_gn_w.reshape(1, RET_WIDTH), dec, qd, kd, cdm, B, S)

    w1 = jnp.stack([cmp_k_w1, cmp_v_w1]).astype(BF16)
    w2k = jnp.tile(cmp_k_w2, (1, HEADS_PER_VREG)).astype(BF16)
    ck, cvt = _compress(cmp_tok, cmp_pos.reshape(1, CMP_BLOCK * HEAD_DIM), w1, w2k,
                        cmp_v_w2.T.astype(BF16), B)

    ovt, eb = _nsa_constants(S)
    nsa_out = _nsa(nq, ck, cvt, kdup, vt, gt, ovt, eb, B, S, tq=256)

    h1, hn = _out_proj(ret_out, nsa_out, h2, w_out.astype(BF16), norm_ffn_w.reshape(1, D), tm)

    wg = ffn_w_up[:, :d_ff].astype(BF16)
    wv = ffn_w_up[:, d_ff:].astype(BF16)
    conv_w = jnp.pad(ffn_conv_w, ((0, 8 - ffn_conv_w.shape[0]), (0, 0)))
    act = _ffn_up(hn, wg, wv, conv_w, ffn_conv_b.reshape(1, d_ff), B, S, tm)
    return act, h1


def kernel(x, p, positions, norm_mix_w, w_in, ret_gn_w, cmp_pos, cmp_k_w1, cmp_k_w2, cmp_v_w1, cmp_v_w2, w_out, norm_ffn_w, ffn_w_up, ffn_conv_w, ffn_conv_b, ffn_w_down, ple_w, ple_gate_w, final_norm_w):
    B, S, D = x.shape
    T = B * S
    depth = w_in.shape[0]
    tm = _row_tile(S, 512)
    h = x.reshape(T, D)
    pos2 = positions.reshape(T, 1).astype(jnp.int32)
    for i in range(depth):
        act, h1 = _layer(h, pos2, B, S, norm_mix_w[i], w_in[i], ret_gn_w[i],
                         cmp_pos[i], cmp_k_w1[i], cmp_k_w2[i], cmp_v_w1[i], cmp_v_w2[i], w_out[i],
                         norm_ffn_w[i], ffn_w_up[i], ffn_conv_w[i], ffn_conv_b[i],
                         ffn_w_down.shape[1])
        h = _ffn_down(act, h1, p[i].reshape(T, -1), ffn_w_down[i].astype(BF16),
                      ple_gate_w[i].astype(BF16), ple_w[i].astype(BF16),
                      final_norm_w.reshape(1, D), tm, final=i == depth - 1)
    return h.reshape(B, S, D)
```

```python
import functools

import numpy as np
import jax
import jax.numpy as jnp
from jax import lax
from jax.experimental import pallas as pl
from jax.experimental.pallas import tpu as pltpu

F32 = jnp.float32
BF16 = jnp.bfloat16

LANES = 128
BF16_ROWS = 16
HEAD_DIM = 64
HEADS_PER_VREG = LANES // HEAD_DIM
RET_HEADS = 8
NSA_Q_HEADS = 8
NSA_KV_HEADS = 2
NSA_GROUP = NSA_Q_HEADS // NSA_KV_HEADS
RET_WIDTH = RET_HEADS * HEAD_DIM
NSA_WIDTH = NSA_Q_HEADS * HEAD_DIM
KV_WIDTH = NSA_KV_HEADS * HEAD_DIM
RET_CHUNK = 128
RET_THETA = 10000.0
ROPE_THETA = 500000.0
ROPE_DIM = HEAD_DIM // 4
CMP_BLOCK = 32
CMP_STRIDE = 16
SEL_BLOCK = 64
SEL_TOPK = 16
WINDOW = 512
N_BRANCH = 3
GATE_ROWS = 16
RANK_STEP = 16
EPS = 1e-6
QK_SCALE = HEAD_DIM ** -0.5
LOG2_E = 1.4426950408889634
MASKED = -1e30
LOWEST = -3e38

_OFF = np.cumsum([0, RET_WIDTH, RET_WIDTH, RET_WIDTH, RET_WIDTH, NSA_WIDTH,
                  KV_WIDTH, KV_WIDTH, KV_WIDTH, KV_WIDTH, KV_WIDTH, KV_WIDTH])
(_RQ, _RK, _RV, _RG, _NQ, _KC, _VC, _KSL, _VSL, _KWN, _VWN, _NG) = (int(v) for v in _OFF)

VMEM_LIMIT = 56 * 1024 * 1024


def _dot(a, b):
    return jnp.dot(a, b, preferred_element_type=F32)


def _dot_nt(a, b):
    return lax.dot_general(a, b, (((1,), (1,)), ((), ())), preferred_element_type=F32)


def _dot_tn(a, b):
    return lax.dot_general(a, b, (((0,), (0,)), ((), ())), preferred_element_type=F32)


def _params(*semantics):
    return pltpu.CompilerParams(dimension_semantics=semantics, vmem_limit_bytes=VMEM_LIMIT)


def _row_tile(n, pref):
    t = pref
    while n % t:
        t //= 2
    return t


def _rope(y, cos, sin_signed, first_half, half):
    nxt = pltpu.roll(y, LANES - half, 1)
    prv = pltpu.roll(y, half, 1)
    return y * cos + jnp.where(first_half, nxt, prv) * sin_signed


def _in_proj_kernel(x_ref, pos_ref, nw_ref, w_ref, wt_ref, tab_ref,
                    ret_ref, rg_ref, nq_ref, cmp_ref, kdup_ref, vt_ref, gt_ref, stage_ref):
    x = x_ref[...]
    tm = x.shape[0]
    ms = jnp.mean(x * x, axis=-1, keepdims=True)
    xn = (x * lax.rsqrt(ms + EPS) * nw_ref[...]).astype(BF16)

    pos = pos_ref[...].astype(F32)
    tab = tab_ref[...]
    lane = lax.broadcasted_iota(jnp.int32, (tm, LANES), 1)
    in_head = lane % HEAD_DIM
    low = lane < HEAD_DIM
    half_r, half_n = HEAD_DIM // 2, ROPE_DIM // 2
    ang = pos * tab[0:1, :]
    first_r = in_head < half_r
    first_n = in_head < half_n
    rot_n = in_head < ROPE_DIM

    def spread_r(c):
        c = jnp.where(lane < half_r, c, pltpu.roll(c, half_r, 1))
        return jnp.where(low, c, pltpu.roll(c, HEAD_DIM, 1))

    def spread_n(c, rest):
        c = pltpu.roll(c, LANES - half_r, 1)
        c = jnp.where(lane < half_n, c, pltpu.roll(c, half_n, 1))
        c = jnp.where(low, c, pltpu.roll(c, HEAD_DIM, 1))
        return jnp.where(rot_n, c, rest)

    cos_a, sin_a = jnp.cos(ang), jnp.sin(ang)
    cos_r, sin_r = spread_r(cos_a), spread_r(sin_a) * tab[1:2, :]
    cos_n, sin_n = spread_n(cos_a, 1.0), spread_n(sin_a, 0.0) * tab[2:3, :]

    rope_r = lambda y: _rope(y, cos_r, sin_r, first_r, HEAD_DIM // 2)
    rope_n = lambda y: _rope(y, cos_n, sin_n, first_n, ROPE_DIM // 2)

    def halves(ci):
        y = _dot(xn, w_ref[:, ci * 256:(ci + 1) * 256])
        return y[:, :LANES], y[:, LANES:]

    for ci in range(2):
        for h, y in enumerate(halves(4 + ci)):
            c0 = 2 * RET_WIDTH + ci * 256 + h * LANES
            ret_ref[:, c0:c0 + LANES] = y.astype(BF16)
    t = _dot_nt(wt_ref[...], xn)
    nv = vt_ref.shape[0]
    vt_ref[...] = t[:nv, :].astype(BF16)
    gt_ref[...] = jax.nn.sigmoid(t[nv:, :])
    for ci in range(2):
        for h, y in enumerate(halves(6 + ci)):
            c0 = ci * 256 + h * LANES
            rg_ref[:, c0:c0 + LANES] = y
    for ci in range(2):
        for h, y in enumerate(halves(ci)):
            c0 = ci * 256 + h * LANES
            ret_ref[:, c0:c0 + LANES] = rope_r(y).astype(BF16)
    for ci in range(2):
        for h, y in enumerate(halves(2 + ci)):
            c0 = RET_WIDTH + ci * 256 + h * LANES
            ret_ref[:, c0:c0 + LANES] = (rope_r(y) * QK_SCALE).astype(BF16)
    for ci in range(2):
        for h, y in enumerate(halves(8 + ci)):
            y = rope_n(y) * (QK_SCALE * LOG2_E)
            c0 = (ci * 2 + h) * 2 * LANES
            nq_ref[:, c0:c0 + LANES] = jnp.where(low, y, 0.0).astype(BF16)
            nq_ref[:, c0 + LANES:c0 + 2 * LANES] = jnp.where(
                low, pltpu.roll(y, HEAD_DIM, 1), 0.0).astype(BF16)
    kc, vc = halves(10)
    for a, y in enumerate((rope_n(kc), vc)):
        stage_ref[2 * a] = y
        stage_ref[2 * a + 1] = pltpu.roll(y, HEAD_DIM, 1)
    n_grp = tm // CMP_STRIDE
    low_g = lax.broadcasted_iota(jnp.int32, (n_grp, LANES), 1) < HEAD_DIM
    for a in range(2):
        for j in range(CMP_STRIDE // 2):
            token = lambda l, s: stage_ref[2 * a + s, pl.ds(l, n_grp, stride=CMP_STRIDE), :]
            even, odd = 2 * j, 2 * j + 1
            cmp_ref[a, 0, :, j * LANES:(j + 1) * LANES] = jnp.where(
                low_g, token(even, 0), token(odd, 1)).astype(BF16)
            cmp_ref[a, 1, :, j * LANES:(j + 1) * LANES] = jnp.where(
                low_g, token(even, 1), token(odd, 0)).astype(BF16)
    for n, y in enumerate(halves(11)):
        y = rope_n(y)
        sw = pltpu.roll(y, HEAD_DIM, 1)
        kdup_ref[:, (2 * n) * LANES:(2 * n + 1) * LANES] = jnp.where(low, y, sw).astype(BF16)
        kdup_ref[:, (2 * n + 1) * LANES:(2 * n + 2) * LANES] = jnp.where(low, sw, y).astype(BF16)


def _in_proj(x2, pos2, norm_w, w_main, w_t, tab, tm):
    T, D = x2.shape
    nv = 2 * KV_WIDTH
    ng = w_t.shape[0] - nv
    grp_w = CMP_STRIDE * HEAD_DIM
    row = lambda w: pl.BlockSpec((tm, w), lambda i: (i, 0))
    col = lambda h: pl.BlockSpec((h, tm), lambda i: (0, i))
    full = lambda a: pl.BlockSpec(a.shape, lambda i: (0,) * a.ndim)
    return pl.pallas_call(
        _in_proj_kernel,
        grid=(T // tm,),
        in_specs=[row(D), row(1), full(norm_w), full(w_main), full(w_t), full(tab)],
        out_specs=[row(3 * RET_WIDTH), row(RET_WIDTH), row(NSA_Q_HEADS * LANES),
                   pl.BlockSpec((2, NSA_KV_HEADS, tm // CMP_STRIDE, grp_w), lambda i: (0, 0, i, 0)),
                   row(4 * LANES), col(nv), col(ng)],
        out_shape=[jax.ShapeDtypeStruct((T, 3 * RET_WIDTH), BF16),
                   jax.ShapeDtypeStruct((T, RET_WIDTH), F32),
                   jax.ShapeDtypeStruct((T, NSA_Q_HEADS * LANES), BF16),
                   jax.ShapeDtypeStruct((2, NSA_KV_HEADS, T // CMP_STRIDE, grp_w), BF16),
                   jax.ShapeDtypeStruct((T, 4 * LANES), BF16),
                   jax.ShapeDtypeStruct((nv, T), BF16),
                   jax.ShapeDtypeStruct((ng, T), F32)],
        scratch_shapes=[pltpu.VMEM((4, tm, LANES), F32)],
        compiler_params=_params("parallel"),
    )(x2, pos2, norm_w, w_main, w_t, tab)


def _in_proj_weights(w_in):
    w_main = jnp.concatenate([w_in[:, :_KSL], w_in[:, _KSL:_VSL], w_in[:, _KWN:_VWN]], axis=1)
    gate_cols = np.full((NSA_KV_HEADS * GATE_ROWS,), -1)
    for g in range(NSA_KV_HEADS):
        for r in range(NSA_GROUP):
            for br in range(N_BRANCH):
                gate_cols[g * GATE_ROWS + r * N_BRANCH + br] = _NG + (g * NSA_GROUP + r) * N_BRANCH + br
    w_gate = jnp.where(gate_cols[None, :] >= 0, w_in[:, np.maximum(gate_cols, 0)], 0.0)
    w_t = jnp.concatenate([w_in[:, _VSL:_KWN], w_in[:, _VWN:_NG], w_gate], axis=1).T
    return w_main.astype(BF16), w_t.astype(BF16)


def _retention_kernel(q_ref, k_ref, v_ref, rg_ref, gnw_ref, dec_ref, qd_ref, kd_ref, cdm_ref,
                      o_ref, state_ref, *, n_chunks, unroll):
    C = RET_CHUNK
    lane = lax.broadcasted_iota(jnp.int32, (C, LANES), 1)
    low = lane < HEAD_DIM
    r_i = lax.broadcasted_iota(jnp.int32, (LANES, LANES), 0)
    c_i = lax.broadcasted_iota(jnp.int32, (LANES, LANES), 1)
    same_head = (r_i < HEAD_DIM) == (c_i < HEAD_DIM)
    state_ref[...] = jnp.zeros_like(state_ref)

    def group(gi, carry):
        rows = [pl.multiple_of((gi * unroll + u) * C, C) for u in range(unroll)]
        q = [q_ref[pl.ds(r0, C), :] for r0 in rows]
        k = [k_ref[pl.ds(r0, C), :] for r0 in rows]
        v = [v_ref[pl.ds(r0, C), :] for r0 in rows]
        zero = jnp.zeros_like(q[0])
        s0 = [_dot_nt(jnp.where(low, q[u], zero), k[u]) for u in range(unroll)]
        s1 = [_dot_nt(jnp.where(low, zero, q[u]), k[u]) for u in range(unroll)]
        kv = [_dot_tn((k[u].astype(F32) * kd_ref[...]).astype(BF16), v[u]) for u in range(unroll)]

        p0 = [(s0[u] * dec_ref[0]).astype(BF16) for u in range(unroll)]
        p1 = [(s1[u] * dec_ref[1]).astype(BF16) for u in range(unroll)]
        states = [state_ref[...]]
        for u in range(unroll):
            states.append(states[u] * cdm_ref[0] + jnp.where(same_head, kv[u], 0.0))
        state_ref[...] = states[unroll]

        o0 = [_dot(p0[u], v[u]) for u in range(unroll)]
        o1 = [_dot(p1[u], v[u]) for u in range(unroll)]
        oc = [_dot((q[u].astype(F32) * qd_ref[...]).astype(BF16), states[u].astype(BF16))
              for u in range(unroll)]

        inv_d = 1.0 / HEAD_DIM
        for u in range(unroll):
            ro = jnp.where(low, o0[u], o1[u]) + oc[u]
            sum0 = jnp.sum(jnp.where(low, ro, 0.0), axis=-1, keepdims=True)
            sum1 = jnp.sum(jnp.where(low, 0.0, ro), axis=-1, keepdims=True)
            d = ro - jnp.where(low, sum0, sum1) * inv_d
            dd = d * d
            var0 = jnp.sum(jnp.where(low, dd, 0.0), axis=-1, keepdims=True)
            var1 = jnp.sum(jnp.where(low, 0.0, dd), axis=-1, keepdims=True)
            var = jnp.where(low, var0, var1) * inv_d
            y = d * lax.rsqrt(var + EPS) * gnw_ref[...]
            g = rg_ref[pl.ds(rows[u], C), :]
            o_ref[pl.ds(rows[u], C), :] = (y * (g * jax.nn.sigmoid(g))).astype(BF16)
        return carry

    lax.fori_loop(0, n_chunks // unroll, group, 0)


def _retention(ret_qkv, rg, gn_w, dec, qd, kd, cdm, B, S):
    T = B * S
    n_pairs = RET_HEADS // HEADS_PER_VREG
    seq = lambda off: pl.BlockSpec((S, LANES), lambda b, hp: (b, off + hp))
    return pl.pallas_call(
        functools.partial(_retention_kernel, n_chunks=S // RET_CHUNK,
                          unroll=_row_tile(S // RET_CHUNK, 8)),
        grid=(B, n_pairs),
        in_specs=[seq(0), seq(n_pairs), seq(2 * n_pairs), seq(0),
                  pl.BlockSpec((1, LANES), lambda b, hp: (0, hp)),
                  pl.BlockSpec((HEADS_PER_VREG, RET_CHUNK, RET_CHUNK), lambda b, hp: (hp, 0, 0)),
                  pl.BlockSpec((RET_CHUNK, LANES), lambda b, hp: (0, hp)),
                  pl.BlockSpec((RET_CHUNK, LANES), lambda b, hp: (0, hp)),
                  pl.BlockSpec((1, LANES, LANES), lambda b, hp: (hp, 0, 0))],
        out_specs=seq(0),
        out_shape=jax.ShapeDtypeStruct((T, RET_WIDTH), BF16),
        scratch_shapes=[pltpu.VMEM((LANES, LANES), F32)],
        compiler_params=_params("parallel", "parallel"),
    )(ret_qkv, ret_qkv, ret_qkv, rg, gn_w, dec, qd, kd, cdm)


def _retention_constants():
    H, C = RET_HEADS, RET_CHUNK
    log_g = jnp.log1p(-jnp.exp2(-5.0 - jnp.arange(H, dtype=F32)))
    pos = jnp.arange(C, dtype=F32)
    diff = pos[:, None] - pos[None, :]
    dec = jnp.where(diff >= 0, jnp.exp(jnp.maximum(diff, 0.0) * log_g[:, None, None]), 0.0)
    q_decay = jnp.exp((pos + 1.0) * log_g[:, None]).T
    k_decay = jnp.exp((C - 1.0 - pos) * log_g[:, None]).T
    chunk_decay = jnp.exp(C * log_g)
    qd = jnp.repeat(q_decay, HEAD_DIM, axis=1)
    kd = jnp.repeat(k_decay, HEAD_DIM, axis=1)
    cd_rows = jnp.repeat(chunk_decay, HEAD_DIM).reshape(H // HEADS_PER_VREG, LANES, 1)
    head_of = jnp.arange(LANES) // HEAD_DIM
    same = (head_of[:, None] == head_of[None, :]).astype(F32)
    return dec, qd, kd, cd_rows * same[None]


def _compress_kernel(tk_ref, tv_ref, pos_ref, w1_ref, w2k_ref, w2vt_ref, ck_ref, cvt_ref):
    pos = jnp.broadcast_to(pos_ref[...], (8, pos_ref.shape[1])).astype(BF16)

    def hidden(t, w1_ref_a):
        half = t.shape[1]
        bias = _dot(pos, w1_ref_a[...])[0:1, :]
        first = _dot(t, w1_ref_a[:half, :])
        second = _dot(t, w1_ref_a[half:, :])
        n = t.shape[0]
        h = first + pltpu.roll(second, n - 1, 0) + bias
        return jax.nn.gelu(h).astype(BF16)

    ck_ref[0, 0] = _dot(hidden(tk_ref[0, 0], w1_ref.at[0]), w2k_ref[...]).astype(BF16)
    cvt_ref[0, 0] = _dot_nt(w2vt_ref[...], hidden(tv_ref[0, 0], w1_ref.at[1])).astype(BF16)


def _compress(cmp_tok, pos_flat, w1, w2k, w2vt, B):
    _, G, rows, width = cmp_tok.shape
    n = rows // B
    tok = lambda a: pl.BlockSpec((1, 1, n, width), lambda b, g: (a, g, b, 0))
    full = lambda a: pl.BlockSpec(a.shape, lambda b, g: (0,) * a.ndim)
    return pl.pallas_call(
        _compress_kernel,
        grid=(B, G),
        in_specs=[tok(0), tok(1), full(pos_flat), full(w1), full(w2k), full(w2vt)],
        out_specs=[pl.BlockSpec((1, 1, n, LANES), lambda b, g: (b, g, 0, 0)),
                   pl.BlockSpec((1, 1, HEAD_DIM, n), lambda b, g: (b, g, 0, 0))],
        out_shape=[jax.ShapeDtypeStruct((B, G, n, LANES), BF16),
                   jax.ShapeDtypeStruct((B, G, HEAD_DIM, n), BF16)],
        compiler_params=_params("parallel", "parallel"),
    )(cmp_tok, cmp_tok, pos_flat, w1, w2k, w2vt)


def _nsa_kernel(q_ref, ck_ref, cvt_ref, ks_ref, kw_ref, vst_ref, vwt_ref, gt_ref, ovt_ref, eb_ref,
                o_ref, m_ref, acc_ref, out_ref, score_ref, flag_ref, sa_ref, sb_ref, sw_ref,
                *, tq, top):
    R = NSA_GROUP
    D = HEAD_DIM
    t0 = pl.multiple_of(pl.program_id(2) * tq, tq)
    t_off = lax.broadcasted_iota(jnp.int32, (1, tq), 1)
    tcol = t0 + t_off
    k_off = lax.broadcasted_iota(jnp.int32, (tq, 1), 0)
    low = lax.broadcasted_iota(jnp.int32, (tq, LANES), 1) < D
    ones_rows = jnp.ones((BF16_ROWS, tq), BF16)
    gate = lambda r, br: gt_ref[r * N_BRANCH + br:r * N_BRANCH + br + 1, :]

    q =[q_ref[:, r * LANES:(r + 1) * LANES] for r in range(R)]

    def value_rows(vt_ref_, start):
        return jnp.concatenate([vt_ref_[:, pl.ds(start, tq)], ones_rows], axis=0)

    def more(r, s_t, vrows):
        m_old = m_ref[r]
        m_new = jnp.maximum(m_old, jnp.max(s_t, axis=0, keepdims=True))
        m_ref[r] = m_new
        acc_ref[r] = (jnp.exp2(m_old - m_new) * acc_ref[r]
                      + _dot(vrows, jnp.exp2(s_t - m_new).astype(BF16)))

    def mix(r, br):
        o = acc_ref[r, :D, :] * ((1.0 / acc_ref[r, D:D + 1, :]) * gate(r, br))
        out_ref[r * D:(r + 1) * D, :] = out_ref[r * D:(r + 1) * D, :] + o

    ck = ck_ref[0, 0]
    ncp = ck.shape[0]
    cvt = jnp.concatenate([cvt_ref[0, 0], jnp.ones((BF16_ROWS, ncp), BF16)], axis=0)
    cmp_end = lax.broadcasted_iota(jnp.int32, (ncp, 1), 0) * CMP_STRIDE + (CMP_BLOCK - 1)
    cmask = cmp_end <= tcol
    sees_any = jnp.where(tcol >= CMP_BLOCK - 1, 1.0, 0.0)
    n_sel = ovt_ref.shape[0]
    imp_t = jnp.zeros((n_sel, tq), F32)
    scores = [_dot_nt(ck, q[r]) for r in range(R)]
    for r in range(R):
        s = jnp.where(cmask, scores[r], MASKED)
        e = jnp.exp2(s - jnp.max(s, axis=0, keepdims=True)).astype(BF16)
        acc = _dot(cvt, e)
        inv_l = sees_any * (1.0 / acc[D:D + 1, :])
        out_ref[r * D:(r + 1) * D, :] = acc[:D, :] * (inv_l * gate(r, 0))
        imp_t = imp_t + _dot(ovt_ref[...], e) * inv_l

    blk_i = lax.broadcasted_iota(jnp.int32, (n_sel, tq), 0)
    cur = tcol // SEL_BLOCK
    valid = blk_i * SEL_BLOCK <= tcol
    forced = (blk_i == 0) | (blk_i == cur) | (blk_i == cur - 1)
    score_ref[...] = jnp.where(forced, jnp.inf, jnp.where(valid, imp_t, -jnp.inf))
    n_back = WINDOW // tq
    win_starts = [pl.multiple_of(jnp.maximum(t0 - j * tq, 0), tq) for j in range(n_back + 1)]

    def issue_window():
        flag_lane = lax.broadcasted_iota(jnp.int32, (tq, LANES), 1) == D
        one = jnp.ones((tq, LANES), BF16)
        qw = [jnp.where(flag_lane, one, q_ref[:, r * LANES:(r + 1) * LANES]) for r in range(R)]
        for j in range(n_back + 1):
            off = jnp.full((tq, LANES), jnp.where(t0 >= j * tq, 0.0, MASKED), F32).astype(BF16)
            kw = jnp.where(flag_lane, off, kw_ref[0, pl.ds(win_starts[j], tq), :])
            for r in range(R):
                sw_ref[j, r] = _dot_nt(kw, qw[r])

    flag_ref[:D, :] = jnp.zeros((D, tq), F32)
    flag_ref[D:, :] = jnp.ones((LANES - D, tq), F32)
    live_blocks = (t0 + tq) // SEL_BLOCK

    def rank_blocks(n_live):
        for jg in range(n_live // 8):
            mine = score_ref[jg * 8:(jg + 1) * 8, :]
            blk_g = jg * 8 + lax.broadcasted_iota(jnp.int32, (8, tq), 0)
            ahead = jnp.zeros((8, tq), F32)
            for kb in range(n_live):
                other = score_ref[kb:kb + 1, :]
                if kb < jg * 8:
                    ahead = ahead + jnp.where(other >= mine, 1.0, 0.0)
                elif kb >= (jg + 1) * 8:
                    ahead = ahead + jnp.where(other > mine, 1.0, 0.0)
                else:
                    ahead = ahead + jnp.where(blk_g > kb, jnp.where(other >= mine, 1.0, 0.0),
                                              jnp.where(other > mine, 1.0, 0.0))
            flag_ref[D + jg * 8:D + (jg + 1) * 8, :] = jnp.where(ahead < top, 0.0, 1.0)

    step = RANK_STEP if n_sel % RANK_STEP == 0 else n_sel
    for v in range(n_sel // step):
        @pl.when((live_blocks + step - 1) // step == v + 1)
        def _():
            rank_blocks((v + 1) * step)

    not_sel_q = flag_ref[...].T.astype(BF16)
    qa = [jnp.where(low, q[r], not_sel_q) for r in range(R)]

    def keys_aug(start):
        return jnp.where(low, ks_ref[0, pl.ds(start, tq), :], eb_ref[pl.ds(start, tq), :])

    def issue(c, buf_ref):
        ka = keys_aug(pl.multiple_of(c * tq, tq))
        for r in range(R):
            buf_ref[r] = _dot_nt(ka, qa[r])

    def consume(c, buf_ref, own):
        vr = value_rows(vst_ref, pl.multiple_of(c * tq, tq))
        for r in range(R):
            more(r, jnp.where(causal, buf_ref[r], MASKED) if own else buf_ref[r], vr)

    causal = k_off <= t_off
    n_plain = pl.program_id(2)
    for r in range(R):
        m_ref[r] = jnp.full((1, tq), LOWEST, F32)
        acc_ref[r] = jnp.zeros(acc_ref.shape[1:], F32)
    issue(0, sa_ref)

    def sel_pair(k, carry):
        issue(2 * k + 1, sb_ref)
        consume(2 * k, sa_ref, False)
        issue(2 * k + 2, sa_ref)
        consume(2 * k + 1, sb_ref, False)
        return carry

    lax.fori_loop(0, n_plain // 2, sel_pair, 0)

    @pl.when(n_plain % 2 == 1)
    def _():
        issue(n_plain, sb_ref)
        consume(n_plain - 1, sa_ref, False)
        consume(n_plain, sb_ref, True)

    @pl.when(n_plain % 2 == 0)
    def _():
        consume(n_plain, sa_ref, True)

    for r in range(R):
        mix(r, 1)

    issue_window()
    for r in range(R):
        m_ref[r] = jnp.full((1, tq), LOWEST, F32)
        acc_ref[r] = jnp.zeros(acc_ref.shape[1:], F32)
    for j in range(n_back + 1):
        vr = value_rows(vwt_ref, win_starts[j])
        for r in range(R):
            s_t = sw_ref[j, r]
            if j == 0:
                s_t = jnp.where(causal, s_t, MASKED)
            elif j == n_back:
                s_t = jnp.where(k_off > t_off, s_t, MASKED)
            more(r, s_t, vr)
    for r in range(R):
        mix(r, 2)

    o_ref[...] = out_ref[...].T.astype(BF16)


def _nsa(nq, ck, cvt, kdup, vt, gt, ovt, eb, B, S, tq):
    T = B * S
    G = NSA_KV_HEADS
    assert S % tq == 0 and WINDOW % tq == 0 and S // SEL_BLOCK <= LANES - HEAD_DIM
    nqt = S // tq
    ncp = ck.shape[2]
    kdup3 = kdup.reshape(B, S, kdup.shape[1])
    acc_rows = HEAD_DIM + BF16_ROWS
    keys = lambda off: pl.BlockSpec((1, S, LANES), lambda b, g, i: (b, 0, off + g))
    vals = lambda off: pl.BlockSpec((HEAD_DIM, S), lambda b, g, i: (off + g, b))
    full = lambda a: pl.BlockSpec(a.shape, lambda b, g, i: (0,) * a.ndim)
    return pl.pallas_call(
        functools.partial(_nsa_kernel, tq=tq, top=min(SEL_TOPK, S // SEL_BLOCK)),
        grid=(B, G, nqt),
        in_specs=[pl.BlockSpec((tq, NSA_GROUP * LANES), lambda b, g, i: (b * nqt + i, g)),
                  pl.BlockSpec((1, 1, ncp, LANES), lambda b, g, i: (b, g, 0, 0)),
                  pl.BlockSpec((1, 1, HEAD_DIM, ncp), lambda b, g, i: (b, g, 0, 0)),
                  keys(0), keys(G), vals(0), vals(G),
                  pl.BlockSpec((GATE_ROWS, tq), lambda b, g, i: (g, b * nqt + i)),
                  full(ovt), full(eb)],
        out_specs=pl.BlockSpec((tq, NSA_GROUP * HEAD_DIM), lambda b, g, i: (b * nqt + i, g)),
        out_shape=jax.ShapeDtypeStruct((T, NSA_WIDTH), BF16),
        scratch_shapes=[pltpu.VMEM((NSA_GROUP, 1, tq), F32),
                        pltpu.VMEM((NSA_GROUP, acc_rows, tq), F32),
                        pltpu.VMEM((NSA_GROUP * HEAD_DIM, tq), F32),
                        pltpu.VMEM((S // SEL_BLOCK, tq), F32),
                        pltpu.VMEM((LANES, tq), F32),
                        pltpu.VMEM((NSA_GROUP, tq, tq), F32),
                        pltpu.VMEM((NSA_GROUP, tq, tq), F32),
                        pltpu.VMEM((WINDOW // tq + 1, NSA_GROUP, tq, tq), F32)],
        compiler_params=_params("parallel", "parallel", "parallel"),
    )(nq, ck, cvt, kdup3, kdup3, vt, vt, gt, ovt, eb)


def _nsa_constants(S):
    n_cmp = (S - CMP_BLOCK) // CMP_STRIDE + 1
    ncp = S // CMP_STRIDE
    n_sel = S // SEL_BLOCK
    ci = np.arange(ncp)[None, :]
    sj = np.arange(n_sel)[:, None]
    ovt = ((ci * CMP_STRIDE < (sj + 1) * SEL_BLOCK)
           & (ci * CMP_STRIDE + CMP_BLOCK > sj * SEL_BLOCK) & (ci < n_cmp))
    eb = np.zeros((S, LANES), np.float32)
    eb[np.arange(S), HEAD_DIM + np.arange(S) // SEL_BLOCK] = MASKED
    return jnp.asarray(ovt, BF16), jnp.asarray(eb, BF16)


def _mlp_kernel(ret_ref, nsa_ref, x_ref, p_ref, wo_ref, nw_ref, wg_ref, wv_ref, cw_ref, cb_ref,
                wd_ref, pg_ref, pw_ref, fw_ref, o_ref, h_ref, act_ref, tail_ref, *, fc, final):
    @pl.when(pl.program_id(1) == 0)
    def _():
        tail_ref[...] = jnp.zeros_like(tail_ref)

    mix = _dot(ret_ref[...], wo_ref[:RET_WIDTH, :]) + _dot(nsa_ref[...], wo_ref[RET_WIDTH:, :])
    h = x_ref[...] + mix
    h_ref[...] = h
    ms = jnp.mean(h * h, axis=-1, keepdims=True)
    hn = (h * lax.rsqrt(ms + EPS) * nw_ref[...]).astype(BF16)

    tm = hn.shape[0]
    row = lax.broadcasted_iota(jnp.int32, (tm, fc), 0)
    for c in range(wg_ref.shape[1] // fc):
        sl = slice(c * fc, (c + 1) * fc)
        g = _dot(hn, wg_ref[:, sl])
        val = _dot(hn, wv_ref[:, sl])
        tail = tail_ref[:, sl]
        g1 = jnp.where(row == 0, tail[7:8, :], pltpu.roll(g, 1, 0))
        g2 = jnp.where(row == 0, tail[6:7, :],
                       jnp.where(row == 1, tail[7:8, :], pltpu.roll(g, 2, 0)))
        tail_ref[:, sl] = g[tm - 8:, :]
        cw = cw_ref[:, sl]
        y = cb_ref[:, sl] + cw[0:1, :] * g2 + cw[1:2, :] * g1 + cw[2:3, :] * g
        act_ref[:, sl] = (y * jax.nn.sigmoid(y) * val).astype(BF16)

    h = h_ref[...] + _dot(act_ref[...], wd_ref[...])
    gate = jax.nn.sigmoid(_dot(h.astype(BF16), pg_ref[...]))
    h = h + gate * _dot(p_ref[...].astype(BF16), pw_ref[...])
    if final:
        ms = jnp.mean(h * h, axis=-1, keepdims=True)
        h = h * lax.rsqrt(ms + EPS) * fw_ref[...]
    o_ref[...] = h


def _mlp(ret_out, nsa_out, x2, p2, w_out, norm_w, wg, wv, conv_w, conv_b, wd, pg, pw, fw,
         B, S, tm, final):
    T, D = x2.shape
    F = wg.shape[1]
    nt = S // tm
    row = lambda w: pl.BlockSpec((tm, w), lambda b, j: (b * nt + j, 0))
    full = lambda a: pl.BlockSpec(a.shape, lambda b, j: (0,) * a.ndim, pipeline_mode=pl.Buffered(1))
    weights = (w_out, norm_w, wg, wv, conv_w, conv_b, wd, pg, pw, fw)
    return pl.pallas_call(
        functools.partial(_mlp_kernel, fc=256, final=final),
        grid=(B, nt),
        in_specs=[row(RET_WIDTH), row(NSA_WIDTH), row(D), row(p2.shape[1])]
                 + [full(w) for w in weights],
        out_specs=row(D),
        out_shape=jax.ShapeDtypeStruct((T, D), F32),
        scratch_shapes=[pltpu.VMEM((tm, D), F32), pltpu.VMEM((tm, F), BF16),
                        pltpu.VMEM((8, F), F32)],
        compiler_params=_params("arbitrary", "arbitrary"),
    )(ret_out, nsa_out, x2, p2, *weights)


def _rope_table():
    lane = np.arange(LANES) % HEAD_DIM
    half_r = HEAD_DIM // 2
    inv_r = jnp.power(jnp.float32(RET_THETA), -jnp.arange(half_r, dtype=F32) / half_r)
    half_n = ROPE_DIM // 2
    inv_n = jnp.power(jnp.float32(ROPE_THETA), -jnp.arange(half_n, dtype=F32) / half_n)
    freq = jnp.concatenate([inv_r, inv_n, jnp.zeros((LANES - half_r - half_n,), F32)])
    rows = [freq,
            jnp.asarray(np.where(lane < half_r, -1.0, 1.0), F32),
            jnp.asarray(np.where(lane < half_n, -1.0, 1.0), F32)]
    return jnp.concatenate([jnp.stack(rows), jnp.zeros((5, LANES), F32)], axis=0)


def _layer(h2, p2, pos2, B, S, final, norm_mix_w, w_in, ret_gn_w, cmp_pos, cmp_k_w1, cmp_k_w2,
           cmp_v_w1, cmp_v_w2, w_out, norm_ffn_w, ffn_w_up, ffn_conv_w, ffn_conv_b, ffn_w_down,
           ple_w, ple_gate_w, final_norm_w):
    T, D = h2.shape
    tm = _row_tile(S, 512)

    w_main, w_t = _in_proj_weights(w_in)
    ret_qkv, rg, nq, cmp_tok, kdup, vt, gt = _in_proj(
        h2, pos2, norm_mix_w.reshape(1, D), w_main, w_t, _rope_table(), tm)

    dec, qd, kd, cdm = _retention_constants()
    ret_out = _retention(ret_qkv, rg, ret_gn_w.reshape(1, RET_WIDTH), dec, qd, kd, cdm, B, S)

    w1 = jnp.stack([cmp_k_w1, cmp_v_w1]).astype(BF16)
    w2k = jnp.tile(cmp_k_w2, (1, HEADS_PER_VREG)).astype(BF16)
    ck, cvt = _compress(cmp_tok, cmp_pos.reshape(1, CMP_BLOCK * HEAD_DIM), w1, w2k,
                        cmp_v_w2.T.astype(BF16), B)

    ovt, eb = _nsa_constants(S)
    nsa_out = _nsa(nq, ck, cvt, kdup, vt, gt, ovt, eb, B, S, tq=256)

    d_ff = ffn_w_down.shape[0]
    conv_w = jnp.pad(ffn_conv_w, ((0, 8 - ffn_conv_w.shape[0]), (0, 0)))
    return _mlp(ret_out, nsa_out, h2, p2, w_out.astype(BF16), norm_ffn_w.reshape(1, D),
                ffn_w_up[:, :d_ff].astype(BF16), ffn_w_up[:, d_ff:].astype(BF16),
                conv_w, ffn_conv_b.reshape(1, d_ff), ffn_w_down.astype(BF16),
                ple_gate_w.astype(BF16), ple_w.astype(BF16), final_norm_w.reshape(1, D),
                B, S, tm, final)


def kernel(x, p, positions, norm_mix_w, w_in, ret_gn_w, cmp_pos, cmp_k_w1, cmp_k_w2, cmp_v_w1, cmp_v_w2, w_out, norm_ffn_w, ffn_w_up, ffn_conv_w, ffn_conv_b, ffn_w_down, ple_w, ple_gate_w, final_norm_w):
    B, S, D = x.shape
    T = B * S
    depth = w_in.shape[0]
    h = x.reshape(T, D)
    pos2 = positions.reshape(T, 1).astype(jnp.int32)
    for i in range(depth):
        h = _layer(h, p[i].reshape(T, -1), pos2, B, S, i == depth - 1, norm_mix_w[i], w_in[i],
                   ret_gn_w[i], cmp_pos[i], cmp_k_w1[i], cmp_k_w2[i], cmp_v_w1[i], cmp_v_w2[i],
                   w_out[i], norm_ffn_w[i], ffn_w_up[i], ffn_conv_w[i], ffn_conv_b[i],
                   ffn_w_down[i], ple_w[i], ple_gate_w[i], final_norm_w)
    return h.reshape(B, S, D)
```

```python
import functools

import numpy as np
import jax
import jax.numpy as jnp
from jax import lax
from jax.experimental import pallas as pl
from jax.experimental.pallas import tpu as pltpu

F32 = jnp.float32
BF16 = jnp.bfloat16

LANES = 128
BF16_ROWS = 16
HEAD_DIM = 64
HEADS_PER_VREG = LANES // HEAD_DIM
RET_HEADS = 8
NSA_Q_HEADS = 8
NSA_KV_HEADS = 2
NSA_GROUP = NSA_Q_HEADS // NSA_KV_HEADS
RET_WIDTH = RET_HEADS * HEAD_DIM
NSA_WIDTH = NSA_Q_HEADS * HEAD_DIM
KV_WIDTH = NSA_KV_HEADS * HEAD_DIM
RET_CHUNK = 128
RET_THETA = 10000.0
ROPE_THETA = 500000.0
ROPE_DIM = HEAD_DIM // 4
CMP_BLOCK = 32
CMP_STRIDE = 16
SEL_BLOCK = 64
SEL_TOPK = 16
WINDOW = 512
N_BRANCH = 3
GATE_ROWS = 16
RANK_STEP = 16
SEL_UNROLL = 4
EPS = 1e-6
QK_SCALE = HEAD_DIM ** -0.5
LOG2_E = 1.4426950408889634
MASKED = -1e30
LOWEST = -3e38

_OFF = np.cumsum([0, RET_WIDTH, RET_WIDTH, RET_WIDTH, RET_WIDTH, NSA_WIDTH,
                  KV_WIDTH, KV_WIDTH, KV_WIDTH, KV_WIDTH, KV_WIDTH, KV_WIDTH])
(_RQ, _RK, _RV, _RG, _NQ, _KC, _VC, _KSL, _VSL, _KWN, _VWN, _NG) = (int(v) for v in _OFF)

VMEM_LIMIT = 56 * 1024 * 1024


def _dot(a, b):
    return jnp.dot(a, b, preferred_element_type=F32)


def _dot_nt(a, b):
    return lax.dot_general(a, b, (((1,), (1,)), ((), ())), preferred_element_type=F32)


def _dot_tn(a, b):
    return lax.dot_general(a, b, (((0,), (0,)), ((), ())), preferred_element_type=F32)


def _params(*semantics):
    return pltpu.CompilerParams(dimension_semantics=semantics, vmem_limit_bytes=VMEM_LIMIT)


def _row_tile(n, pref):
    t = pref
    while n % t:
        t //= 2
    return t


def _rope(y, cos, sin_signed, first_half, half):
    nxt = pltpu.roll(y, LANES - half, 1)
    prv = pltpu.roll(y, half, 1)
    return y * cos + jnp.where(first_half, nxt, prv) * sin_signed


def _in_proj_kernel(x_ref, pos_ref, posrow_ref, nw_ref, w_ref, wt_ref, tab_ref, freq_ref,
                    ret_ref, rg_ref, qt_ref, cmp_ref, kdup_ref, vt_ref, gt_ref, stage_ref):
    x = x_ref[...]
    tm = x.shape[0]
    ms = jnp.mean(x * x, axis=-1, keepdims=True)
    xn = (x * lax.rsqrt(ms + EPS) * nw_ref[...]).astype(BF16)

    pos = pos_ref[...].astype(F32)
    tab = tab_ref[...]
    lane = lax.broadcasted_iota(jnp.int32, (tm, LANES), 1)
    in_head = lane % HEAD_DIM
    low = lane < HEAD_DIM
    half_r, half_n = HEAD_DIM // 2, ROPE_DIM // 2
    ang = pos * tab[0:1, :]
    first_r = in_head < half_r
    first_n = in_head < half_n
    rot_n = in_head < ROPE_DIM

    def spread_r(c):
        c = jnp.where(lane < half_r, c, pltpu.roll(c, half_r, 1))
        return jnp.where(low, c, pltpu.roll(c, HEAD_DIM, 1))

    def spread_n(c, rest):
        c = pltpu.roll(c, LANES - half_r, 1)
        c = jnp.where(lane < half_n, c, pltpu.roll(c, half_n, 1))
        c = jnp.where(low, c, pltpu.roll(c, HEAD_DIM, 1))
        return jnp.where(rot_n, c, rest)

    cos_a, sin_a = jnp.cos(ang), jnp.sin(ang)
    cos_r, sin_r = spread_r(cos_a), spread_r(sin_a) * tab[1:2, :]
    cos_n, sin_n = spread_n(cos_a, 1.0), spread_n(sin_a, 0.0) * tab[2:3, :]

    rope_r = lambda y: _rope(y, cos_r, sin_r, first_r, HEAD_DIM // 2)
    rope_n = lambda y: _rope(y, cos_n, sin_n, first_n, ROPE_DIM // 2)

    def halves(ci):
        y = _dot(xn, w_ref[:, ci * 256:(ci + 1) * 256])
        return y[:, :LANES], y[:, LANES:]

    for ci in range(2):
        for h, y in enumerate(halves(4 + ci)):
            c0 = 2 * RET_WIDTH + ci * 256 + h * LANES
            ret_ref[:, c0:c0 + LANES] = y.astype(BF16)
    t = _dot_nt(wt_ref[...], xn)
    ang_q = freq_ref[...] * posrow_ref[0].astype(F32)
    cos_q, sin_q = jnp.cos(ang_q), jnp.sin(ang_q)
    for hd in range(NSA_Q_HEADS):
        b0 = hd * HEAD_DIM
        x1, x2 = t[b0:b0 + half_n, :], t[b0 + half_n:b0 + ROPE_DIM, :]
        head = jnp.concatenate([x1 * cos_q - x2 * sin_q, x1 * sin_q + x2 * cos_q,
                                t[b0 + ROPE_DIM:b0 + HEAD_DIM, :]], axis=0)
        qt_ref[b0:b0 + HEAD_DIM, :] = (head * (QK_SCALE * LOG2_E)).astype(BF16)
    nq, nv = qt_ref.shape[0], vt_ref.shape[0]
    vt_ref[...] = t[nq:nq + nv, :].astype(BF16)
    gt_ref[...] = jax.nn.sigmoid(t[nq + nv:, :])
    for ci in range(2):
        for h, y in enumerate(halves(6 + ci)):
            c0 = ci * 256 + h * LANES
            rg_ref[:, c0:c0 + LANES] = y
    for ci in range(2):
        for h, y in enumerate(halves(ci)):
            c0 = ci * 256 + h * LANES
            ret_ref[:, c0:c0 + LANES] = rope_r(y).astype(BF16)
    for ci in range(2):
        for h, y in enumerate(halves(2 + ci)):
            c0 = RET_WIDTH + ci * 256 + h * LANES
            ret_ref[:, c0:c0 + LANES] = (rope_r(y) * QK_SCALE).astype(BF16)
    kc, vc = halves(8)
    for a, y in enumerate((rope_n(kc), vc)):
        stage_ref[2 * a] = y
        stage_ref[2 * a + 1] = pltpu.roll(y, HEAD_DIM, 1)
    n_grp = tm // CMP_STRIDE
    low_g = lax.broadcasted_iota(jnp.int32, (n_grp, LANES), 1) < HEAD_DIM
    for a in range(2):
        for j in range(CMP_STRIDE // 2):
            token = lambda l, s: stage_ref[2 * a + s, pl.ds(l, n_grp, stride=CMP_STRIDE), :]
            even, odd = 2 * j, 2 * j + 1
            cmp_ref[a, 0, :, j * LANES:(j + 1) * LANES] = jnp.where(
                low_g, token(even, 0), token(odd, 1)).astype(BF16)
            cmp_ref[a, 1, :, j * LANES:(j + 1) * LANES] = jnp.where(
                low_g, token(even, 1), token(odd, 0)).astype(BF16)
    for n, y in enumerate(halves(9)):
        y = rope_n(y)
        sw = pltpu.roll(y, HEAD_DIM, 1)
        kdup_ref[:, (2 * n) * LANES:(2 * n + 1) * LANES] = jnp.where(low, y, sw).astype(BF16)
        kdup_ref[:, (2 * n + 1) * LANES:(2 * n + 2) * LANES] = jnp.where(low, sw, y).astype(BF16)


def _in_proj(x2, pos2, norm_w, w_main, w_t, tab, freq, tm):
    T, D = x2.shape
    nv = 2 * KV_WIDTH
    ng = w_t.shape[0] - nv - NSA_WIDTH
    grp_w = CMP_STRIDE * HEAD_DIM
    pos_rows = pos2.reshape(T // tm, 1, tm)
    row = lambda w: pl.BlockSpec((tm, w), lambda i: (i, 0))
    col = lambda h: pl.BlockSpec((h, tm), lambda i: (0, i))
    full = lambda a: pl.BlockSpec(a.shape, lambda i: (0,) * a.ndim)
    return pl.pallas_call(
        _in_proj_kernel,
        grid=(T // tm,),
        in_specs=[row(D), row(1), pl.BlockSpec((1, 1, tm), lambda i: (i, 0, 0)),
                  full(norm_w), full(w_main), full(w_t), full(tab), full(freq)],
        out_specs=[row(3 * RET_WIDTH), row(RET_WIDTH), col(NSA_WIDTH),
                   pl.BlockSpec((2, NSA_KV_HEADS, tm // CMP_STRIDE, grp_w), lambda i: (0, 0, i, 0)),
                   row(4 * LANES), col(nv), col(ng)],
        out_shape=[jax.ShapeDtypeStruct((T, 3 * RET_WIDTH), BF16),
                   jax.ShapeDtypeStruct((T, RET_WIDTH), F32),
                   jax.ShapeDtypeStruct((NSA_WIDTH, T), BF16),
                   jax.ShapeDtypeStruct((2, NSA_KV_HEADS, T // CMP_STRIDE, grp_w), BF16),
                   jax.ShapeDtypeStruct((T, 4 * LANES), BF16),
                   jax.ShapeDtypeStruct((nv, T), BF16),
                   jax.ShapeDtypeStruct((ng, T), F32)],
        scratch_shapes=[pltpu.VMEM((4, tm, LANES), F32)],
        compiler_params=_params("parallel"),
    )(x2, pos2, pos_rows, norm_w, w_main, w_t, tab, freq)


def _in_proj_weights(w_in):
    w_main = jnp.concatenate([w_in[:, :_NQ], w_in[:, _KC:_KSL], w_in[:, _KSL:_VSL],
                              w_in[:, _KWN:_VWN]], axis=1)
    gate_cols = np.full((NSA_KV_HEADS * GATE_ROWS,), -1)
    for g in range(NSA_KV_HEADS):
        for r in range(NSA_GROUP):
            for br in range(N_BRANCH):
                gate_cols[g * GATE_ROWS + r * N_BRANCH + br] = _NG + (g * NSA_GROUP + r) * N_BRANCH + br
    w_gate = jnp.where(gate_cols[None, :] >= 0, w_in[:, np.maximum(gate_cols, 0)], 0.0)
    w_t = jnp.concatenate([w_in[:, _NQ:_KC], w_in[:, _VSL:_KWN], w_in[:, _VWN:_NG], w_gate],
                          axis=1).T
    return w_main.astype(BF16), w_t.astype(BF16)


def _retention_kernel(q_ref, k_ref, v_ref, rg_ref, gnw_ref, dec_ref, qd_ref, kd_ref, cdm_ref,
                      o_ref, state_ref, *, n_chunks, unroll):
    C = RET_CHUNK
    lane = lax.broadcasted_iota(jnp.int32, (C, LANES), 1)
    low = lane < HEAD_DIM
    r_i = lax.broadcasted_iota(jnp.int32, (LANES, LANES), 0)
    c_i = lax.broadcasted_iota(jnp.int32, (LANES, LANES), 1)
    same_head = (r_i < HEAD_DIM) == (c_i < HEAD_DIM)
    state_ref[...] = jnp.zeros_like(state_ref)

    def group(gi, carry):
        rows = [pl.multiple_of((gi * unroll + u) * C, C) for u in range(unroll)]
        q = [q_ref[pl.ds(r0, C), :] for r0 in rows]
        k = [k_ref[pl.ds(r0, C), :] for r0 in rows]
        v = [v_ref[pl.ds(r0, C), :] for r0 in rows]
        zero = jnp.zeros_like(q[0])
        s0 = [_dot_nt(jnp.where(low, q[u], zero), k[u]) for u in range(unroll)]
        s1 = [_dot_nt(jnp.where(low, zero, q[u]), k[u]) for u in range(unroll)]
        kv = [_dot_tn((k[u].astype(F32) * kd_ref[...]).astype(BF16), v[u]) for u in range(unroll)]

        p0 = [(s0[u] * dec_ref[0]).astype(BF16) for u in range(unroll)]
        p1 = [(s1[u] * dec_ref[1]).astype(BF16) for u in range(unroll)]
        states = [state_ref[...]]
        for u in range(unroll):
            states.append(states[u] * cdm_ref[0] + jnp.where(same_head, kv[u], 0.0))
        state_ref[...] = states[unroll]

        o0 = [_dot(p0[u], v[u]) for u in range(unroll)]
        o1 = [_dot(p1[u], v[u]) for u in range(unroll)]
        oc = [_dot((q[u].astype(F32) * qd_ref[...]).astype(BF16), states[u].astype(BF16))
              for u in range(unroll)]

        inv_d = 1.0 / HEAD_DIM
        for u in range(unroll):
            ro = jnp.where(low, o0[u], o1[u]) + oc[u]
            sum0 = jnp.sum(jnp.where(low, ro, 0.0), axis=-1, keepdims=True)
            sum1 = jnp.sum(jnp.where(low, 0.0, ro), axis=-1, keepdims=True)
            d = ro - jnp.where(low, sum0, sum1) * inv_d
            dd = d * d
            var0 = jnp.sum(jnp.where(low, dd, 0.0), axis=-1, keepdims=True)
            var1 = jnp.sum(jnp.where(low, 0.0, dd), axis=-1, keepdims=True)
            var = jnp.where(low, var0, var1) * inv_d
            y = d * lax.rsqrt(var + EPS) * gnw_ref[...]
            g = rg_ref[pl.ds(rows[u], C), :]
            o_ref[pl.ds(rows[u], C), :] = (y * (g * jax.nn.sigmoid(g))).astype(BF16)
        return carry

    lax.fori_loop(0, n_chunks // unroll, group, 0)


def _retention(ret_qkv, rg, gn_w, dec, qd, kd, cdm, B, S):
    T = B * S
    n_pairs = RET_HEADS // HEADS_PER_VREG
    seq = lambda off: pl.BlockSpec((S, LANES), lambda b, hp: (b, off + hp))
    return pl.pallas_call(
        functools.partial(_retention_kernel, n_chunks=S // RET_CHUNK,
                          unroll=_row_tile(S // RET_CHUNK, 8)),
        grid=(B, n_pairs),
        in_specs=[seq(0), seq(n_pairs), seq(2 * n_pairs), seq(0),
                  pl.BlockSpec((1, LANES), lambda b, hp: (0, hp)),
                  pl.BlockSpec((HEADS_PER_VREG, RET_CHUNK, RET_CHUNK), lambda b, hp: (hp, 0, 0)),
                  pl.BlockSpec((RET_CHUNK, LANES), lambda b, hp: (0, hp)),
                  pl.BlockSpec((RET_CHUNK, LANES), lambda b, hp: (0, hp)),
                  pl.BlockSpec((1, LANES, LANES), lambda b, hp: (hp, 0, 0))],
        out_specs=seq(0),
        out_shape=jax.ShapeDtypeStruct((T, RET_WIDTH), BF16),
        scratch_shapes=[pltpu.VMEM((LANES, LANES), F32)],
        compiler_params=_params("parallel", "parallel"),
    )(ret_qkv, ret_qkv, ret_qkv, rg, gn_w, dec, qd, kd, cdm)


def _retention_constants():
    H, C = RET_HEADS, RET_CHUNK
    log_g = jnp.log1p(-jnp.exp2(-5.0 - jnp.arange(H, dtype=F32)))
    pos = jnp.arange(C, dtype=F32)
    diff = pos[:, None] - pos[None, :]
    dec = jnp.where(diff >= 0, jnp.exp(jnp.maximum(diff, 0.0) * log_g[:, None, None]), 0.0)
    q_decay = jnp.exp((pos + 1.0) * log_g[:, None]).T
    k_decay = jnp.exp((C - 1.0 - pos) * log_g[:, None]).T
    chunk_decay = jnp.exp(C * log_g)
    qd = jnp.repeat(q_decay, HEAD_DIM, axis=1)
    kd = jnp.repeat(k_decay, HEAD_DIM, axis=1)
    cd_rows = jnp.repeat(chunk_decay, HEAD_DIM).reshape(H // HEADS_PER_VREG, LANES, 1)
    head_of = jnp.arange(LANES) // HEAD_DIM
    same = (head_of[:, None] == head_of[None, :]).astype(F32)
    return dec, qd, kd, cd_rows * same[None]


def _compress_kernel(tk_ref, tv_ref, pos_ref, w1_ref, w2k_ref, w2vt_ref, ck_ref, cvt_ref):
    pos = jnp.broadcast_to(pos_ref[...], (8, pos_ref.shape[1])).astype(BF16)

    def hidden(t, w1_ref_a):
        half = t.shape[1]
        bias = _dot(pos, w1_ref_a[...])[0:1, :]
        first = _dot(t, w1_ref_a[:half, :])
        second = _dot(t, w1_ref_a[half:, :])
        n = t.shape[0]
        h = first + pltpu.roll(second, n - 1, 0) + bias
        return jax.nn.gelu(h).astype(BF16)

    ck_ref[0, 0] = _dot(hidden(tk_ref[0, 0], w1_ref.at[0]), w2k_ref[...]).astype(BF16)
    cvt_ref[0, 0] = _dot_nt(w2vt_ref[...], hidden(tv_ref[0, 0], w1_ref.at[1])).astype(BF16)


def _compress(cmp_tok, pos_flat, w1, w2k, w2vt, B):
    _, G, rows, width = cmp_tok.shape
    n = rows // B
    tok = lambda a: pl.BlockSpec((1, 1, n, width), lambda b, g: (a, g, b, 0))
    full = lambda a: pl.BlockSpec(a.shape, lambda b, g: (0,) * a.ndim)
    return pl.pallas_call(
        _compress_kernel,
        grid=(B, G),
        in_specs=[tok(0), tok(1), full(pos_flat), full(w1), full(w2k), full(w2vt)],
        out_specs=[pl.BlockSpec((1, 1, n, LANES), lambda b, g: (b, g, 0, 0)),
                   pl.BlockSpec((1, 1, HEAD_DIM, n), lambda b, g: (b, g, 0, 0))],
        out_shape=[jax.ShapeDtypeStruct((B, G, n, LANES), BF16),
                   jax.ShapeDtypeStruct((B, G, HEAD_DIM, n), BF16)],
        compiler_params=_params("parallel", "parallel"),
    )(cmp_tok, cmp_tok, pos_flat, w1, w2k, w2vt)


def _nsa_kernel(q_ref, ck_ref, cvt_ref, ks_ref, kw_ref, vst_ref, vwt_ref, gt_ref, ovt_ref, eb_ref,
                o_ref, m_ref, acc_ref, out_ref, score_ref, flag_ref, sa_ref, sb_ref, sw_ref,
                *, tq, top):
    R = NSA_GROUP
    D = HEAD_DIM
    t0 = pl.multiple_of(pl.program_id(2) * tq, tq)
    t_off = lax.broadcasted_iota(jnp.int32, (1, tq), 1)
    tcol = t0 + t_off
    k_off = lax.broadcasted_iota(jnp.int32, (tq, 1), 0)
    low = lax.broadcasted_iota(jnp.int32, (tq, LANES), 1) < D
    ones_rows = jnp.ones((BF16_ROWS, tq), BF16)
    gate = lambda r, br: gt_ref[r * N_BRANCH + br:r * N_BRANCH + br + 1, :]

    def q_aug(r, extra):
        return jnp.concatenate([q_ref[r * D:(r + 1) * D, :], extra], axis=0)

    zero_rows = jnp.zeros((LANES - D, tq), BF16)

    def value_rows(vt_ref_, start):
        return jnp.concatenate([vt_ref_[:, pl.ds(start, tq)], ones_rows], axis=0)

    def more(r, s_t, vrows):
        m_old = m_ref[r]
        m_new = jnp.maximum(m_old, jnp.max(s_t, axis=0, keepdims=True))
        m_ref[r] = m_new
        acc_ref[r] = (jnp.exp2(m_old - m_new) * acc_ref[r]
                      + _dot(vrows, jnp.exp2(s_t - m_new).astype(BF16)))

    def mix(r, br):
        o = acc_ref[r, :D, :] * ((1.0 / acc_ref[r, D:D + 1, :]) * gate(r, br))
        out_ref[r * D:(r + 1) * D, :] = out_ref[r * D:(r + 1) * D, :] + o

    ck = ck_ref[0, 0]
    ncp = ck.shape[0]
    cvt = jnp.concatenate([cvt_ref[0, 0], jnp.ones((BF16_ROWS, ncp), BF16)], axis=0)
    cmp_end = lax.broadcasted_iota(jnp.int32, (ncp, 1), 0) * CMP_STRIDE + (CMP_BLOCK - 1)
    cmask = cmp_end <= tcol
    sees_any = jnp.where(tcol >= CMP_BLOCK - 1, 1.0, 0.0)
    n_sel = ovt_ref.shape[0]
    imp_t = jnp.zeros((n_sel, tq), F32)
    scores = [_dot(ck, q_aug(r, zero_rows)) for r in range(R)]
    for r in range(R):
        s = jnp.where(cmask, scores[r], MASKED)
        e = jnp.exp2(s - jnp.max(s, axis=0, keepdims=True)).astype(BF16)
        acc = _dot(cvt, e)
        inv_l = sees_any * (1.0 / acc[D:D + 1, :])
        out_ref[r * D:(r + 1) * D, :] = acc[:D, :] * (inv_l * gate(r, 0))
        imp_t = imp_t + _dot(ovt_ref[...], e) * inv_l

    blk_i = lax.broadcasted_iota(jnp.int32, (n_sel, tq), 0)
    cur = tcol // SEL_BLOCK
    valid = blk_i * SEL_BLOCK <= tcol
    forced = (blk_i == 0) | (blk_i == cur) | (blk_i == cur - 1)
    score_ref[...] = jnp.where(forced, jnp.inf, jnp.where(valid, imp_t, -jnp.inf))
    n_back = WINDOW // tq
    win_starts = [pl.multiple_of(jnp.maximum(t0 - j * tq, 0), tq) for j in range(n_back + 1)]

    def issue_window():
        flag_lane = lax.broadcasted_iota(jnp.int32, (tq, LANES), 1) == D
        flag_row = lax.broadcasted_iota(jnp.int32, (LANES - D, tq), 0) == 0
        qw = [q_aug(r, jnp.where(flag_row, 1.0, 0.0).astype(BF16)) for r in range(R)]
        for j in range(n_back + 1):
            off = jnp.full((tq, LANES), jnp.where(t0 >= j * tq, 0.0, MASKED), F32).astype(BF16)
            kw = jnp.where(flag_lane, off, kw_ref[0, pl.ds(win_starts[j], tq), :])
            for r in range(R):
                sw_ref[j, r] = _dot(kw, qw[r])

    flag_ref[...] = jnp.ones(flag_ref.shape, F32)
    live_blocks = (t0 + tq) // SEL_BLOCK

    def rank_blocks(n_live):
        for jg in range(n_live // 8):
            mine = score_ref[jg * 8:(jg + 1) * 8, :]
            blk_g = jg * 8 + lax.broadcasted_iota(jnp.int32, (8, tq), 0)
            ahead = jnp.zeros((8, tq), F32)
            for kb in range(n_live):
                other = score_ref[kb:kb + 1, :]
                if kb < jg * 8:
                    ahead = ahead + jnp.where(other >= mine, 1.0, 0.0)
                elif kb >= (jg + 1) * 8:
                    ahead = ahead + jnp.where(other > mine, 1.0, 0.0)
                else:
                    ahead = ahead + jnp.where(blk_g > kb, jnp.where(other >= mine, 1.0, 0.0),
                                              jnp.where(other > mine, 1.0, 0.0))
            flag_ref[jg * 8:(jg + 1) * 8, :] = jnp.where(ahead < top, 0.0, 1.0)

    step = RANK_STEP if n_sel % RANK_STEP == 0 else n_sel
    for v in range(n_sel // step):
        @pl.when((live_blocks + step - 1) // step == v + 1)
        def _():
            rank_blocks((v + 1) * step)

    not_sel = flag_ref[...].astype(BF16)
    qa = [q_aug(r, not_sel) for r in range(R)]

    def keys_aug(start):
        return jnp.where(low, ks_ref[0, pl.ds(start, tq), :], eb_ref[pl.ds(start, tq), :])

    def issue(c, buf_ref):
        ka = keys_aug(pl.multiple_of(c * tq, tq))
        for r in range(R):
            buf_ref[r] = _dot(ka, qa[r])

    def consume(c, buf_ref, own):
        vr = value_rows(vst_ref, pl.multiple_of(c * tq, tq))
        for r in range(R):
            more(r, jnp.where(causal, buf_ref[r], MASKED) if own else buf_ref[r], vr)

    causal = k_off <= t_off
    n_plain = pl.program_id(2)
    for r in range(R):
        m_ref[r] = jnp.full((1, tq), LOWEST, F32)
        acc_ref[r] = jnp.zeros(acc_ref.shape[1:], F32)
    issue(0, sa_ref)
    bufs = (sa_ref, sb_ref)

    def sel_trip(k, carry):
        for u in range(SEL_UNROLL):
            c = SEL_UNROLL * k + u
            issue(c + 1, bufs[(u + 1) % 2])
            consume(c, bufs[u % 2], False)
        return carry

    lax.fori_loop(0, n_plain // SEL_UNROLL, sel_trip, 0)

    for rem in range(SEL_UNROLL):
        @pl.when(n_plain % SEL_UNROLL == rem)
        def _():
            for u in range(rem):
                c = n_plain - rem + u
                issue(c + 1, bufs[(u + 1) % 2])
                consume(c, bufs[u % 2], False)
            consume(n_plain, bufs[rem % 2], True)

    for r in range(R):
        mix(r, 1)

    issue_window()
    for r in range(R):
        m_ref[r] = jnp.full((1, tq), LOWEST, F32)
        acc_ref[r] = jnp.zeros(acc_ref.shape[1:], F32)
    for j in range(n_back + 1):
        vr = value_rows(vwt_ref, win_starts[j])
        for r in range(R):
            s_t = sw_ref[j, r]
            if j == 0:
                s_t = jnp.where(causal, s_t, MASKED)
            elif j == n_back:
                s_t = jnp.where(k_off > t_off, s_t, MASKED)
            more(r, s_t, vr)
    for r in range(R):
        mix(r, 2)

    o_ref[...] = out_ref[...].T.astype(BF16)


def _nsa(nq, ck, cvt, kdup, vt, gt, ovt, eb, B, S, tq):
    T = B * S
    G = NSA_KV_HEADS
    assert S % tq == 0 and WINDOW % tq == 0 and S // SEL_BLOCK <= LANES - HEAD_DIM
    nqt = S // tq
    ncp = ck.shape[2]
    kdup3 = kdup.reshape(B, S, kdup.shape[1])
    acc_rows = HEAD_DIM + BF16_ROWS
    keys = lambda off: pl.BlockSpec((1, S, LANES), lambda b, g, i: (b, 0, off + g))
    vals = lambda off: pl.BlockSpec((HEAD_DIM, S), lambda b, g, i: (off + g, b))
    full = lambda a: pl.BlockSpec(a.shape, lambda b, g, i: (0,) * a.ndim)
    return pl.pallas_call(
        functools.partial(_nsa_kernel, tq=tq, top=min(SEL_TOPK, S // SEL_BLOCK)),
        grid=(B, G, nqt),
        in_specs=[pl.BlockSpec((NSA_GROUP * HEAD_DIM, tq), lambda b, g, i: (g, b * nqt + i)),
                  pl.BlockSpec((1, 1, ncp, LANES), lambda b, g, i: (b, g, 0, 0)),
                  pl.BlockSpec((1, 1, HEAD_DIM, ncp), lambda b, g, i: (b, g, 0, 0)),
                  keys(0), keys(G), vals(0), vals(G),
                  pl.BlockSpec((GATE_ROWS, tq), lambda b, g, i: (g, b * nqt + i)),
                  full(ovt), full(eb)],
        out_specs=pl.BlockSpec((tq, NSA_GROUP * HEAD_DIM), lambda b, g, i: (b * nqt + i, g)),
        out_shape=jax.ShapeDtypeStruct((T, NSA_WIDTH), BF16),
        scratch_shapes=[pltpu.VMEM((NSA_GROUP, 1, tq), F32),
                        pltpu.VMEM((NSA_GROUP, acc_rows, tq), F32),
                        pltpu.VMEM((NSA_GROUP * HEAD_DIM, tq), F32),
                        pltpu.VMEM((S // SEL_BLOCK, tq), F32),
                        pltpu.VMEM((LANES - HEAD_DIM, tq), F32),
                        pltpu.VMEM((NSA_GROUP, tq, tq), F32),
                        pltpu.VMEM((NSA_GROUP, tq, tq), F32),
                        pltpu.VMEM((WINDOW // tq + 1, NSA_GROUP, tq, tq), F32)],
        compiler_params=_params("parallel", "parallel", "parallel"),
    )(nq, ck, cvt, kdup3, kdup3, vt, vt, gt, ovt, eb)


def _nsa_constants(S):
    n_cmp = (S - CMP_BLOCK) // CMP_STRIDE + 1
    ncp = S // CMP_STRIDE
    n_sel = S // SEL_BLOCK
    ci = np.arange(ncp)[None, :]
    sj = np.arange(n_sel)[:, None]
    ovt = ((ci * CMP_STRIDE < (sj + 1) * SEL_BLOCK)
           & (ci * CMP_STRIDE + CMP_BLOCK > sj * SEL_BLOCK) & (ci < n_cmp))
    eb = np.zeros((S, LANES), np.float32)
    eb[np.arange(S), HEAD_DIM + np.arange(S) // SEL_BLOCK] = MASKED
    return jnp.asarray(ovt, BF16), jnp.asarray(eb, BF16)


def _mlp_kernel(ret_ref, nsa_ref, x_ref, p_ref, wo_ref, nw_ref, wg_ref, wv_ref, cw_ref, cb_ref,
                wd_ref, pg_ref, pw_ref, fw_ref, o_ref, h_ref, act_ref, tail_ref, *, fc, final):
    @pl.when(pl.program_id(1) == 0)
    def _():
        tail_ref[...] = jnp.zeros_like(tail_ref)

    mix = _dot(ret_ref[...], wo_ref[:RET_WIDTH, :]) + _dot(nsa_ref[...], wo_ref[RET_WIDTH:, :])
    h = x_ref[...] + mix
    h_ref[...] = h
    ms = jnp.mean(h * h, axis=-1, keepdims=True)
    hn = (h * lax.rsqrt(ms + EPS) * nw_ref[...]).astype(BF16)

    tm = hn.shape[0]
    row = lax.broadcasted_iota(jnp.int32, (tm, fc), 0)
    for c in range(wg_ref.shape[1] // fc):
        sl = slice(c * fc, (c + 1) * fc)
        g = _dot(hn, wg_ref[:, sl])
        val = _dot(hn, wv_ref[:, sl])
        tail = tail_ref[:, sl]
        g1 = jnp.where(row == 0, tail[7:8, :], pltpu.roll(g, 1, 0))
        g2 = jnp.where(row == 0, tail[6:7, :],
                       jnp.where(row == 1, tail[7:8, :], pltpu.roll(g, 2, 0)))
        tail_ref[:, sl] = g[tm - 8:, :]
        cw = cw_ref[:, sl]
        y = cb_ref[:, sl] + cw[0:1, :] * g2 + cw[1:2, :] * g1 + cw[2:3, :] * g
        act_ref[:, sl] = (y * jax.nn.sigmoid(y) * val).astype(BF16)

    h = h_ref[...] + _dot(act_ref[...], wd_ref[...])
    gate = jax.nn.sigmoid(_dot(h.astype(BF16), pg_ref[...]))
    h = h + gate * _dot(p_ref[...].astype(BF16), pw_ref[...])
    if final:
        ms = jnp.mean(h * h, axis=-1, keepdims=True)
        h = h * lax.rsqrt(ms + EPS) * fw_ref[...]
    o_ref[...] = h


def _mlp(ret_out, nsa_out, x2, p2, w_out, norm_w, wg, wv, conv_w, conv_b, wd, pg, pw, fw,
         B, S, tm, final):
    T, D = x2.shape
    F = wg.shape[1]
    nt = S // tm
    row = lambda w: pl.BlockSpec((tm, w), lambda b, j: (b * nt + j, 0))
    full = lambda a: pl.BlockSpec(a.shape, lambda b, j: (0,) * a.ndim, pipeline_mode=pl.Buffered(1))
    weights = (w_out, norm_w, wg, wv, conv_w, conv_b, wd, pg, pw, fw)
    return pl.pallas_call(
        functools.partial(_mlp_kernel, fc=256, final=final),
        grid=(B, nt),
        in_specs=[row(RET_WIDTH), row(NSA_WIDTH), row(D), row(p2.shape[1])]
                 + [full(w) for w in weights],
        out_specs=row(D),
        out_shape=jax.ShapeDtypeStruct((T, D), F32),
        scratch_shapes=[pltpu.VMEM((tm, D), F32), pltpu.VMEM((tm, F), BF16),
                        pltpu.VMEM((8, F), F32)],
        compiler_params=_params("arbitrary", "arbitrary"),
    )(ret_out, nsa_out, x2, p2, *weights)


def _rope_tables(tm):
    lane = np.arange(LANES) % HEAD_DIM
    half_r = HEAD_DIM // 2
    inv_r = jnp.power(jnp.float32(RET_THETA), -jnp.arange(half_r, dtype=F32) / half_r)
    half_n = ROPE_DIM // 2
    inv_n = jnp.power(jnp.float32(ROPE_THETA), -jnp.arange(half_n, dtype=F32) / half_n)
    freq = jnp.concatenate([inv_r, inv_n, jnp.zeros((LANES - half_r - half_n,), F32)])
    rows = [freq,
            jnp.asarray(np.where(lane < half_r, -1.0, 1.0), F32),
            jnp.asarray(np.where(lane < half_n, -1.0, 1.0), F32)]
    table = jnp.concatenate([jnp.stack(rows), jnp.zeros((5, LANES), F32)], axis=0)
    return table, jnp.broadcast_to(inv_n[:, None], (half_n, tm))


def _layer(h2, p2, pos2, B, S, final, norm_mix_w, w_in, ret_gn_w, cmp_pos, cmp_k_w1, cmp_k_w2,
           cmp_v_w1, cmp_v_w2, w_out, norm_ffn_w, ffn_w_up, ffn_conv_w, ffn_conv_b, ffn_w_down,
           ple_w, ple_gate_w, final_norm_w):
    T, D = h2.shape
    tm = _row_tile(S, 512)

    w_main, w_t = _in_proj_weights(w_in)
    ret_qkv, rg, qt, cmp_tok, kdup, vt, gt = _in_proj(
        h2, pos2, norm_mix_w.reshape(1, D), w_main, w_t, *_rope_tables(tm), tm)

    dec, qd, kd, cdm = _retention_constants()
    ret_out = _retention(ret_qkv, rg, ret_gn_w.reshape(1, RET_WIDTH), dec, qd, kd, cdm, B, S)

    w1 = jnp.stack([cmp_k_w1, cmp_v_w1]).astype(BF16)
    w2k = jnp.tile(cmp_k_w2, (1, HEADS_PER_VREG)).astype(BF16)
    ck, cvt = _compress(cmp_tok, cmp_pos.reshape(1, CMP_BLOCK * HEAD_DIM), w1, w2k,
                        cmp_v_w2.T.astype(BF16), B)

    ovt, eb = _nsa_constants(S)
    nsa_out = _nsa(qt, ck, cvt, kdup, vt, gt, ovt, eb, B, S, tq=256)

    d_ff = ffn_w_down.shape[0]
    conv_w = jnp.pad(ffn_conv_w, ((0, 8 - ffn_conv_w.shape[0]), (0, 0)))
    return _mlp(ret_out, nsa_out, h2, p2, w_out.astype(BF16), norm_ffn_w.reshape(1, D),
                ffn_w_up[:, :d_ff].astype(BF16), ffn_w_up[:, d_ff:].astype(BF16),
                conv_w, ffn_conv_b.reshape(1, d_ff), ffn_w_down.astype(BF16),
                ple_gate_w.astype(BF16), ple_w.astype(BF16), final_norm_w.reshape(1, D),
                B, S, tm, final)


def kernel(x, p, positions, norm_mix_w, w_in, ret_gn_w, cmp_pos, cmp_k_w1, cmp_k_w2, cmp_v_w1, cmp_v_w2, w_out, norm_ffn_w, ffn_w_up, ffn_conv_w, ffn_conv_b, ffn_w_down, ple_w, ple_gate_w, final_norm_w):
    B, S, D = x.shape
    T = B * S
    depth = w_in.shape[0]
    h = x.reshape(T, D)
    pos2 = positions.reshape(T, 1).astype(jnp.int32)
    for i in range(depth):
        h = _layer(h, p[i].reshape(T, -1), pos2, B, S, i == depth - 1, norm_mix_w[i], w_in[i],
                   ret_gn_w[i], cmp_pos[i], cmp_k_w1[i], cmp_k_w2[i], cmp_v_w1[i], cmp_v_w2[i],
                   w_out[i], norm_ffn_w[i], ffn_w_up[i], ffn_conv_w[i], ffn_conv_b[i],
                   ffn_w_down[i], ple_w[i], ple_gate_w[i], final_norm_w)
    return h.reshape(B, S, D)
```

```python
import functools

import numpy as np
import jax
import jax.numpy as jnp
from jax import lax
from jax.experimental import pallas as pl
from jax.experimental.pallas import tpu as pltpu

F32 = jnp.float32
BF16 = jnp.bfloat16

LANES = 128
BF16_ROWS = 16
HEAD_DIM = 64
HEADS_PER_VREG = LANES // HEAD_DIM
RET_HEADS = 8
NSA_Q_HEADS = 8
NSA_KV_HEADS = 2
NSA_GROUP = NSA_Q_HEADS // NSA_KV_HEADS
RET_WIDTH = RET_HEADS * HEAD_DIM
NSA_WIDTH = NSA_Q_HEADS * HEAD_DIM
KV_WIDTH = NSA_KV_HEADS * HEAD_DIM
RET_CHUNK = 128
RET_THETA = 10000.0
ROPE_THETA = 500000.0
ROPE_DIM = HEAD_DIM // 4
CMP_BLOCK = 32
CMP_STRIDE = 16
SEL_BLOCK = 64
SEL_TOPK = 16
WINDOW = 512
N_BRANCH = 3
GATE_ROWS = 16
RANK_STEP = 16
SEL_UNROLL = 4
EPS = 1e-6
QK_SCALE = HEAD_DIM ** -0.5
LOG2_E = 1.4426950408889634
MASKED = -1e30
LOWEST = -3e38

_OFF = np.cumsum([0, RET_WIDTH, RET_WIDTH, RET_WIDTH, RET_WIDTH, NSA_WIDTH,
                  KV_WIDTH, KV_WIDTH, KV_WIDTH, KV_WIDTH, KV_WIDTH, KV_WIDTH])
(_RQ, _RK, _RV, _RG, _NQ, _KC, _VC, _KSL, _VSL, _KWN, _VWN, _NG) = (int(v) for v in _OFF)

VMEM_LIMIT = 56 * 1024 * 1024


def _dot(a, b):
    return jnp.dot(a, b, preferred_element_type=F32)


def _dot_nt(a, b):
    return lax.dot_general(a, b, (((1,), (1,)), ((), ())), preferred_element_type=F32)


def _dot_tn(a, b):
    return lax.dot_general(a, b, (((0,), (0,)), ((), ())), preferred_element_type=F32)


def _params(*semantics):
    return pltpu.CompilerParams(dimension_semantics=semantics, vmem_limit_bytes=VMEM_LIMIT)


def _row_tile(n, pref):
    t = pref
    while n % t:
        t //= 2
    return t


def _rope(y, cos, sin_signed, first_half, half):
    nxt = pltpu.roll(y, LANES - half, 1)
    prv = pltpu.roll(y, half, 1)
    return y * cos + jnp.where(first_half, nxt, prv) * sin_signed


def _in_proj_kernel(x_ref, posrow_ref, nw_ref, w_ref, wt_ref, tab_ref, freq_ref,
                    ret_ref, rg_ref, qt_ref, cmp_ref, kdup_ref, vt_ref, gt_ref, stage_ref):
    x = x_ref[...]
    tm = x.shape[0]
    ms = jnp.mean(x * x, axis=-1, keepdims=True)
    xn = (x * lax.rsqrt(ms + EPS) * nw_ref[...]).astype(BF16)

    tab = tab_ref[...]
    lane = lax.broadcasted_iota(jnp.int32, (tm, LANES), 1)
    in_head = lane % HEAD_DIM
    low = lane < HEAD_DIM
    half_r, half_n = HEAD_DIM // 2, ROPE_DIM // 2
    ang_t = freq_ref[...] * posrow_ref[0].astype(F32)
    cos_t, sin_t = jnp.cos(ang_t), jnp.sin(ang_t)
    pad = jnp.zeros((LANES - half_r - half_n, tm), F32)
    cos_a = jnp.concatenate([cos_t, pad], axis=0).T
    sin_a = jnp.concatenate([sin_t, pad], axis=0).T
    cos_q, sin_q = cos_t[half_r:, :], sin_t[half_r:, :]
    first_r = in_head < half_r
    first_n = in_head < half_n
    rot_n = in_head < ROPE_DIM

    def spread_r(c):
        c = jnp.where(lane < half_r, c, pltpu.roll(c, half_r, 1))
        return jnp.where(low, c, pltpu.roll(c, HEAD_DIM, 1))

    def spread_n(c, rest):
        c = pltpu.roll(c, LANES - half_r, 1)
        c = jnp.where(lane < half_n, c, pltpu.roll(c, half_n, 1))
        c = jnp.where(low, c, pltpu.roll(c, HEAD_DIM, 1))
        return jnp.where(rot_n, c, rest)

    cos_r, sin_r = spread_r(cos_a), spread_r(sin_a) * tab[0:1, :]
    cos_n, sin_n = spread_n(cos_a, 1.0), spread_n(sin_a, 0.0) * tab[1:2, :]

    rope_r = lambda y: _rope(y, cos_r, sin_r, first_r, HEAD_DIM // 2)
    rope_n = lambda y: _rope(y, cos_n, sin_n, first_n, ROPE_DIM // 2)

    def halves(ci):
        y = _dot(xn, w_ref[:, ci * 256:(ci + 1) * 256])
        return y[:, :LANES], y[:, LANES:]

    for ci in range(2):
        for h, y in enumerate(halves(4 + ci)):
            c0 = 2 * RET_WIDTH + ci * 256 + h * LANES
            ret_ref[:, c0:c0 + LANES] = y.astype(BF16)
    t = _dot_nt(wt_ref[...], xn)
    for hd in range(NSA_Q_HEADS):
        b0 = hd * HEAD_DIM
        x1, x2 = t[b0:b0 + half_n, :], t[b0 + half_n:b0 + ROPE_DIM, :]
        head = jnp.concatenate([x1 * cos_q - x2 * sin_q, x1 * sin_q + x2 * cos_q,
                                t[b0 + ROPE_DIM:b0 + HEAD_DIM, :]], axis=0)
        qt_ref[b0:b0 + HEAD_DIM, :] = (head * (QK_SCALE * LOG2_E)).astype(BF16)
    nq, nv = qt_ref.shape[0], vt_ref.shape[0]
    vt_ref[...] = t[nq:nq + nv, :].astype(BF16)
    gt_ref[...] = jax.nn.sigmoid(t[nq + nv:, :])
    for ci in range(2):
        for h, y in enumerate(halves(ci)):
            c0 = ci * 256 + h * LANES
            ret_ref[:, c0:c0 + LANES] = rope_r(y).astype(BF16)
    for ci in range(2):
        for h, y in enumerate(halves(2 + ci)):
            c0 = RET_WIDTH + ci * 256 + h * LANES
            ret_ref[:, c0:c0 + LANES] = (rope_r(y) * QK_SCALE).astype(BF16)
    kc, vc = halves(8)
    for a, y in enumerate((rope_n(kc), vc)):
        stage_ref[2 * a] = y
        stage_ref[2 * a + 1] = pltpu.roll(y, HEAD_DIM, 1)
    n_grp = tm // CMP_STRIDE
    low_g = lax.broadcasted_iota(jnp.int32, (n_grp, LANES), 1) < HEAD_DIM
    for a in range(2):
        for j in range(CMP_STRIDE // 2):
            token = lambda l, s: stage_ref[2 * a + s, pl.ds(l, n_grp, stride=CMP_STRIDE), :]
            even, odd = 2 * j, 2 * j + 1
            cmp_ref[a, 0, :, j * LANES:(j + 1) * LANES] = jnp.where(
                low_g, token(even, 0), token(odd, 1)).astype(BF16)
            cmp_ref[a, 1, :, j * LANES:(j + 1) * LANES] = jnp.where(
                low_g, token(even, 1), token(odd, 0)).astype(BF16)
    for n, y in enumerate(halves(9)):
        y = rope_n(y)
        sw = pltpu.roll(y, HEAD_DIM, 1)
        kdup_ref[:, (2 * n) * LANES:(2 * n + 1) * LANES] = jnp.where(low, y, sw).astype(BF16)
        kdup_ref[:, (2 * n + 1) * LANES:(2 * n + 2) * LANES] = jnp.where(low, sw, y).astype(BF16)
    for ci in range(2):
        for h, y in enumerate(halves(6 + ci)):
            c0 = ci * 256 + h * LANES
            rg_ref[:, c0:c0 + LANES] = y


def _in_proj(x2, positions, norm_w, w_main, w_t, tab, freq, tm):
    T, D = x2.shape
    nv = 2 * KV_WIDTH
    ng = w_t.shape[0] - nv - NSA_WIDTH
    grp_w = CMP_STRIDE * HEAD_DIM
    pos_rows = positions.reshape(T // tm, 1, tm)
    row = lambda w: pl.BlockSpec((tm, w), lambda i: (i, 0))
    col = lambda h: pl.BlockSpec((h, tm), lambda i: (0, i))
    full = lambda a: pl.BlockSpec(a.shape, lambda i: (0,) * a.ndim)
    return pl.pallas_call(
        _in_proj_kernel,
        grid=(T // tm,),
        in_specs=[row(D), pl.BlockSpec((1, 1, tm), lambda i: (i, 0, 0)),
                  full(norm_w), full(w_main), full(w_t), full(tab), full(freq)],
        out_specs=[row(3 * RET_WIDTH), row(RET_WIDTH), col(NSA_WIDTH),
                   pl.BlockSpec((2, NSA_KV_HEADS, tm // CMP_STRIDE, grp_w), lambda i: (0, 0, i, 0)),
                   row(4 * LANES), col(nv), col(ng)],
        out_shape=[jax.ShapeDtypeStruct((T, 3 * RET_WIDTH), BF16),
                   jax.ShapeDtypeStruct((T, RET_WIDTH), F32),
                   jax.ShapeDtypeStruct((NSA_WIDTH, T), BF16),
                   jax.ShapeDtypeStruct((2, NSA_KV_HEADS, T // CMP_STRIDE, grp_w), BF16),
                   jax.ShapeDtypeStruct((T, 4 * LANES), BF16),
                   jax.ShapeDtypeStruct((nv, T), BF16),
                   jax.ShapeDtypeStruct((ng, T), F32)],
        scratch_shapes=[pltpu.VMEM((4, tm, LANES), F32)],
        compiler_params=_params("parallel"),
    )(x2, pos_rows, norm_w, w_main, w_t, tab, freq)


def _in_proj_weights(w_in):
    w_main = jnp.concatenate([w_in[:, :_NQ], w_in[:, _KC:_KSL], w_in[:, _KSL:_VSL],
                              w_in[:, _KWN:_VWN]], axis=1)
    gate_cols = np.full((NSA_KV_HEADS * GATE_ROWS,), -1)
    for g in range(NSA_KV_HEADS):
        for r in range(NSA_GROUP):
            for br in range(N_BRANCH):
                gate_cols[g * GATE_ROWS + r * N_BRANCH + br] = _NG + (g * NSA_GROUP + r) * N_BRANCH + br
    w_gate = jnp.where(gate_cols[None, :] >= 0, w_in[:, np.maximum(gate_cols, 0)], 0.0)
    w_t = jnp.concatenate([w_in[:, _NQ:_KC], w_in[:, _VSL:_KWN], w_in[:, _VWN:_NG], w_gate],
                          axis=1).T
    return w_main.astype(BF16), w_t.astype(BF16)


def _retention_kernel(q_ref, k_ref, v_ref, rg_ref, gnw_ref, dec_ref, qd_ref, kd_ref, cdm_ref,
                      o_ref, state_ref, *, n_chunks, unroll):
    C = RET_CHUNK
    lane = lax.broadcasted_iota(jnp.int32, (C, LANES), 1)
    low = lane < HEAD_DIM
    r_i = lax.broadcasted_iota(jnp.int32, (LANES, LANES), 0)
    c_i = lax.broadcasted_iota(jnp.int32, (LANES, LANES), 1)
    same_head = (r_i < HEAD_DIM) == (c_i < HEAD_DIM)
    state_ref[...] = jnp.zeros_like(state_ref)

    def group(gi, carry):
        rows = [pl.multiple_of((gi * unroll + u) * C, C) for u in range(unroll)]
        q = [q_ref[pl.ds(r0, C), :] for r0 in rows]
        k = [k_ref[pl.ds(r0, C), :] for r0 in rows]
        v = [v_ref[pl.ds(r0, C), :] for r0 in rows]
        zero = jnp.zeros_like(q[0])
        s0 = [_dot_nt(jnp.where(low, q[u], zero), k[u]) for u in range(unroll)]
        s1 = [_dot_nt(jnp.where(low, zero, q[u]), k[u]) for u in range(unroll)]
        kv = [_dot_tn((k[u].astype(F32) * kd_ref[...]).astype(BF16), v[u]) for u in range(unroll)]

        p0 = [(s0[u] * dec_ref[0]).astype(BF16) for u in range(unroll)]
        p1 = [(s1[u] * dec_ref[1]).astype(BF16) for u in range(unroll)]
        states = [state_ref[...]]
        for u in range(unroll):
            states.append(states[u] * cdm_ref[0] + jnp.where(same_head, kv[u], 0.0))
        state_ref[...] = states[unroll]

        o0 = [_dot(p0[u], v[u]) for u in range(unroll)]
        o1 = [_dot(p1[u], v[u]) for u in range(unroll)]
        oc = [_dot((q[u].astype(F32) * qd_ref[...]).astype(BF16), states[u].astype(BF16))
              for u in range(unroll)]

        inv_d = 1.0 / HEAD_DIM
        for u in range(unroll):
            ro = jnp.where(low, o0[u], o1[u]) + oc[u]
            sum0 = jnp.sum(jnp.where(low, ro, 0.0), axis=-1, keepdims=True)
            sum1 = jnp.sum(jnp.where(low, 0.0, ro), axis=-1, keepdims=True)
            d = ro - jnp.where(low, sum0, sum1) * inv_d
            dd = d * d
            var0 = jnp.sum(jnp.where(low, dd, 0.0), axis=-1, keepdims=True)
            var1 = jnp.sum(jnp.where(low, 0.0, dd), axis=-1, keepdims=True)
            var = jnp.where(low, var0, var1) * inv_d
            y = d * lax.rsqrt(var + EPS) * gnw_ref[...]
            g = rg_ref[pl.ds(rows[u], C), :]
            o_ref[pl.ds(rows[u], C), :] = (y * (g * jax.nn.sigmoid(g))).astype(BF16)
        return carry

    lax.fori_loop(0, n_chunks // unroll, group, 0)


def _retention(ret_qkv, rg, gn_w, dec, qd, kd, cdm, B, S):
    T = B * S
    n_pairs = RET_HEADS // HEADS_PER_VREG
    seq = lambda off: pl.BlockSpec((S, LANES), lambda b, hp: (b, off + hp))
    return pl.pallas_call(
        functools.partial(_retention_kernel, n_chunks=S // RET_CHUNK,
                          unroll=_row_tile(S // RET_CHUNK, 8)),
        grid=(B, n_pairs),
        in_specs=[seq(0), seq(n_pairs), seq(2 * n_pairs), seq(0),
                  pl.BlockSpec((1, LANES), lambda b, hp: (0, hp)),
                  pl.BlockSpec((HEADS_PER_VREG, RET_CHUNK, RET_CHUNK), lambda b, hp: (hp, 0, 0)),
                  pl.BlockSpec((RET_CHUNK, LANES), lambda b, hp: (0, hp)),
                  pl.BlockSpec((RET_CHUNK, LANES), lambda b, hp: (0, hp)),
                  pl.BlockSpec((1, LANES, LANES), lambda b, hp: (hp, 0, 0))],
        out_specs=seq(0),
        out_shape=jax.ShapeDtypeStruct((T, RET_WIDTH), BF16),
        scratch_shapes=[pltpu.VMEM((LANES, LANES), F32)],
        compiler_params=_params("parallel", "parallel"),
    )(ret_qkv, ret_qkv, ret_qkv, rg, gn_w, dec, qd, kd, cdm)


def _retention_constants():
    H, C = RET_HEADS, RET_CHUNK
    log_g = jnp.log1p(-jnp.exp2(-5.0 - jnp.arange(H, dtype=F32)))
    pos = jnp.arange(C, dtype=F32)
    diff = pos[:, None] - pos[None, :]
    dec = jnp.where(diff >= 0, jnp.exp(jnp.maximum(diff, 0.0) * log_g[:, None, None]), 0.0)
    q_decay = jnp.exp((pos + 1.0) * log_g[:, None]).T
    k_decay = jnp.exp((C - 1.0 - pos) * log_g[:, None]).T
    chunk_decay = jnp.exp(C * log_g)
    qd = jnp.repeat(q_decay, HEAD_DIM, axis=1)
    kd = jnp.repeat(k_decay, HEAD_DIM, axis=1)
    cd_rows = jnp.repeat(chunk_decay, HEAD_DIM).reshape(H // HEADS_PER_VREG, LANES, 1)
    head_of = jnp.arange(LANES) // HEAD_DIM
    same = (head_of[:, None] == head_of[None, :]).astype(F32)
    return dec, qd, kd, cd_rows * same[None]


def _compress_kernel(tk_ref, tv_ref, pos_ref, w1_ref, w2k_ref, w2vt_ref, ck_ref, cvt_ref):
    pos = jnp.broadcast_to(pos_ref[...], (8, pos_ref.shape[1])).astype(BF16)

    def hidden(t, w1_ref_a):
        half = t.shape[1]
        bias = _dot(pos, w1_ref_a[...])[0:1, :]
        first = _dot(t, w1_ref_a[:half, :])
        second = _dot(t, w1_ref_a[half:, :])
        n = t.shape[0]
        h = first + pltpu.roll(second, n - 1, 0) + bias
        return jax.nn.gelu(h).astype(BF16)

    ck_ref[0, 0] = _dot(hidden(tk_ref[0, 0], w1_ref.at[0]), w2k_ref[...]).astype(BF16)
    cvt_ref[0, 0] = _dot_nt(w2vt_ref[...], hidden(tv_ref[0, 0], w1_ref.at[1])).astype(BF16)


def _compress(cmp_tok, pos_flat, w1, w2k, w2vt, B):
    _, G, rows, width = cmp_tok.shape
    n = rows // B
    tok = lambda a: pl.BlockSpec((1, 1, n, width), lambda b, g: (a, g, b, 0))
    full = lambda a: pl.BlockSpec(a.shape, lambda b, g: (0,) * a.ndim)
    return pl.pallas_call(
        _compress_kernel,
        grid=(B, G),
        in_specs=[tok(0), tok(1), full(pos_flat), full(w1), full(w2k), full(w2vt)],
        out_specs=[pl.BlockSpec((1, 1, n, LANES), lambda b, g: (b, g, 0, 0)),
                   pl.BlockSpec((1, 1, HEAD_DIM, n), lambda b, g: (b, g, 0, 0))],
        out_shape=[jax.ShapeDtypeStruct((B, G, n, LANES), BF16),
                   jax.ShapeDtypeStruct((B, G, HEAD_DIM, n), BF16)],
        compiler_params=_params("parallel", "parallel"),
    )(cmp_tok, cmp_tok, pos_flat, w1, w2k, w2vt)


def _nsa_kernel(q_ref, ck_ref, cvt_ref, ks_ref, kw_ref, vst_ref, vwt_ref, gt_ref, ovt_ref, eb_ref,
                o_ref, m_ref, acc_ref, out_ref, score_ref, ahead_ref, sa_ref, sb_ref, sw_ref,
                *, tq, top):
    R = NSA_GROUP
    D = HEAD_DIM
    t0 = pl.multiple_of(pl.program_id(2) * tq, tq)
    t_off = lax.broadcasted_iota(jnp.int32, (1, tq), 1)
    tcol = t0 + t_off
    k_off = lax.broadcasted_iota(jnp.int32, (tq, 1), 0)
    low = lax.broadcasted_iota(jnp.int32, (tq, LANES), 1) < D
    ones_rows = jnp.ones((BF16_ROWS, tq), BF16)
    gate = lambda r, br: gt_ref[r * N_BRANCH + br:r * N_BRANCH + br + 1, :]

    def q_aug(r, extra):
        return jnp.concatenate([q_ref[r * D:(r + 1) * D, :], extra], axis=0)

    zero_rows = jnp.zeros((LANES - D, tq), BF16)

    def value_rows(vt_ref_, start):
        return jnp.concatenate([vt_ref_[:, pl.ds(start, tq)], ones_rows], axis=0)

    def more(r, s_t, vrows):
        m_old = m_ref[r]
        m_new = jnp.maximum(m_old, jnp.max(s_t, axis=0, keepdims=True))
        m_ref[r] = m_new
        acc_ref[r] = (jnp.exp2(m_old - m_new) * acc_ref[r]
                      + _dot(vrows, jnp.exp2(s_t - m_new).astype(BF16)))

    def mix(r, br):
        o = acc_ref[r, :D, :] * ((1.0 / acc_ref[r, D:D + 1, :]) * gate(r, br))
        out_ref[r * D:(r + 1) * D, :] = out_ref[r * D:(r + 1) * D, :] + o

    ck = ck_ref[0, 0]
    ncp = ck.shape[0]
    cvt = jnp.concatenate([cvt_ref[0, 0], jnp.ones((BF16_ROWS, ncp), BF16)], axis=0)
    cmp_end = lax.broadcasted_iota(jnp.int32, (ncp, 1), 0) * CMP_STRIDE + (CMP_BLOCK - 1)
    cmask = cmp_end <= tcol
    sees_any = jnp.where(tcol >= CMP_BLOCK - 1, 1.0, 0.0)
    n_sel = ovt_ref.shape[0]
    imp_t = jnp.zeros((n_sel, tq), F32)
    scores = [_dot(ck, q_aug(r, zero_rows)) for r in range(R)]

    n_back = WINDOW // tq
    win_starts = [pl.multiple_of(jnp.maximum(t0 - j * tq, 0), tq) for j in range(n_back + 1)]
    flag_lane = lax.broadcasted_iota(jnp.int32, (tq, LANES), 1) == D
    flag_row = lax.broadcasted_iota(jnp.int32, (LANES - D, tq), 0) == 0
    qw = [q_aug(r, jnp.where(flag_row, 1.0, 0.0).astype(BF16)) for r in range(R)]

    def issue_window(j):
        off = jnp.full((tq, LANES), jnp.where(t0 >= j * tq, 0.0, MASKED), F32).astype(BF16)
        kw = jnp.where(flag_lane, off, kw_ref[0, pl.ds(win_starts[j], tq), :])
        for r in range(R):
            sw_ref[j, r] = _dot(kw, qw[r])

    for r in range(R):
        for j in range(r, n_back + 1, R):
            issue_window(j)
        s = jnp.where(cmask, scores[r], MASKED)
        e = jnp.exp2(s - jnp.max(s, axis=0, keepdims=True)).astype(BF16)
        acc = _dot(cvt, e)
        inv_l = sees_any * (1.0 / acc[D:D + 1, :])
        out_ref[r * D:(r + 1) * D, :] = acc[:D, :] * (inv_l * gate(r, 0))
        imp_t = imp_t + _dot(ovt_ref[...], e) * inv_l

    blk_i = lax.broadcasted_iota(jnp.int32, (n_sel, tq), 0)
    cur = tcol // SEL_BLOCK
    valid = blk_i * SEL_BLOCK <= tcol
    forced = (blk_i == 0) | (blk_i == cur) | (blk_i == cur - 1)
    score_ref[...] = jnp.where(forced, jnp.inf, jnp.where(valid, imp_t, -jnp.inf))
    ahead_ref[...] = jnp.zeros(ahead_ref.shape, F32)
    live_blocks = (t0 + tq) // SEL_BLOCK

    def count(jg, kb_lo, kb_hi):
        mine = score_ref[jg * 8:(jg + 1) * 8, :]
        blk_g = jg * 8 + lax.broadcasted_iota(jnp.int32, (8, tq), 0)
        ahead = ahead_ref[jg * 8:(jg + 1) * 8, :]
        for kb in range(kb_lo, kb_hi):
            other = score_ref[kb:kb + 1, :]
            if kb < jg * 8:
                ahead = ahead + jnp.where(other >= mine, 1.0, 0.0)
            elif kb >= (jg + 1) * 8:
                ahead = ahead + jnp.where(other > mine, 1.0, 0.0)
            else:
                ahead = ahead + jnp.where(blk_g > kb, jnp.where(other >= mine, 1.0, 0.0),
                                          jnp.where(other > mine, 1.0, 0.0))
        ahead_ref[jg * 8:(jg + 1) * 8, :] = ahead

    def rank_level(lo, hi):
        for jg in range(lo // 8):
            count(jg, lo, hi)
        for jg in range(lo // 8, hi // 8):
            count(jg, 0, hi)

    step = RANK_STEP if n_sel % RANK_STEP == 0 else n_sel
    rank_level(0, step)
    for lo in range(step, n_sel, step):
        @pl.when(live_blocks > lo)
        def _():
            rank_level(lo, lo + step)

    pad_rows = LANES - D - n_sel
    not_sel = jnp.where(ahead_ref[...] < top, 0.0, 1.0)
    if pad_rows:
        not_sel = jnp.concatenate([not_sel, jnp.zeros((pad_rows, tq), F32)], axis=0)
    not_sel = not_sel.astype(BF16)
    qa = [q_aug(r, not_sel) for r in range(R)]

    def keys_aug(start):
        return jnp.where(low, ks_ref[0, pl.ds(start, tq), :], eb_ref[pl.ds(start, tq), :])

    def issue(c, buf_ref):
        ka = keys_aug(pl.multiple_of(c * tq, tq))
        for r in range(R):
            buf_ref[r] = _dot(ka, qa[r])

    def consume(c, buf_ref, own):
        vr = value_rows(vst_ref, pl.multiple_of(c * tq, tq))
        for r in range(R):
            more(r, jnp.where(causal, buf_ref[r], MASKED) if own else buf_ref[r], vr)

    causal = k_off <= t_off
    n_plain = pl.program_id(2)
    for r in range(R):
        m_ref[r] = jnp.full((1, tq), LOWEST, F32)
        acc_ref[r] = jnp.zeros(acc_ref.shape[1:], F32)
    issue(0, sa_ref)
    bufs = (sa_ref, sb_ref)

    def sel_trip(k, carry):
        for u in range(SEL_UNROLL):
            c = SEL_UNROLL * k + u
            issue(c + 1, bufs[(u + 1) % 2])
            consume(c, bufs[u % 2], False)
        return carry

    lax.fori_loop(0, n_plain // SEL_UNROLL, sel_trip, 0)

    for rem in range(SEL_UNROLL):
        @pl.when(n_plain % SEL_UNROLL == rem)
        def _():
            for u in range(rem):
                c = n_plain - rem + u
                issue(c + 1, bufs[(u + 1) % 2])
                consume(c, bufs[u % 2], False)
            consume(n_plain, bufs[rem % 2], True)

    for r in range(R):
        mix(r, 1)

    for r in range(R):
        m_ref[r] = jnp.full((1, tq), LOWEST, F32)
        acc_ref[r] = jnp.zeros(acc_ref.shape[1:], F32)
    for j in range(n_back + 1):
        vr = value_rows(vwt_ref, win_starts[j])
        for r in range(R):
            s_t = sw_ref[j, r]
            if j == 0:
                s_t = jnp.where(causal, s_t, MASKED)
            elif j == n_back:
                s_t = jnp.where(k_off > t_off, s_t, MASKED)
            more(r, s_t, vr)
    for r in range(R):
        mix(r, 2)

    o_ref[...] = out_ref[...].T.astype(BF16)


def _nsa(nq, ck, cvt, kdup, vt, gt, ovt, eb, B, S, tq):
    T = B * S
    G = NSA_KV_HEADS
    assert S % tq == 0 and WINDOW % tq == 0 and S // SEL_BLOCK <= LANES - HEAD_DIM
    nqt = S // tq
    ncp = ck.shape[2]
    kdup3 = kdup.reshape(B, S, kdup.shape[1])
    acc_rows = HEAD_DIM + BF16_ROWS
    keys = lambda off: pl.BlockSpec((1, S, LANES), lambda b, g, i: (b, 0, off + g))
    vals = lambda off: pl.BlockSpec((HEAD_DIM, S), lambda b, g, i: (off + g, b))
    full = lambda a: pl.BlockSpec(a.shape, lambda b, g, i: (0,) * a.ndim)
    return pl.pallas_call(
        functools.partial(_nsa_kernel, tq=tq, top=min(SEL_TOPK, S // SEL_BLOCK)),
        grid=(B, G, nqt),
        in_specs=[pl.BlockSpec((NSA_GROUP * HEAD_DIM, tq), lambda b, g, i: (g, b * nqt + i)),
                  pl.BlockSpec((1, 1, ncp, LANES), lambda b, g, i: (b, g, 0, 0)),
                  pl.BlockSpec((1, 1, HEAD_DIM, ncp), lambda b, g, i: (b, g, 0, 0)),
                  keys(0), keys(G), vals(0), vals(G),
                  pl.BlockSpec((GATE_ROWS, tq), lambda b, g, i: (g, b * nqt + i)),
                  full(ovt), full(eb)],
        out_specs=pl.BlockSpec((tq, NSA_GROUP * HEAD_DIM), lambda b, g, i: (b * nqt + i, g)),
        out_shape=jax.ShapeDtypeStruct((T, NSA_WIDTH), BF16),
        scratch_shapes=[pltpu.VMEM((NSA_GROUP, 1, tq), F32),
                        pltpu.VMEM((NSA_GROUP, acc_rows, tq), F32),
                        pltpu.VMEM((NSA_GROUP * HEAD_DIM, tq), F32),
                        pltpu.VMEM((S // SEL_BLOCK, tq), F32),
                        pltpu.VMEM((S // SEL_BLOCK, tq), F32),
                        pltpu.VMEM((NSA_GROUP, tq, tq), F32),
                        pltpu.VMEM((NSA_GROUP, tq, tq), F32),
                        pltpu.VMEM((WINDOW // tq + 1, NSA_GROUP, tq, tq), F32)],
        compiler_params=_params("parallel", "parallel", "parallel"),
    )(nq, ck, cvt, kdup3, kdup3, vt, vt, gt, ovt, eb)


def _nsa_constants(S):
    n_cmp = (S - CMP_BLOCK) // CMP_STRIDE + 1
    ncp = S // CMP_STRIDE
    n_sel = S // SEL_BLOCK
    ci = np.arange(ncp)[None, :]
    sj = np.arange(n_sel)[:, None]
    ovt = ((ci * CMP_STRIDE < (sj + 1) * SEL_BLOCK)
           & (ci * CMP_STRIDE + CMP_BLOCK > sj * SEL_BLOCK) & (ci < n_cmp))
    eb = np.zeros((S, LANES), np.float32)
    eb[np.arange(S), HEAD_DIM + np.arange(S) // SEL_BLOCK] = MASKED
    return jnp.asarray(ovt, BF16), jnp.asarray(eb, BF16)


def _mlp_kernel(ret_ref, nsa_ref, x_ref, p_ref, wo_ref, nw_ref, wg_ref, wv_ref, cw_ref, cb_ref,
                wd_ref, pg_ref, pw_ref, fw_ref, o_ref, h_ref, act_ref, tail_ref, *, fc, final):
    @pl.when(pl.program_id(1) == 0)
    def _():
        tail_ref[...] = jnp.zeros_like(tail_ref)

    mix = _dot(ret_ref[...], wo_ref[:RET_WIDTH, :]) + _dot(nsa_ref[...], wo_ref[RET_WIDTH:, :])
    h = x_ref[...] + mix
    h_ref[...] = h
    ms = jnp.mean(h * h, axis=-1, keepdims=True)
    hn = (h * lax.rsqrt(ms + EPS) * nw_ref[...]).astype(BF16)

    tm = hn.shape[0]
    row = lax.broadcasted_iota(jnp.int32, (tm, fc), 0)
    for c in range(wg_ref.shape[1] // fc):
        sl = slice(c * fc, (c + 1) * fc)
        g = _dot(hn, wg_ref[:, sl])
        val = _dot(hn, wv_ref[:, sl])
        tail = tail_ref[:, sl]
        g1 = jnp.where(row == 0, tail[7:8, :], pltpu.roll(g, 1, 0))
        g2 = jnp.where(row == 0, tail[6:7, :],
                       jnp.where(row == 1, tail[7:8, :], pltpu.roll(g, 2, 0)))
        tail_ref[:, sl] = g[tm - 8:, :]
        cw = cw_ref[:, sl]
        y = cb_ref[:, sl] + cw[0:1, :] * g2 + cw[1:2, :] * g1 + cw[2:3, :] * g
        act_ref[:, sl] = (y * jax.nn.sigmoid(y) * val).astype(BF16)

    h = h_ref[...] + _dot(act_ref[...], wd_ref[...])
    gate = jax.nn.sigmoid(_dot(h.astype(BF16), pg_ref[...]))
    h = h + gate * _dot(p_ref[...].astype(BF16), pw_ref[...])
    if final:
        ms = jnp.mean(h * h, axis=-1, keepdims=True)
        h = h * lax.rsqrt(ms + EPS) * fw_ref[...]
    o_ref[...] = h


def _mlp(ret_out, nsa_out, x2, p2, w_out, norm_w, wg, wv, conv_w, conv_b, wd, pg, pw, fw,
         B, S, tm, final):
    T, D = x2.shape
    F = wg.shape[1]
    nt = S // tm
    row = lambda w: pl.BlockSpec((tm, w), lambda b, j: (b * nt + j, 0))
    full = lambda a: pl.BlockSpec(a.shape, lambda b, j: (0,) * a.ndim, pipeline_mode=pl.Buffered(1))
    weights = (w_out, norm_w, wg, wv, conv_w, conv_b, wd, pg, pw, fw)
    return pl.pallas_call(
        functools.partial(_mlp_kernel, fc=256, final=final),
        grid=(B, nt),
        in_specs=[row(RET_WIDTH), row(NSA_WIDTH), row(D), row(p2.shape[1])]
                 + [full(w) for w in weights],
        out_specs=row(D),
        out_shape=jax.ShapeDtypeStruct((T, D), F32),
        scratch_shapes=[pltpu.VMEM((tm, D), F32), pltpu.VMEM((tm, F), BF16),
                        pltpu.VMEM((8, F), F32)],
        compiler_params=_params("arbitrary", "arbitrary"),
    )(ret_out, nsa_out, x2, p2, *weights)


def _rope_tables(tm):
    lane = np.arange(LANES) % HEAD_DIM
    half_r = HEAD_DIM // 2
    inv_r = jnp.power(jnp.float32(RET_THETA), -jnp.arange(half_r, dtype=F32) / half_r)
    half_n = ROPE_DIM // 2
    inv_n = jnp.power(jnp.float32(ROPE_THETA), -jnp.arange(half_n, dtype=F32) / half_n)
    signs = [jnp.asarray(np.where(lane < half_r, -1.0, 1.0), F32),
             jnp.asarray(np.where(lane < half_n, -1.0, 1.0), F32)]
    table = jnp.concatenate([jnp.stack(signs), jnp.zeros((6, LANES), F32)], axis=0)
    freq = jnp.concatenate([inv_r, inv_n])
    return table, jnp.broadcast_to(freq[:, None], (half_r + half_n, tm))


def _layer(h2, p2, pos2, B, S, final, norm_mix_w, w_in, ret_gn_w, cmp_pos, cmp_k_w1, cmp_k_w2,
           cmp_v_w1, cmp_v_w2, w_out, norm_ffn_w, ffn_w_up, ffn_conv_w, ffn_conv_b, ffn_w_down,
           ple_w, ple_gate_w, final_norm_w):
    T, D = h2.shape
    tm = _row_tile(S, 512)

    w_main, w_t = _in_proj_weights(w_in)
    ret_qkv, rg, qt, cmp_tok, kdup, vt, gt = _in_proj(
        h2, pos2, norm_mix_w.reshape(1, D), w_main, w_t, *_rope_tables(tm), tm)

    dec, qd, kd, cdm = _retention_constants()
    ret_out = _retention(ret_qkv, rg, ret_gn_w.reshape(1, RET_WIDTH), dec, qd, kd, cdm, B, S)

    w1 = jnp.stack([cmp_k_w1, cmp_v_w1]).astype(BF16)
    w2k = jnp.tile(cmp_k_w2, (1, HEADS_PER_VREG)).astype(BF16)
    ck, cvt = _compress(cmp_tok, cmp_pos.reshape(1, CMP_BLOCK * HEAD_DIM), w1, w2k,
                        cmp_v_w2.T.astype(BF16), B)

    ovt, eb = _nsa_constants(S)
    nsa_out = _nsa(qt, ck, cvt, kdup, vt, gt, ovt, eb, B, S, tq=256)

    d_ff = ffn_w_down.shape[0]
    conv_w = jnp.pad(ffn_conv_w, ((0, 8 - ffn_conv_w.shape[0]), (0, 0)))
    return _mlp(ret_out, nsa_out, h2, p2, w_out.astype(BF16), norm_ffn_w.reshape(1, D),
                ffn_w_up[:, :d_ff].astype(BF16), ffn_w_up[:, d_ff:].astype(BF16),
                conv_w, ffn_conv_b.reshape(1, d_ff), ffn_w_down.astype(BF16),
                ple_gate_w.astype(BF16), ple_w.astype(BF16), final_norm_w.reshape(1, D),
                B, S, tm, final)


def kernel(x, p, positions, norm_mix_w, w_in, ret_gn_w, cmp_pos, cmp_k_w1, cmp_k_w2, cmp_v_w1, cmp_v_w2, w_out, norm_ffn_w, ffn_w_up, ffn_conv_w, ffn_conv_b, ffn_w_down, ple_w, ple_gate_w, final_norm_w):
    B, S, D = x.shape
    T = B * S
    depth = w_in.shape[0]
    h = x.reshape(T, D)
    pos2 = positions.reshape(T).astype(jnp.int32)
    for i in range(depth):
        h = _layer(h, p[i].reshape(T, -1), pos2, B, S, i == depth - 1, norm_mix_w[i], w_in[i],
                   ret_gn_w[i], cmp_pos[i], cmp_k_w1[i], cmp_k_w2[i], cmp_v_w1[i], cmp_v_w2[i],
                   w_out[i], norm_ffn_w[i], ffn_w_up[i], ffn_conv_w[i], ffn_conv_b[i],
                   ffn_w_down[i], ple_w[i], ple_gate_w[i], final_norm_w)
    return h.reshape(B, S, D)
```

```python
import functools

import numpy as np
import jax
import jax.numpy as jnp
from jax import lax
from jax.experimental import pallas as pl
from jax.experimental.pallas import tpu as pltpu

F32 = jnp.float32
BF16 = jnp.bfloat16

LANES = 128
BF16_ROWS = 16
HEAD_DIM = 64
HEADS_PER_VREG = LANES // HEAD_DIM
RET_HEADS = 8
NSA_Q_HEADS = 8
NSA_KV_HEADS = 2
NSA_GROUP = NSA_Q_HEADS // NSA_KV_HEADS
RET_WIDTH = RET_HEADS * HEAD_DIM
NSA_WIDTH = NSA_Q_HEADS * HEAD_DIM
KV_WIDTH = NSA_KV_HEADS * HEAD_DIM
RET_CHUNK = 128
RET_THETA = 10000.0
ROPE_THETA = 500000.0
ROPE_DIM = HEAD_DIM // 4
CMP_BLOCK = 32
CMP_STRIDE = 16
SEL_BLOCK = 64
SEL_TOPK = 16
WINDOW = 512
N_BRANCH = 3
GATE_ROWS = 16
SEL_UNROLL = 4
EPS = 1e-6
QK_SCALE = HEAD_DIM ** -0.5
LOG2_E = 1.4426950408889634
MASKED = -1e30
LOWEST = -3e38

_OFF = np.cumsum([0, RET_WIDTH, RET_WIDTH, RET_WIDTH, RET_WIDTH, NSA_WIDTH,
                  KV_WIDTH, KV_WIDTH, KV_WIDTH, KV_WIDTH, KV_WIDTH, KV_WIDTH])
(_RQ, _RK, _RV, _RG, _NQ, _KC, _VC, _KSL, _VSL, _KWN, _VWN, _NG) = (int(v) for v in _OFF)

VMEM_LIMIT = 56 * 1024 * 1024


def _dot(a, b):
    return jnp.dot(a, b, preferred_element_type=F32)


def _dot_nt(a, b):
    return lax.dot_general(a, b, (((1,), (1,)), ((), ())), preferred_element_type=F32)


def _dot_tn(a, b):
    return lax.dot_general(a, b, (((0,), (0,)), ((), ())), preferred_element_type=F32)


def _params(*semantics):
    return pltpu.CompilerParams(dimension_semantics=semantics, vmem_limit_bytes=VMEM_LIMIT)


def _row_tile(n, pref):
    t = pref
    while n % t:
        t //= 2
    return t


def _rope(y, cos, sin_signed, first_half, half):
    nxt = pltpu.roll(y, LANES - half, 1)
    prv = pltpu.roll(y, half, 1)
    return y * cos + jnp.where(first_half, nxt, prv) * sin_signed


def _in_proj_kernel(x_ref, posrow_ref, nw_ref, w_ref, wt_ref, tab_ref, freq_ref,
                    ret_ref, rg_ref, qt_ref, cmp_ref, kdup_ref, vt_ref, gt_ref, stage_ref):
    x = x_ref[...]
    tm = x.shape[0]
    ms = jnp.mean(x * x, axis=-1, keepdims=True)
    xn = (x * lax.rsqrt(ms + EPS) * nw_ref[...]).astype(BF16)

    tab = tab_ref[...]
    lane = lax.broadcasted_iota(jnp.int32, (tm, LANES), 1)
    in_head = lane % HEAD_DIM
    low = lane < HEAD_DIM
    half_r, half_n = HEAD_DIM // 2, ROPE_DIM // 2
    ang_t = freq_ref[...] * posrow_ref[0].astype(F32)
    cos_t, sin_t = jnp.cos(ang_t), jnp.sin(ang_t)
    pad = jnp.zeros((LANES - half_r - half_n, tm), F32)
    cos_a = jnp.concatenate([cos_t, pad], axis=0).T
    sin_a = jnp.concatenate([sin_t, pad], axis=0).T
    cos_q, sin_q = cos_t[half_r:, :], sin_t[half_r:, :]
    first_r = in_head < half_r
    first_n = in_head < half_n
    rot_n = in_head < ROPE_DIM

    def spread_r(c):
        c = jnp.where(lane < half_r, c, pltpu.roll(c, half_r, 1))
        return jnp.where(low, c, pltpu.roll(c, HEAD_DIM, 1))

    def spread_n(c, rest):
        c = pltpu.roll(c, LANES - half_r, 1)
        c = jnp.where(lane < half_n, c, pltpu.roll(c, half_n, 1))
        c = jnp.where(low, c, pltpu.roll(c, HEAD_DIM, 1))
        return jnp.where(rot_n, c, rest)

    cos_r, sin_r = spread_r(cos_a), spread_r(sin_a) * tab[0:1, :]
    cos_n, sin_n = spread_n(cos_a, 1.0), spread_n(sin_a, 0.0) * tab[1:2, :]

    rope_r = lambda y: _rope(y, cos_r, sin_r, first_r, HEAD_DIM // 2)
    rope_n = lambda y: _rope(y, cos_n, sin_n, first_n, ROPE_DIM // 2)

    def halves(ci):
        y = _dot(xn, w_ref[:, ci * 256:(ci + 1) * 256])
        return y[:, :LANES], y[:, LANES:]

    for ci in range(2):
        for h, y in enumerate(halves(4 + ci)):
            c0 = 2 * RET_WIDTH + ci * 256 + h * LANES
            ret_ref[:, c0:c0 + LANES] = y.astype(BF16)
    t = _dot_nt(wt_ref[...], xn)
    for hd in range(NSA_Q_HEADS):
        b0 = hd * HEAD_DIM
        x1, x2 = t[b0:b0 + half_n, :], t[b0 + half_n:b0 + ROPE_DIM, :]
        head = jnp.concatenate([x1 * cos_q - x2 * sin_q, x1 * sin_q + x2 * cos_q,
                                t[b0 + ROPE_DIM:b0 + HEAD_DIM, :]], axis=0)
        qt_ref[b0:b0 + HEAD_DIM, :] = (head * (QK_SCALE * LOG2_E)).astype(BF16)
    nq, nv = qt_ref.shape[0], vt_ref.shape[0]
    vt_ref[...] = t[nq:nq + nv, :].astype(BF16)
    gt_ref[...] = jax.nn.sigmoid(t[nq + nv:, :])
    for ci in range(2):
        for h, y in enumerate(halves(ci)):
            c0 = ci * 256 + h * LANES
            ret_ref[:, c0:c0 + LANES] = rope_r(y).astype(BF16)
    for ci in range(2):
        for h, y in enumerate(halves(2 + ci)):
            c0 = RET_WIDTH + ci * 256 + h * LANES
            ret_ref[:, c0:c0 + LANES] = (rope_r(y) * QK_SCALE).astype(BF16)
    kc, vc = halves(8)
    for a, y in enumerate((rope_n(kc), vc)):
        stage_ref[2 * a] = y
        stage_ref[2 * a + 1] = pltpu.roll(y, HEAD_DIM, 1)
    n_grp = tm // CMP_STRIDE
    low_g = lax.broadcasted_iota(jnp.int32, (n_grp, LANES), 1) < HEAD_DIM
    for a in range(2):
        for j in range(CMP_STRIDE // 2):
            token = lambda l, s: stage_ref[2 * a + s, pl.ds(l, n_grp, stride=CMP_STRIDE), :]
            even, odd = 2 * j, 2 * j + 1
            cmp_ref[a, 0, :, j * LANES:(j + 1) * LANES] = jnp.where(
                low_g, token(even, 0), token(odd, 1)).astype(BF16)
            cmp_ref[a, 1, :, j * LANES:(j + 1) * LANES] = jnp.where(
                low_g, token(even, 1), token(odd, 0)).astype(BF16)
    for n, y in enumerate(halves(9)):
        y = rope_n(y)
        sw = pltpu.roll(y, HEAD_DIM, 1)
        kdup_ref[:, (2 * n) * LANES:(2 * n + 1) * LANES] = jnp.where(low, y, sw).astype(BF16)
        kdup_ref[:, (2 * n + 1) * LANES:(2 * n + 2) * LANES] = jnp.where(low, sw, y).astype(BF16)
    for ci in range(2):
        for h, y in enumerate(halves(6 + ci)):
            c0 = ci * 256 + h * LANES
            rg_ref[:, c0:c0 + LANES] = y


def _in_proj(x2, positions, norm_w, w_main, w_t, tab, freq, tm):
    T, D = x2.shape
    nv = 2 * KV_WIDTH
    ng = w_t.shape[0] - nv - NSA_WIDTH
    grp_w = CMP_STRIDE * HEAD_DIM
    pos_rows = positions.reshape(T // tm, 1, tm)
    row = lambda w: pl.BlockSpec((tm, w), lambda i: (i, 0))
    col = lambda h: pl.BlockSpec((h, tm), lambda i: (0, i))
    full = lambda a: pl.BlockSpec(a.shape, lambda i: (0,) * a.ndim)
    return pl.pallas_call(
        _in_proj_kernel,
        grid=(T // tm,),
        in_specs=[row(D), pl.BlockSpec((1, 1, tm), lambda i: (i, 0, 0)),
                  full(norm_w), full(w_main), full(w_t), full(tab), full(freq)],
        out_specs=[row(3 * RET_WIDTH), row(RET_WIDTH), col(NSA_WIDTH),
                   pl.BlockSpec((2, NSA_KV_HEADS, tm // CMP_STRIDE, grp_w), lambda i: (0, 0, i, 0)),
                   row(4 * LANES), col(nv), col(ng)],
        out_shape=[jax.ShapeDtypeStruct((T, 3 * RET_WIDTH), BF16),
                   jax.ShapeDtypeStruct((T, RET_WIDTH), F32),
                   jax.ShapeDtypeStruct((NSA_WIDTH, T), BF16),
                   jax.ShapeDtypeStruct((2, NSA_KV_HEADS, T // CMP_STRIDE, grp_w), BF16),
                   jax.ShapeDtypeStruct((T, 4 * LANES), BF16),
                   jax.ShapeDtypeStruct((nv, T), BF16),
                   jax.ShapeDtypeStruct((ng, T), F32)],
        scratch_shapes=[pltpu.VMEM((4, tm, LANES), F32)],
        compiler_params=_params("parallel"),
    )(x2, pos_rows, norm_w, w_main, w_t, tab, freq)


def _in_proj_weights(w_in):
    w_main = jnp.concatenate([w_in[:, :_NQ], w_in[:, _KC:_KSL], w_in[:, _KSL:_VSL],
                              w_in[:, _KWN:_VWN]], axis=1)
    gate_cols = np.full((NSA_KV_HEADS * GATE_ROWS,), -1)
    for g in range(NSA_KV_HEADS):
        for r in range(NSA_GROUP):
            for br in range(N_BRANCH):
                gate_cols[g * GATE_ROWS + r * N_BRANCH + br] = _NG + (g * NSA_GROUP + r) * N_BRANCH + br
    w_gate = jnp.where(gate_cols[None, :] >= 0, w_in[:, np.maximum(gate_cols, 0)], 0.0)
    w_t = jnp.concatenate([w_in[:, _NQ:_KC], w_in[:, _VSL:_KWN], w_in[:, _VWN:_NG], w_gate],
                          axis=1).T
    return w_main.astype(BF16), w_t.astype(BF16)


def _retention_kernel(q_ref, k_ref, v_ref, rg_ref, gnw_ref, dec_ref, qd_ref, kd_ref, cdm_ref,
                      o_ref, state_ref, *, n_chunks, unroll):
    C = RET_CHUNK
    lane = lax.broadcasted_iota(jnp.int32, (C, LANES), 1)
    low = lane < HEAD_DIM
    r_i = lax.broadcasted_iota(jnp.int32, (LANES, LANES), 0)
    c_i = lax.broadcasted_iota(jnp.int32, (LANES, LANES), 1)
    same_head = (r_i < HEAD_DIM) == (c_i < HEAD_DIM)
    state_ref[...] = jnp.zeros_like(state_ref)

    def group(gi, carry):
        rows = [pl.multiple_of((gi * unroll + u) * C, C) for u in range(unroll)]
        q = [q_ref[pl.ds(r0, C), :] for r0 in rows]
        k = [k_ref[pl.ds(r0, C), :] for r0 in rows]
        v = [v_ref[pl.ds(r0, C), :] for r0 in rows]
        zero = jnp.zeros_like(q[0])
        s0 = [_dot_nt(jnp.where(low, q[u], zero), k[u]) for u in range(unroll)]
        s1 = [_dot_nt(jnp.where(low, zero, q[u]), k[u]) for u in range(unroll)]
        kv = [_dot_tn((k[u].astype(F32) * kd_ref[...]).astype(BF16), v[u]) for u in range(unroll)]

        p0 = [(s0[u] * dec_ref[0]).astype(BF16) for u in range(unroll)]
        p1 = [(s1[u] * dec_ref[1]).astype(BF16) for u in range(unroll)]
        states = [state_ref[...]]
        for u in range(unroll):
            states.append(states[u] * cdm_ref[0] + jnp.where(same_head, kv[u], 0.0))
        state_ref[...] = states[unroll]

        o0 = [_dot(p0[u], v[u]) for u in range(unroll)]
        o1 = [_dot(p1[u], v[u]) for u in range(unroll)]
        oc = [_dot((q[u].astype(F32) * qd_ref[...]).astype(BF16), states[u].astype(BF16))
              for u in range(unroll)]

        inv_d = 1.0 / HEAD_DIM
        for u in range(unroll):
            ro = jnp.where(low, o0[u], o1[u]) + oc[u]
            sum0 = jnp.sum(jnp.where(low, ro, 0.0), axis=-1, keepdims=True)
            sum1 = jnp.sum(jnp.where(low, 0.0, ro), axis=-1, keepdims=True)
            d = ro - jnp.where(low, sum0, sum1) * inv_d
            dd = d * d
            var0 = jnp.sum(jnp.where(low, dd, 0.0), axis=-1, keepdims=True)
            var1 = jnp.sum(jnp.where(low, 0.0, dd), axis=-1, keepdims=True)
            var = jnp.where(low, var0, var1) * inv_d
            y = d * lax.rsqrt(var + EPS) * gnw_ref[...]
            g = rg_ref[pl.ds(rows[u], C), :]
            o_ref[pl.ds(rows[u], C), :] = (y * (g * jax.nn.sigmoid(g))).astype(BF16)
        return carry

    lax.fori_loop(0, n_chunks // unroll, group, 0)


def _retention(ret_qkv, rg, gn_w, dec, qd, kd, cdm, B, S):
    T = B * S
    n_pairs = RET_HEADS // HEADS_PER_VREG
    seq = lambda off: pl.BlockSpec((S, LANES), lambda b, hp: (b, off + hp))
    return pl.pallas_call(
        functools.partial(_retention_kernel, n_chunks=S // RET_CHUNK,
                          unroll=_row_tile(S // RET_CHUNK, 8)),
        grid=(B, n_pairs),
        in_specs=[seq(0), seq(n_pairs), seq(2 * n_pairs), seq(0),
                  pl.BlockSpec((1, LANES), lambda b, hp: (0, hp)),
                  pl.BlockSpec((HEADS_PER_VREG, RET_CHUNK, RET_CHUNK), lambda b, hp: (hp, 0, 0)),
                  pl.BlockSpec((RET_CHUNK, LANES), lambda b, hp: (0, hp)),
                  pl.BlockSpec((RET_CHUNK, LANES), lambda b, hp: (0, hp)),
                  pl.BlockSpec((1, LANES, LANES), lambda b, hp: (hp, 0, 0))],
        out_specs=seq(0),
        out_shape=jax.ShapeDtypeStruct((T, RET_WIDTH), BF16),
        scratch_shapes=[pltpu.VMEM((LANES, LANES), F32)],
        compiler_params=_params("parallel", "parallel"),
    )(ret_qkv, ret_qkv, ret_qkv, rg, gn_w, dec, qd, kd, cdm)


def _retention_constants():
    H, C = RET_HEADS, RET_CHUNK
    log_g = jnp.log1p(-jnp.exp2(-5.0 - jnp.arange(H, dtype=F32)))
    pos = jnp.arange(C, dtype=F32)
    diff = pos[:, None] - pos[None, :]
    dec = jnp.where(diff >= 0, jnp.exp(jnp.maximum(diff, 0.0) * log_g[:, None, None]), 0.0)
    q_decay = jnp.exp((pos + 1.0) * log_g[:, None]).T
    k_decay = jnp.exp((C - 1.0 - pos) * log_g[:, None]).T
    chunk_decay = jnp.exp(C * log_g)
    qd = jnp.repeat(q_decay, HEAD_DIM, axis=1)
    kd = jnp.repeat(k_decay, HEAD_DIM, axis=1)
    cd_rows = jnp.repeat(chunk_decay, HEAD_DIM).reshape(H // HEADS_PER_VREG, LANES, 1)
    head_of = jnp.arange(LANES) // HEAD_DIM
    same = (head_of[:, None] == head_of[None, :]).astype(F32)
    return dec, qd, kd, cd_rows * same[None]


def _compress_kernel(tk_ref, tv_ref, pos_ref, w1_ref, w2k_ref, w2vt_ref, ck_ref, cvt_ref):
    pos = jnp.broadcast_to(pos_ref[...], (8, pos_ref.shape[1])).astype(BF16)

    def hidden(t, w1_ref_a):
        half = t.shape[1]
        bias = _dot(pos, w1_ref_a[...])[0:1, :]
        first = _dot(t, w1_ref_a[:half, :])
        second = _dot(t, w1_ref_a[half:, :])
        n = t.shape[0]
        h = first + pltpu.roll(second, n - 1, 0) + bias
        return jax.nn.gelu(h).astype(BF16)

    ck_ref[0, 0] = _dot(hidden(tk_ref[0, 0], w1_ref.at[0]), w2k_ref[...]).astype(BF16)
    cvt_ref[0, 0] = _dot_nt(w2vt_ref[...], hidden(tv_ref[0, 0], w1_ref.at[1])).astype(BF16)


def _compress(cmp_tok, pos_flat, w1, w2k, w2vt, B):
    _, G, rows, width = cmp_tok.shape
    n = rows // B
    tok = lambda a: pl.BlockSpec((1, 1, n, width), lambda b, g: (a, g, b, 0))
    full = lambda a: pl.BlockSpec(a.shape, lambda b, g: (0,) * a.ndim)
    return pl.pallas_call(
        _compress_kernel,
        grid=(B, G),
        in_specs=[tok(0), tok(1), full(pos_flat), full(w1), full(w2k), full(w2vt)],
        out_specs=[pl.BlockSpec((1, 1, n, LANES), lambda b, g: (b, g, 0, 0)),
                   pl.BlockSpec((1, 1, HEAD_DIM, n), lambda b, g: (b, g, 0, 0))],
        out_shape=[jax.ShapeDtypeStruct((B, G, n, LANES), BF16),
                   jax.ShapeDtypeStruct((B, G, HEAD_DIM, n), BF16)],
        compiler_params=_params("parallel", "parallel"),
    )(cmp_tok, cmp_tok, pos_flat, w1, w2k, w2vt)


def _nsa_kernel(q_ref, ck_ref, cvt_ref, ks_ref, kw_ref, vst_ref, vwt_ref, gt_ref, ovt_ref, eb_ref,
                o_ref, m_ref, acc_ref, out_ref, score_ref, sa_ref, sb_ref, sw_ref,
                *, tq, top):
    R = NSA_GROUP
    D = HEAD_DIM
    t0 = pl.multiple_of(pl.program_id(2) * tq, tq)
    t_off = lax.broadcasted_iota(jnp.int32, (1, tq), 1)
    tcol = t0 + t_off
    k_off = lax.broadcasted_iota(jnp.int32, (tq, 1), 0)
    low = lax.broadcasted_iota(jnp.int32, (tq, LANES), 1) < D
    ones_rows = jnp.ones((BF16_ROWS, tq), BF16)
    gate = lambda r, br: gt_ref[r * N_BRANCH + br:r * N_BRANCH + br + 1, :]

    def q_aug(r, extra):
        return jnp.concatenate([q_ref[r * D:(r + 1) * D, :], extra], axis=0)

    zero_rows = jnp.zeros((LANES - D, tq), BF16)

    def value_rows(vt_ref_, start):
        return jnp.concatenate([vt_ref_[:, pl.ds(start, tq)], ones_rows], axis=0)

    def more(r, s_t, vrows):
        m_old = m_ref[r]
        m_new = jnp.maximum(m_old, jnp.max(s_t, axis=0, keepdims=True))
        m_ref[r] = m_new
        acc_ref[r] = (jnp.exp2(m_old - m_new) * acc_ref[r]
                      + _dot(vrows, jnp.exp2(s_t - m_new).astype(BF16)))

    def mix(r, br):
        o = acc_ref[r, :D, :] * ((1.0 / acc_ref[r, D:D + 1, :]) * gate(r, br))
        out_ref[r * D:(r + 1) * D, :] = out_ref[r * D:(r + 1) * D, :] + o

    ck = ck_ref[0, 0]
    ncp = ck.shape[0]
    cvt = jnp.concatenate([cvt_ref[0, 0], jnp.ones((BF16_ROWS, ncp), BF16)], axis=0)
    cmp_end = lax.broadcasted_iota(jnp.int32, (ncp, 1), 0) * CMP_STRIDE + (CMP_BLOCK - 1)
    cmask = cmp_end <= tcol
    sees_any = jnp.where(tcol >= CMP_BLOCK - 1, 1.0, 0.0)
    n_sel = ovt_ref.shape[0]
    imp_t = jnp.zeros((n_sel, tq), F32)
    scores = [_dot(ck, q_aug(r, zero_rows)) for r in range(R)]

    n_back = WINDOW // tq
    win_starts = [pl.multiple_of(jnp.maximum(t0 - j * tq, 0), tq) for j in range(n_back + 1)]
    flag_lane = lax.broadcasted_iota(jnp.int32, (tq, LANES), 1) == D
    flag_row = lax.broadcasted_iota(jnp.int32, (LANES - D, tq), 0) == 0
    qw = [q_aug(r, jnp.where(flag_row, 1.0, 0.0).astype(BF16)) for r in range(R)]

    def window_keys(j):
        off = jnp.full((tq, LANES), jnp.where(t0 >= j * tq, 0.0, MASKED), F32).astype(BF16)
        return jnp.where(flag_lane, off, kw_ref[0, pl.ds(win_starts[j], tq), :])

    for r in range(R):
        s = jnp.where(cmask, scores[r], MASKED)
        e = jnp.exp2(s - jnp.max(s, axis=0, keepdims=True)).astype(BF16)
        acc = _dot(cvt, e)
        inv_l = sees_any * (1.0 / acc[D:D + 1, :])
        out_ref[r * D:(r + 1) * D, :] = acc[:D, :] * (inv_l * gate(r, 0))
        imp_t = imp_t + _dot(ovt_ref[...], e) * inv_l

    blk_i = lax.broadcasted_iota(jnp.int32, (n_sel, tq), 0)
    cur = tcol // SEL_BLOCK
    valid = blk_i * SEL_BLOCK <= tcol
    forced = (blk_i == 0) | (blk_i == cur) | (blk_i == cur - 1)
    score_ref[...] = jnp.where(forced, jnp.inf, jnp.where(valid, imp_t, -jnp.inf))
    win_jobs = [(j, r) for j in range(n_back + 1) for r in range(R)]
    win_keys = {}
    groups = n_sel // 8
    per_group = -(-len(win_jobs) // groups)
    not_sel = []
    for jg in range(groups):
        for j, r in win_jobs[jg * per_group:(jg + 1) * per_group]:
            if j not in win_keys:
                win_keys[j] = window_keys(j)
            sw_ref[j, r] = _dot(win_keys[j], qw[r])
        mine = score_ref[jg * 8:(jg + 1) * 8, :]
        blk_g = jg * 8 + lax.broadcasted_iota(jnp.int32, (8, tq), 0)
        ahead = jnp.zeros((8, tq), F32)
        for kb in range(n_sel):
            other = score_ref[kb:kb + 1, :]
            if kb < jg * 8:
                ahead = ahead + jnp.where(other >= mine, 1.0, 0.0)
            elif kb >= (jg + 1) * 8:
                ahead = ahead + jnp.where(other > mine, 1.0, 0.0)
            else:
                ahead = ahead + jnp.where(blk_g > kb, jnp.where(other >= mine, 1.0, 0.0),
                                          jnp.where(other > mine, 1.0, 0.0))
        not_sel.append(jnp.where(ahead < top, 0.0, 1.0))
    pad_rows = LANES - D - n_sel
    if pad_rows:
        not_sel.append(jnp.zeros((pad_rows, tq), F32))
    not_sel = jnp.concatenate(not_sel, axis=0).astype(BF16)
    qa = [q_aug(r, not_sel) for r in range(R)]

    def keys_aug(start):
        return jnp.where(low, ks_ref[0, pl.ds(start, tq), :], eb_ref[pl.ds(start, tq), :])

    def issue(c, buf_ref):
        ka = keys_aug(pl.multiple_of(c * tq, tq))
        for r in range(R):
            buf_ref[r] = _dot(ka, qa[r])

    def consume(c, buf_ref, own):
        vr = value_rows(vst_ref, pl.multiple_of(c * tq, tq))
        for r in range(R):
            more(r, jnp.where(causal, buf_ref[r], MASKED) if own else buf_ref[r], vr)

    causal = k_off <= t_off
    n_plain = pl.program_id(2)
    for r in range(R):
        m_ref[r] = jnp.full((1, tq), LOWEST, F32)
        acc_ref[r] = jnp.zeros(acc_ref.shape[1:], F32)
    issue(0, sa_ref)
    bufs = (sa_ref, sb_ref)

    def sel_trip(k, carry):
        for u in range(SEL_UNROLL):
            c = SEL_UNROLL * k + u
            issue(c + 1, bufs[(u + 1) % 2])
            consume(c, bufs[u % 2], False)
        return carry

    lax.fori_loop(0, n_plain // SEL_UNROLL, sel_trip, 0)

    for rem in range(SEL_UNROLL):
        @pl.when(n_plain % SEL_UNROLL == rem)
        def _():
            for u in range(rem):
                c = n_plain - rem + u
                issue(c + 1, bufs[(u + 1) % 2])
                consume(c, bufs[u % 2], False)
            consume(n_plain, bufs[rem % 2], True)

    for r in range(R):
        mix(r, 1)

    for r in range(R):
        m_ref[r] = jnp.full((1, tq), LOWEST, F32)
        acc_ref[r] = jnp.zeros(acc_ref.shape[1:], F32)
    for j in range(n_back + 1):
        vr = value_rows(vwt_ref, win_starts[j])
        for r in range(R):
            s_t = sw_ref[j, r]
            if j == 0:
                s_t = jnp.where(causal, s_t, MASKED)
            elif j == n_back:
                s_t = jnp.where(k_off > t_off, s_t, MASKED)
            more(r, s_t, vr)
    for r in range(R):
        mix(r, 2)

    o_ref[...] = out_ref[...].T.astype(BF16)


def _nsa(nq, ck, cvt, kdup, vt, gt, ovt, eb, B, S, tq):
    T = B * S
    G = NSA_KV_HEADS
    assert S % tq == 0 and WINDOW % tq == 0 and S // SEL_BLOCK <= LANES - HEAD_DIM
    nqt = S // tq
    ncp = ck.shape[2]
    kdup3 = kdup.reshape(B, S, kdup.shape[1])
    acc_rows = HEAD_DIM + BF16_ROWS
    keys = lambda off: pl.BlockSpec((1, S, LANES), lambda b, g, i: (b, 0, off + g))
    vals = lambda off: pl.BlockSpec((HEAD_DIM, S), lambda b, g, i: (off + g, b))
    full = lambda a: pl.BlockSpec(a.shape, lambda b, g, i: (0,) * a.ndim)
    return pl.pallas_call(
        functools.partial(_nsa_kernel, tq=tq, top=min(SEL_TOPK, S // SEL_BLOCK)),
        grid=(B, G, nqt),
        in_specs=[pl.BlockSpec((NSA_GROUP * HEAD_DIM, tq), lambda b, g, i: (g, b * nqt + i)),
                  pl.BlockSpec((1, 1, ncp, LANES), lambda b, g, i: (b, g, 0, 0)),
                  pl.BlockSpec((1, 1, HEAD_DIM, ncp), lambda b, g, i: (b, g, 0, 0)),
                  keys(0), keys(G), vals(0), vals(G),
                  pl.BlockSpec((GATE_ROWS, tq), lambda b, g, i: (g, b * nqt + i)),
                  full(ovt), full(eb)],
        out_specs=pl.BlockSpec((tq, NSA_GROUP * HEAD_DIM), lambda b, g, i: (b * nqt + i, g)),
        out_shape=jax.ShapeDtypeStruct((T, NSA_WIDTH), BF16),
        scratch_shapes=[pltpu.VMEM((NSA_GROUP, 1, tq), F32),
                        pltpu.VMEM((NSA_GROUP, acc_rows, tq), F32),
                        pltpu.VMEM((NSA_GROUP * HEAD_DIM, tq), F32),
                        pltpu.VMEM((S // SEL_BLOCK, tq), F32),
                        pltpu.VMEM((NSA_GROUP, tq, tq), F32),
                        pltpu.VMEM((NSA_GROUP, tq, tq), F32),
                        pltpu.VMEM((WINDOW // tq + 1, NSA_GROUP, tq, tq), F32)],
        compiler_params=_params("parallel", "parallel", "parallel"),
    )(nq, ck, cvt, kdup3, kdup3, vt, vt, gt, ovt, eb)


def _nsa_constants(S):
    n_cmp = (S - CMP_BLOCK) // CMP_STRIDE + 1
    ncp = S // CMP_STRIDE
    n_sel = S // SEL_BLOCK
    ci = np.arange(ncp)[None, :]
    sj = np.arange(n_sel)[:, None]
    ovt = ((ci * CMP_STRIDE < (sj + 1) * SEL_BLOCK)
           & (ci * CMP_STRIDE + CMP_BLOCK > sj * SEL_BLOCK) & (ci < n_cmp))
    eb = np.zeros((S, LANES), np.float32)
    eb[np.arange(S), HEAD_DIM + np.arange(S) // SEL_BLOCK] = MASKED
    return jnp.asarray(ovt, BF16), jnp.asarray(eb, BF16)


def _mlp_kernel(ret_ref, nsa_ref, x_ref, p_ref, wo_ref, nw_ref, wg_ref, wv_ref, cw_ref, cb_ref,
                wd_ref, pg_ref, pw_ref, fw_ref, o_ref, h_ref, act_ref, tail_ref, *, fc, final):
    @pl.when(pl.program_id(1) == 0)
    def _():
        tail_ref[...] = jnp.zeros_like(tail_ref)

    mix = _dot(ret_ref[...], wo_ref[:RET_WIDTH, :]) + _dot(nsa_ref[...], wo_ref[RET_WIDTH:, :])
    h = x_ref[...] + mix
    h_ref[...] = h
    ms = jnp.mean(h * h, axis=-1, keepdims=True)
    hn = (h * lax.rsqrt(ms + EPS) * nw_ref[...]).astype(BF16)

    tm = hn.shape[0]
    row = lax.broadcasted_iota(jnp.int32, (tm, fc), 0)
    for c in range(wg_ref.shape[1] // fc):
        sl = slice(c * fc, (c + 1) * fc)
        g = _dot(hn, wg_ref[:, sl])
        val = _dot(hn, wv_ref[:, sl])
        tail = tail_ref[:, sl]
        g1 = jnp.where(row == 0, tail[7:8, :], pltpu.roll(g, 1, 0))
        g2 = jnp.where(row == 0, tail[6:7, :],
                       jnp.where(row == 1, tail[7:8, :], pltpu.roll(g, 2, 0)))
        tail_ref[:, sl] = g[tm - 8:, :]
        cw = cw_ref[:, sl]
        y = cb_ref[:, sl] + cw[0:1, :] * g2 + cw[1:2, :] * g1 + cw[2:3, :] * g
        act_ref[:, sl] = (y * jax.nn.sigmoid(y) * val).astype(BF16)

    h = h_ref[...] + _dot(act_ref[...], wd_ref[...])
    gate = jax.nn.sigmoid(_dot(h.astype(BF16), pg_ref[...]))
    h = h + gate * _dot(p_ref[...].astype(BF16), pw_ref[...])
    if final:
        ms = jnp.mean(h * h, axis=-1, keepdims=True)
        h = h * lax.rsqrt(ms + EPS) * fw_ref[...]
    o_ref[...] = h


def _mlp(ret_out, nsa_out, x2, p2, w_out, norm_w, wg, wv, conv_w, conv_b, wd, pg, pw, fw,
         B, S, tm, final):
    T, D = x2.shape
    F = wg.shape[1]
    nt = S // tm
    row = lambda w: pl.BlockSpec((tm, w), lambda b, j: (b * nt + j, 0))
    full = lambda a: pl.BlockSpec(a.shape, lambda b, j: (0,) * a.ndim, pipeline_mode=pl.Buffered(1))
    weights = (w_out, norm_w, wg, wv, conv_w, conv_b, wd, pg, pw, fw)
    return pl.pallas_call(
        functools.partial(_mlp_kernel, fc=256, final=final),
        grid=(B, nt),
        in_specs=[row(RET_WIDTH), row(NSA_WIDTH), row(D), row(p2.shape[1])]
                 + [full(w) for w in weights],
        out_specs=row(D),
        out_shape=jax.ShapeDtypeStruct((T, D), F32),
        scratch_shapes=[pltpu.VMEM((tm, D), F32), pltpu.VMEM((tm, F), BF16),
                        pltpu.VMEM((8, F), F32)],
        compiler_params=_params("arbitrary", "arbitrary"),
    )(ret_out, nsa_out, x2, p2, *weights)


def _rope_tables(tm):
    lane = np.arange(LANES) % HEAD_DIM
    half_r = HEAD_DIM // 2
    inv_r = jnp.power(jnp.float32(RET_THETA), -jnp.arange(half_r, dtype=F32) / half_r)
    half_n = ROPE_DIM // 2
    inv_n = jnp.power(jnp.float32(ROPE_THETA), -jnp.arange(half_n, dtype=F32) / half_n)
    signs = [jnp.asarray(np.where(lane < half_r, -1.0, 1.0), F32),
             jnp.asarray(np.where(lane < half_n, -1.0, 1.0), F32)]
    table = jnp.concatenate([jnp.stack(signs), jnp.zeros((6, LANES), F32)], axis=0)
    freq = jnp.concatenate([inv_r, inv_n])
    return table, jnp.broadcast_to(freq[:, None], (half_r + half_n, tm))


def _layer(h2, p2, pos2, B, S, final, norm_mix_w, w_in, ret_gn_w, cmp_pos, cmp_k_w1, cmp_k_w2,
           cmp_v_w1, cmp_v_w2, w_out, norm_ffn_w, ffn_w_up, ffn_conv_w, ffn_conv_b, ffn_w_down,
           ple_w, ple_gate_w, final_norm_w):
    T, D = h2.shape
    tm = _row_tile(S, 512)

    w_main, w_t = _in_proj_weights(w_in)
    ret_qkv, rg, qt, cmp_tok, kdup, vt, gt = _in_proj(
        h2, pos2, norm_mix_w.reshape(1, D), w_main, w_t, *_rope_tables(tm), tm)

    dec, qd, kd, cdm = _retention_constants()
    ret_out = _retention(ret_qkv, rg, ret_gn_w.reshape(1, RET_WIDTH), dec, qd, kd, cdm, B, S)

    w1 = jnp.stack([cmp_k_w1, cmp_v_w1]).astype(BF16)
    w2k = jnp.tile(cmp_k_w2, (1, HEADS_PER_VREG)).astype(BF16)
    ck, cvt = _compress(cmp_tok, cmp_pos.reshape(1, CMP_BLOCK * HEAD_DIM), w1, w2k,
                        cmp_v_w2.T.astype(BF16), B)

    ovt, eb = _nsa_constants(S)
    nsa_out = _nsa(qt, ck, cvt, kdup, vt, gt, ovt, eb, B, S, tq=256)

    d_ff = ffn_w_down.shape[0]
    conv_w = jnp.pad(ffn_conv_w, ((0, 8 - ffn_conv_w.shape[0]), (0, 0)))
    return _mlp(ret_out, nsa_out, h2, p2, w_out.astype(BF16), norm_ffn_w.reshape(1, D),
                ffn_w_up[:, :d_ff].astype(BF16), ffn_w_up[:, d_ff:].astype(BF16),
                conv_w, ffn_conv_b.reshape(1, d_ff), ffn_w_down.astype(BF16),
                ple_gate_w.astype(BF16), ple_w.astype(BF16), final_norm_w.reshape(1, D),
                B, S, tm, final)


def kernel(x, p, positions, norm_mix_w, w_in, ret_gn_w, cmp_pos, cmp_k_w1, cmp_k_w2, cmp_v_w1, cmp_v_w2, w_out, norm_ffn_w, ffn_w_up, ffn_conv_w, ffn_conv_b, ffn_w_down, ple_w, ple_gate_w, final_norm_w):
    B, S, D = x.shape
    T = B * S
    depth = w_in.shape[0]
    h = x.reshape(T, D)
    pos2 = positions.reshape(T).astype(jnp.int32)
    for i in range(depth):
        h = _layer(h, p[i].reshape(T, -1), pos2, B, S, i == depth - 1, norm_mix_w[i], w_in[i],
                   ret_gn_w[i], cmp_pos[i], cmp_k_w1[i], cmp_k_w2[i], cmp_v_w1[i], cmp_v_w2[i],
                   w_out[i], norm_ffn_w[i], ffn_w_up[i], ffn_conv_w[i], ffn_conv_b[i],
                   ffn_w_down[i], ple_w[i], ple_gate_w[i], final_norm_w)
    return h.reshape(B, S, D)
```

```python
import functools

import numpy as np
import jax
import jax.numpy as jnp
from jax import lax
from jax.experimental import pallas as pl
from jax.experimental.pallas import tpu as pltpu

F32 = jnp.float32
BF16 = jnp.bfloat16

LANES = 128
BF16_ROWS = 16
HEAD_DIM = 64
HEADS_PER_VREG = LANES // HEAD_DIM
RET_HEADS = 8
NSA_Q_HEADS = 8
NSA_KV_HEADS = 2
NSA_GROUP = NSA_Q_HEADS // NSA_KV_HEADS
RET_WIDTH = RET_HEADS * HEAD_DIM
NSA_WIDTH = NSA_Q_HEADS * HEAD_DIM
KV_WIDTH = NSA_KV_HEADS * HEAD_DIM
RET_CHUNK = 128
RET_THETA = 10000.0
ROPE_THETA = 500000.0
ROPE_DIM = HEAD_DIM // 4
CMP_BLOCK = 32
CMP_STRIDE = 16
SEL_BLOCK = 64
SEL_TOPK = 16
WINDOW = 512
N_BRANCH = 3
GATE_ROWS = 16
RANK_STEP = 16
SEL_UNROLL = 4
EPS = 1e-6
QK_SCALE = HEAD_DIM ** -0.5
LOG2_E = 1.4426950408889634
MASKED = -1e30
LOWEST = -3e38

_OFF = np.cumsum([0, RET_WIDTH, RET_WIDTH, RET_WIDTH, RET_WIDTH, NSA_WIDTH,
                  KV_WIDTH, KV_WIDTH, KV_WIDTH, KV_WIDTH, KV_WIDTH, KV_WIDTH])
(_RQ, _RK, _RV, _RG, _NQ, _KC, _VC, _KSL, _VSL, _KWN, _VWN, _NG) = (int(v) for v in _OFF)

VMEM_LIMIT = 56 * 1024 * 1024


def _dot(a, b):
    return jnp.dot(a, b, preferred_element_type=F32)


def _dot_nt(a, b):
    return lax.dot_general(a, b, (((1,), (1,)), ((), ())), preferred_element_type=F32)


def _dot_tn(a, b):
    return lax.dot_general(a, b, (((0,), (0,)), ((), ())), preferred_element_type=F32)


def _params(*semantics):
    return pltpu.CompilerParams(dimension_semantics=semantics, vmem_limit_bytes=VMEM_LIMIT)


def _row_tile(n, pref):
    t = pref
    while n % t:
        t //= 2
    return t


def _rope(y, cos, sin_signed, first_half, half):
    nxt = pltpu.roll(y, LANES - half, 1)
    prv = pltpu.roll(y, half, 1)
    return y * cos + jnp.where(first_half, nxt, prv) * sin_signed


def _in_proj_kernel(x_ref, posrow_ref, nw_ref, w_ref, wt_ref, tab_ref, freq_ref,
                    ret_ref, rg_ref, qt_ref, cmp_ref, kdup_ref, vt_ref, gt_ref, stage_ref):
    x = x_ref[...]
    tm = x.shape[0]
    ms = jnp.mean(x * x, axis=-1, keepdims=True)
    xn = (x * lax.rsqrt(ms + EPS) * nw_ref[...]).astype(BF16)

    tab = tab_ref[...]
    lane = lax.broadcasted_iota(jnp.int32, (tm, LANES), 1)
    in_head = lane % HEAD_DIM
    low = lane < HEAD_DIM
    half_r, half_n = HEAD_DIM // 2, ROPE_DIM // 2
    ang_t = freq_ref[...] * posrow_ref[0].astype(F32)
    cos_t, sin_t = jnp.cos(ang_t), jnp.sin(ang_t)
    pad = jnp.zeros((LANES - half_r - half_n, tm), F32)
    cos_a = jnp.concatenate([cos_t, pad], axis=0).T
    sin_a = jnp.concatenate([sin_t, pad], axis=0).T
    cos_q, sin_q = cos_t[half_r:, :], sin_t[half_r:, :]
    first_r = in_head < half_r
    first_n = in_head < half_n
    rot_n = in_head < ROPE_DIM

    def spread_r(c):
        c = jnp.where(lane < half_r, c, pltpu.roll(c, half_r, 1))
        return jnp.where(low, c, pltpu.roll(c, HEAD_DIM, 1))

    def spread_n(c, rest):
        c = pltpu.roll(c, LANES - half_r, 1)
        c = jnp.where(lane < half_n, c, pltpu.roll(c, half_n, 1))
        c = jnp.where(low, c, pltpu.roll(c, HEAD_DIM, 1))
        return jnp.where(rot_n, c, rest)

    cos_r, sin_r = spread_r(cos_a), spread_r(sin_a) * tab[0:1, :]
    cos_n, sin_n = spread_n(cos_a, 1.0), spread_n(sin_a, 0.0) * tab[1:2, :]

    rope_r = lambda y: _rope(y, cos_r, sin_r, first_r, HEAD_DIM // 2)
    rope_n = lambda y: _rope(y, cos_n, sin_n, first_n, ROPE_DIM // 2)

    def halves(ci):
        y = _dot(xn, w_ref[:, ci * 256:(ci + 1) * 256])
        return y[:, :LANES], y[:, LANES:]

    for ci in range(2):
        for h, y in enumerate(halves(4 + ci)):
            c0 = 2 * RET_WIDTH + ci * 256 + h * LANES
            ret_ref[:, c0:c0 + LANES] = y.astype(BF16)
    t = _dot_nt(wt_ref[...], xn)
    for hd in range(NSA_Q_HEADS):
        b0 = hd * HEAD_DIM
        x1, x2 = t[b0:b0 + half_n, :], t[b0 + half_n:b0 + ROPE_DIM, :]
        head = jnp.concatenate([x1 * cos_q - x2 * sin_q, x1 * sin_q + x2 * cos_q,
                                t[b0 + ROPE_DIM:b0 + HEAD_DIM, :]], axis=0)
        qt_ref[b0:b0 + HEAD_DIM, :] = (head * (QK_SCALE * LOG2_E)).astype(BF16)
    nq, nv = qt_ref.shape[0], vt_ref.shape[0]
    vt_ref[...] = t[nq:nq + nv, :].astype(BF16)
    gt_ref[...] = jax.nn.sigmoid(t[nq + nv:, :])
    for ci in range(2):
        for h, y in enumerate(halves(ci)):
            c0 = ci * 256 + h * LANES
            ret_ref[:, c0:c0 + LANES] = rope_r(y).astype(BF16)
    for ci in range(2):
        for h, y in enumerate(halves(2 + ci)):
            c0 = RET_WIDTH + ci * 256 + h * LANES
            ret_ref[:, c0:c0 + LANES] = (rope_r(y) * QK_SCALE).astype(BF16)
    kc, vc = halves(8)
    for a, y in enumerate((rope_n(kc), vc)):
        stage_ref[2 * a] = y
        stage_ref[2 * a + 1] = pltpu.roll(y, HEAD_DIM, 1)
    n_grp = tm // CMP_STRIDE
    low_g = lax.broadcasted_iota(jnp.int32, (n_grp, LANES), 1) < HEAD_DIM
    for a in range(2):
        for j in range(CMP_STRIDE // 2):
            token = lambda l, s: stage_ref[2 * a + s, pl.ds(l, n_grp, stride=CMP_STRIDE), :]
            even, odd = 2 * j, 2 * j + 1
            cmp_ref[a, 0, :, j * LANES:(j + 1) * LANES] = jnp.where(
                low_g, token(even, 0), token(odd, 1)).astype(BF16)
            cmp_ref[a, 1, :, j * LANES:(j + 1) * LANES] = jnp.where(
                low_g, token(even, 1), token(odd, 0)).astype(BF16)
    for n, y in enumerate(halves(9)):
        y = rope_n(y)
        sw = pltpu.roll(y, HEAD_DIM, 1)
        kdup_ref[:, (2 * n) * LANES:(2 * n + 1) * LANES] = jnp.where(low, y, sw).astype(BF16)
        kdup_ref[:, (2 * n + 1) * LANES:(2 * n + 2) * LANES] = jnp.where(low, sw, y).astype(BF16)
    for ci in range(2):
        for h, y in enumerate(halves(6 + ci)):
            c0 = ci * 256 + h * LANES
            rg_ref[:, c0:c0 + LANES] = y


def _in_proj(x2, positions, norm_w, w_main, w_t, tab, freq, tm):
    T, D = x2.shape
    nv = 2 * KV_WIDTH
    ng = w_t.shape[0] - nv - NSA_WIDTH
    grp_w = CMP_STRIDE * HEAD_DIM
    pos_rows = positions.reshape(T // tm, 1, tm)
    row = lambda w: pl.BlockSpec((tm, w), lambda i: (i, 0))
    col = lambda h: pl.BlockSpec((h, tm), lambda i: (0, i))
    full = lambda a: pl.BlockSpec(a.shape, lambda i: (0,) * a.ndim)
    return pl.pallas_call(
        _in_proj_kernel,
        grid=(T // tm,),
        in_specs=[row(D), pl.BlockSpec((1, 1, tm), lambda i: (i, 0, 0)),
                  full(norm_w), full(w_main), full(w_t), full(tab), full(freq)],
        out_specs=[row(3 * RET_WIDTH), row(RET_WIDTH), col(NSA_WIDTH),
                   pl.BlockSpec((2, NSA_KV_HEADS, tm // CMP_STRIDE, grp_w), lambda i: (0, 0, i, 0)),
                   row(4 * LANES), col(nv), col(ng)],
        out_shape=[jax.ShapeDtypeStruct((T, 3 * RET_WIDTH), BF16),
                   jax.ShapeDtypeStruct((T, RET_WIDTH), F32),
                   jax.ShapeDtypeStruct((NSA_WIDTH, T), BF16),
                   jax.ShapeDtypeStruct((2, NSA_KV_HEADS, T // CMP_STRIDE, grp_w), BF16),
                   jax.ShapeDtypeStruct((T, 4 * LANES), BF16),
                   jax.ShapeDtypeStruct((nv, T), BF16),
                   jax.ShapeDtypeStruct((ng, T), F32)],
        scratch_shapes=[pltpu.VMEM((4, tm, LANES), F32)],
        compiler_params=_params("parallel"),
    )(x2, pos_rows, norm_w, w_main, w_t, tab, freq)


def _in_proj_weights(w_in):
    w_main = jnp.concatenate([w_in[:, :_NQ], w_in[:, _KC:_KSL], w_in[:, _KSL:_VSL],
                              w_in[:, _KWN:_VWN]], axis=1)
    gate_cols = np.full((NSA_KV_HEADS * GATE_ROWS,), -1)
    for g in range(NSA_KV_HEADS):
        for r in range(NSA_GROUP):
            for br in range(N_BRANCH):
                gate_cols[g * GATE_ROWS + r * N_BRANCH + br] = _NG + (g * NSA_GROUP + r) * N_BRANCH + br
    w_gate = jnp.where(gate_cols[None, :] >= 0, w_in[:, np.maximum(gate_cols, 0)], 0.0)
    w_t = jnp.concatenate([w_in[:, _NQ:_KC], w_in[:, _VSL:_KWN], w_in[:, _VWN:_NG], w_gate],
                          axis=1).T
    return w_main.astype(BF16), w_t.astype(BF16)


def _retention_kernel(q_ref, k_ref, v_ref, rg_ref, gnw_ref, dec_ref, qd_ref, kd_ref, cdm_ref,
                      o_ref, state_ref, *, n_chunks, unroll):
    C = RET_CHUNK
    lane = lax.broadcasted_iota(jnp.int32, (C, LANES), 1)
    low = lane < HEAD_DIM
    r_i = lax.broadcasted_iota(jnp.int32, (LANES, LANES), 0)
    c_i = lax.broadcasted_iota(jnp.int32, (LANES, LANES), 1)
    same_head = (r_i < HEAD_DIM) == (c_i < HEAD_DIM)
    state_ref[...] = jnp.zeros_like(state_ref)

    def group(gi, carry):
        rows = [pl.multiple_of((gi * unroll + u) * C, C) for u in range(unroll)]
        q = [q_ref[pl.ds(r0, C), :] for r0 in rows]
        k = [k_ref[pl.ds(r0, C), :] for r0 in rows]
        v = [v_ref[pl.ds(r0, C), :] for r0 in rows]
        zero = jnp.zeros_like(q[0])
        s0 = [_dot_nt(jnp.where(low, q[u], zero), k[u]) for u in range(unroll)]
        s1 = [_dot_nt(jnp.where(low, zero, q[u]), k[u]) for u in range(unroll)]
        kv = [_dot_tn((k[u].astype(F32) * kd_ref[...]).astype(BF16), v[u]) for u in range(unroll)]

        p0 = [(s0[u] * dec_ref[0]).astype(BF16) for u in range(unroll)]
        p1 = [(s1[u] * dec_ref[1]).astype(BF16) for u in range(unroll)]
        states = [state_ref[...]]
        for u in range(unroll):
            states.append(states[u] * cdm_ref[0] + jnp.where(same_head, kv[u], 0.0))
        state_ref[...] = states[unroll]

        o0 = [_dot(p0[u], v[u]) for u in range(unroll)]
        o1 = [_dot(p1[u], v[u]) for u in range(unroll)]
        oc = [_dot((q[u].astype(F32) * qd_ref[...]).astype(BF16), states[u].astype(BF16))
              for u in range(unroll)]

        inv_d = 1.0 / HEAD_DIM
        for u in range(unroll):
            ro = jnp.where(low, o0[u], o1[u]) + oc[u]
            sum0 = jnp.sum(jnp.where(low, ro, 0.0), axis=-1, keepdims=True)
            sum1 = jnp.sum(jnp.where(low, 0.0, ro), axis=-1, keepdims=True)
            d = ro - jnp.where(low, sum0, sum1) * inv_d
            dd = d * d
            var0 = jnp.sum(jnp.where(low, dd, 0.0), axis=-1, keepdims=True)
            var1 = jnp.sum(jnp.where(low, 0.0, dd), axis=-1, keepdims=True)
            var = jnp.where(low, var0, var1) * inv_d
            y = d * lax.rsqrt(var + EPS) * gnw_ref[...]
            g = rg_ref[pl.ds(rows[u], C), :]
            o_ref[pl.ds(rows[u], C), :] = (y * (g * jax.nn.sigmoid(g))).astype(BF16)
        return carry

    lax.fori_loop(0, n_chunks // unroll, group, 0)


def _retention(ret_qkv, rg, gn_w, dec, qd, kd, cdm, B, S):
    T = B * S
    n_pairs = RET_HEADS // HEADS_PER_VREG
    seq = lambda off: pl.BlockSpec((S, LANES), lambda b, hp: (b, off + hp))
    return pl.pallas_call(
        functools.partial(_retention_kernel, n_chunks=S // RET_CHUNK,
                          unroll=_row_tile(S // RET_CHUNK, 8)),
        grid=(B, n_pairs),
        in_specs=[seq(0), seq(n_pairs), seq(2 * n_pairs), seq(0),
                  pl.BlockSpec((1, LANES), lambda b, hp: (0, hp)),
                  pl.BlockSpec((HEADS_PER_VREG, RET_CHUNK, RET_CHUNK), lambda b, hp: (hp, 0, 0)),
                  pl.BlockSpec((RET_CHUNK, LANES), lambda b, hp: (0, hp)),
                  pl.BlockSpec((RET_CHUNK, LANES), lambda b, hp: (0, hp)),
                  pl.BlockSpec((1, LANES, LANES), lambda b, hp: (hp, 0, 0))],
        out_specs=seq(0),
        out_shape=jax.ShapeDtypeStruct((T, RET_WIDTH), BF16),
        scratch_shapes=[pltpu.VMEM((LANES, LANES), F32)],
        compiler_params=_params("parallel", "parallel"),
    )(ret_qkv, ret_qkv, ret_qkv, rg, gn_w, dec, qd, kd, cdm)


def _retention_constants():
    H, C = RET_HEADS, RET_CHUNK
    log_g = jnp.log1p(-jnp.exp2(-5.0 - jnp.arange(H, dtype=F32)))
    pos = jnp.arange(C, dtype=F32)
    diff = pos[:, None] - pos[None, :]
    dec = jnp.where(diff >= 0, jnp.exp(jnp.maximum(diff, 0.0) * log_g[:, None, None]), 0.0)
    q_decay = jnp.exp((pos + 1.0) * log_g[:, None]).T
    k_decay = jnp.exp((C - 1.0 - pos) * log_g[:, None]).T
    chunk_decay = jnp.exp(C * log_g)
    qd = jnp.repeat(q_decay, HEAD_DIM, axis=1)
    kd = jnp.repeat(k_decay, HEAD_DIM, axis=1)
    cd_rows = jnp.repeat(chunk_decay, HEAD_DIM).reshape(H // HEADS_PER_VREG, LANES, 1)
    head_of = jnp.arange(LANES) // HEAD_DIM
    same = (head_of[:, None] == head_of[None, :]).astype(F32)
    return dec, qd, kd, cd_rows * same[None]


def _compress_kernel(tk_ref, tv_ref, pos_ref, w1_ref, w2k_ref, w2vt_ref, ck_ref, cvt_ref):
    pos = jnp.broadcast_to(pos_ref[...], (8, pos_ref.shape[1])).astype(BF16)

    def hidden(t, w1_ref_a):
        half = t.shape[1]
        bias = _dot(pos, w1_ref_a[...])[0:1, :]
        first = _dot(t, w1_ref_a[:half, :])
        second = _dot(t, w1_ref_a[half:, :])
        n = t.shape[0]
        h = first + pltpu.roll(second, n - 1, 0) + bias
        return jax.nn.gelu(h).astype(BF16)

    ck_ref[0, 0] = _dot(hidden(tk_ref[0, 0], w1_ref.at[0]), w2k_ref[...]).astype(BF16)
    cvt_ref[0, 0] = _dot_nt(w2vt_ref[...], hidden(tv_ref[0, 0], w1_ref.at[1])).astype(BF16)


def _compress(cmp_tok, pos_flat, w1, w2k, w2vt, B):
    _, G, rows, width = cmp_tok.shape
    n = rows // B
    tok = lambda a: pl.BlockSpec((1, 1, n, width), lambda b, g: (a, g, b, 0))
    full = lambda a: pl.BlockSpec(a.shape, lambda b, g: (0,) * a.ndim)
    return pl.pallas_call(
        _compress_kernel,
        grid=(B, G),
        in_specs=[tok(0), tok(1), full(pos_flat), full(w1), full(w2k), full(w2vt)],
        out_specs=[pl.BlockSpec((1, 1, n, LANES), lambda b, g: (b, g, 0, 0)),
                   pl.BlockSpec((1, 1, HEAD_DIM, n), lambda b, g: (b, g, 0, 0))],
        out_shape=[jax.ShapeDtypeStruct((B, G, n, LANES), BF16),
                   jax.ShapeDtypeStruct((B, G, HEAD_DIM, n), BF16)],
        compiler_params=_params("parallel", "parallel"),
    )(cmp_tok, cmp_tok, pos_flat, w1, w2k, w2vt)


def _nsa_kernel(q_ref, ck_ref, cvt_ref, ks_ref, kw_ref, vst_ref, vwt_ref, gt_ref, ovt_ref, eb_ref,
                o_ref, m_ref, acc_ref, mw_ref, accw_ref, out_ref, score_ref, ahead_ref,
                sa_ref, sb_ref, sw_ref,
                *, tq, top):
    R = NSA_GROUP
    D = HEAD_DIM
    t0 = pl.multiple_of(pl.program_id(2) * tq, tq)
    t_off = lax.broadcasted_iota(jnp.int32, (1, tq), 1)
    tcol = t0 + t_off
    k_off = lax.broadcasted_iota(jnp.int32, (tq, 1), 0)
    low = lax.broadcasted_iota(jnp.int32, (tq, LANES), 1) < D
    ones_rows = jnp.ones((BF16_ROWS, tq), BF16)
    gate = lambda r, br: gt_ref[r * N_BRANCH + br:r * N_BRANCH + br + 1, :]

    def q_aug(r, extra):
        return jnp.concatenate([q_ref[r * D:(r + 1) * D, :], extra], axis=0)

    zero_rows = jnp.zeros((LANES - D, tq), BF16)

    def value_rows(vt_ref_, start):
        return jnp.concatenate([vt_ref_[:, pl.ds(start, tq)], ones_rows], axis=0)

    sel_state, win_state = (m_ref, acc_ref), (mw_ref, accw_ref)

    def reset(state):
        for r in range(R):
            state[0][r] = jnp.full((1, tq), LOWEST, F32)
            state[1][r] = jnp.zeros(acc_ref.shape[1:], F32)

    def more(state, r, s_t, vrows):
        m_old = state[0][r]
        m_new = jnp.maximum(m_old, jnp.max(s_t, axis=0, keepdims=True))
        state[0][r] = m_new
        state[1][r] = (jnp.exp2(m_old - m_new) * state[1][r]
                       + _dot(vrows, jnp.exp2(s_t - m_new).astype(BF16)))

    def mix(state, r, br):
        o = state[1][r, :D, :] * ((1.0 / state[1][r, D:D + 1, :]) * gate(r, br))
        out_ref[r * D:(r + 1) * D, :] = out_ref[r * D:(r + 1) * D, :] + o

    ck = ck_ref[0, 0]
    ncp = ck.shape[0]
    cvt = jnp.concatenate([cvt_ref[0, 0], jnp.ones((BF16_ROWS, ncp), BF16)], axis=0)
    cmp_end = lax.broadcasted_iota(jnp.int32, (ncp, 1), 0) * CMP_STRIDE + (CMP_BLOCK - 1)
    cmask = cmp_end <= tcol
    sees_any = jnp.where(tcol >= CMP_BLOCK - 1, 1.0, 0.0)
    n_sel = ovt_ref.shape[0]
    imp_t = jnp.zeros((n_sel, tq), F32)
    scores = [_dot(ck, q_aug(r, zero_rows)) for r in range(R)]

    n_back = WINDOW // tq
    win_starts = [pl.multiple_of(jnp.maximum(t0 - j * tq, 0), tq) for j in range(n_back + 1)]
    flag_lane = lax.broadcasted_iota(jnp.int32, (tq, LANES), 1) == D
    flag_row = lax.broadcasted_iota(jnp.int32, (LANES - D, tq), 0) == 0
    qw = [q_aug(r, jnp.where(flag_row, 1.0, 0.0).astype(BF16)) for r in range(R)]

    def issue_window(j):
        off = jnp.full((tq, LANES), jnp.where(t0 >= j * tq, 0.0, MASKED), F32).astype(BF16)
        kw = jnp.where(flag_lane, off, kw_ref[0, pl.ds(win_starts[j], tq), :])
        for r in range(R):
            sw_ref[j, r] = _dot(kw, qw[r])

    for r in range(R):
        for j in range(r, n_back + 1, R):
            issue_window(j)
        s = jnp.where(cmask, scores[r], MASKED)
        e = jnp.exp2(s - jnp.max(s, axis=0, keepdims=True)).astype(BF16)
        acc = _dot(cvt, e)
        inv_l = sees_any * (1.0 / acc[D:D + 1, :])
        out_ref[r * D:(r + 1) * D, :] = acc[:D, :] * (inv_l * gate(r, 0))
        imp_t = imp_t + _dot(ovt_ref[...], e) * inv_l

    blk_i = lax.broadcasted_iota(jnp.int32, (n_sel, tq), 0)
    cur = tcol // SEL_BLOCK
    valid = blk_i * SEL_BLOCK <= tcol
    forced = (blk_i == 0) | (blk_i == cur) | (blk_i == cur - 1)
    score_ref[...] = jnp.where(forced, jnp.inf, jnp.where(valid, imp_t, -jnp.inf))
    ahead_ref[...] = jnp.zeros(ahead_ref.shape, F32)
    live_blocks = (t0 + tq) // SEL_BLOCK

    def count(jg, kb_lo, kb_hi):
        mine = score_ref[jg * 8:(jg + 1) * 8, :]
        blk_g = jg * 8 + lax.broadcasted_iota(jnp.int32, (8, tq), 0)
        ahead = ahead_ref[jg * 8:(jg + 1) * 8, :]
        for kb in range(kb_lo, kb_hi):
            other = score_ref[kb:kb + 1, :]
            if kb < jg * 8:
                ahead = ahead + jnp.where(other >= mine, 1.0, 0.0)
            elif kb >= (jg + 1) * 8:
                ahead = ahead + jnp.where(other > mine, 1.0, 0.0)
            else:
                ahead = ahead + jnp.where(blk_g > kb, jnp.where(other >= mine, 1.0, 0.0),
                                          jnp.where(other > mine, 1.0, 0.0))
        ahead_ref[jg * 8:(jg + 1) * 8, :] = ahead

    def rank_level(lo, hi):
        for jg in range(lo // 8):
            count(jg, lo, hi)
        for jg in range(lo // 8, hi // 8):
            count(jg, 0, hi)

    step = RANK_STEP if n_sel % RANK_STEP == 0 else n_sel
    rank_level(0, step)
    for lo in range(step, n_sel, step):
        @pl.when(live_blocks > lo)
        def _():
            rank_level(lo, lo + step)

    pad_rows = LANES - D - n_sel
    not_sel = jnp.where(ahead_ref[...] < top, 0.0, 1.0)
    if pad_rows:
        not_sel = jnp.concatenate([not_sel, jnp.zeros((pad_rows, tq), F32)], axis=0)
    not_sel = not_sel.astype(BF16)
    qa = [q_aug(r, not_sel) for r in range(R)]

    def keys_aug(start):
        return jnp.where(low, ks_ref[0, pl.ds(start, tq), :], eb_ref[pl.ds(start, tq), :])

    def issue(c, buf_ref):
        ka = keys_aug(pl.multiple_of(c * tq, tq))
        for r in range(R):
            buf_ref[r] = _dot(ka, qa[r])

    def consume(c, buf_ref, own):
        vr = value_rows(vst_ref, pl.multiple_of(c * tq, tq))
        for r in range(R):
            more(sel_state, r, jnp.where(causal, buf_ref[r], MASKED) if own else buf_ref[r], vr)

    causal = k_off <= t_off
    n_plain = pl.program_id(2)
    reset(sel_state)
    issue(0, sa_ref)
    bufs = (sa_ref, sb_ref)

    def sel_trip(k, carry):
        for u in range(SEL_UNROLL):
            c = SEL_UNROLL * k + u
            issue(c + 1, bufs[(u + 1) % 2])
            consume(c, bufs[u % 2], False)
        return carry

    lax.fori_loop(0, n_plain // SEL_UNROLL, sel_trip, 0)

    def window_and_output():
        reset(win_state)
        for j in range(n_back + 1):
            vr = value_rows(vwt_ref, win_starts[j])
            for r in range(R):
                s_t = sw_ref[j, r]
                if j == 0:
                    s_t = jnp.where(causal, s_t, MASKED)
                elif j == n_back:
                    s_t = jnp.where(k_off > t_off, s_t, MASKED)
                more(win_state, r, s_t, vr)
        for r in range(R):
            mix(sel_state, r, 1)
            mix(win_state, r, 2)
        o_ref[...] = out_ref[...].T.astype(BF16)

    for rem in range(SEL_UNROLL):
        @pl.when(n_plain % SEL_UNROLL == rem)
        def _():
            for u in range(rem):
                c = n_plain - rem + u
                issue(c + 1, bufs[(u + 1) % 2])
                consume(c, bufs[u % 2], False)
            consume(n_plain, bufs[rem % 2], True)
            window_and_output()


def _nsa(nq, ck, cvt, kdup, vt, gt, ovt, eb, B, S, tq):
    T = B * S
    G = NSA_KV_HEADS
    assert S % tq == 0 and WINDOW % tq == 0 and S // SEL_BLOCK <= LANES - HEAD_DIM
    nqt = S // tq
    ncp = ck.shape[2]
    kdup3 = kdup.reshape(B, S, kdup.shape[1])
    acc_rows = HEAD_DIM + BF16_ROWS
    keys = lambda off: pl.BlockSpec((1, S, LANES), lambda b, g, i: (b, 0, off + g))
    vals = lambda off: pl.BlockSpec((HEAD_DIM, S), lambda b, g, i: (off + g, b))
    full = lambda a: pl.BlockSpec(a.shape, lambda b, g, i: (0,) * a.ndim)
    return pl.pallas_call(
        functools.partial(_nsa_kernel, tq=tq, top=min(SEL_TOPK, S // SEL_BLOCK)),
        grid=(B, G, nqt),
        in_specs=[pl.BlockSpec((NSA_GROUP * HEAD_DIM, tq), lambda b, g, i: (g, b * nqt + i)),
                  pl.BlockSpec((1, 1, ncp, LANES), lambda b, g, i: (b, g, 0, 0)),
                  pl.BlockSpec((1, 1, HEAD_DIM, ncp), lambda b, g, i: (b, g, 0, 0)),
                  keys(0), keys(G), vals(0), vals(G),
                  pl.BlockSpec((GATE_ROWS, tq), lambda b, g, i: (g, b * nqt + i)),
                  full(ovt), full(eb)],
        out_specs=pl.BlockSpec((tq, NSA_GROUP * HEAD_DIM), lambda b, g, i: (b * nqt + i, g)),
        out_shape=jax.ShapeDtypeStruct((T, NSA_WIDTH), BF16),
        scratch_shapes=[pltpu.VMEM((NSA_GROUP, 1, tq), F32),
                        pltpu.VMEM((NSA_GROUP, acc_rows, tq), F32),
                        pltpu.VMEM((NSA_GROUP, 1, tq), F32),
                        pltpu.VMEM((NSA_GROUP, acc_rows, tq), F32),
                        pltpu.VMEM((NSA_GROUP * HEAD_DIM, tq), F32),
                        pltpu.VMEM((S // SEL_BLOCK, tq), F32),
                        pltpu.VMEM((S // SEL_BLOCK, tq), F32),
                        pltpu.VMEM((NSA_GROUP, tq, tq), F32),
                        pltpu.VMEM((NSA_GROUP, tq, tq), F32),
                        pltpu.VMEM((WINDOW // tq + 1, NSA_GROUP, tq, tq), F32)],
        compiler_params=_params("parallel", "parallel", "parallel"),
    )(nq, ck, cvt, kdup3, kdup3, vt, vt, gt, ovt, eb)


def _nsa_constants(S):
    n_cmp = (S - CMP_BLOCK) // CMP_STRIDE + 1
    ncp = S // CMP_STRIDE
    n_sel = S // SEL_BLOCK
    ci = np.arange(ncp)[None, :]
    sj = np.arange(n_sel)[:, None]
    ovt = ((ci * CMP_STRIDE < (sj + 1) * SEL_BLOCK)
           & (ci * CMP_STRIDE + CMP_BLOCK > sj * SEL_BLOCK) & (ci < n_cmp))
    eb = np.zeros((S, LANES), np.float32)
    eb[np.arange(S), HEAD_DIM + np.arange(S) // SEL_BLOCK] = MASKED
    return jnp.asarray(ovt, BF16), jnp.asarray(eb, BF16)


def _mlp_kernel(ret_ref, nsa_ref, x_ref, p_ref, wo_ref, nw_ref, wg_ref, wv_ref, cw_ref, cb_ref,
                wd_ref, pg_ref, pw_ref, fw_ref, o_ref, h_ref, act_ref, tail_ref, *, fc, final):
    @pl.when(pl.program_id(1) == 0)
    def _():
        tail_ref[...] = jnp.zeros_like(tail_ref)

    mix = _dot(ret_ref[...], wo_ref[:RET_WIDTH, :]) + _dot(nsa_ref[...], wo_ref[RET_WIDTH:, :])
    h = x_ref[...] + mix
    h_ref[...] = h
    ms = jnp.mean(h * h, axis=-1, keepdims=True)
    hn = (h * lax.rsqrt(ms + EPS) * nw_ref[...]).astype(BF16)

    tm = hn.shape[0]
    row = lax.broadcasted_iota(jnp.int32, (tm, fc), 0)
    for c in range(wg_ref.shape[1] // fc):
        sl = slice(c * fc, (c + 1) * fc)
        g = _dot(hn, wg_ref[:, sl])
        val = _dot(hn, wv_ref[:, sl])
        tail = tail_ref[:, sl]
        g1 = jnp.where(row == 0, tail[7:8, :], pltpu.roll(g, 1, 0))
        g2 = jnp.where(row == 0, tail[6:7, :],
                       jnp.where(row == 1, tail[7:8, :], pltpu.roll(g, 2, 0)))
        tail_ref[:, sl] = g[tm - 8:, :]
        cw = cw_ref[:, sl]
        y = cb_ref[:, sl] + cw[0:1, :] * g2 + cw[1:2, :] * g1 + cw[2:3, :] * g
        act_ref[:, sl] = (y * jax.nn.sigmoid(y) * val).astype(BF16)

    h = h_ref[...] + _dot(act_ref[...], wd_ref[...])
    gate = jax.nn.sigmoid(_dot(h.astype(BF16), pg_ref[...]))
    h = h + gate * _dot(p_ref[...].astype(BF16), pw_ref[...])
    if final:
        ms = jnp.mean(h * h, axis=-1, keepdims=True)
        h = h * lax.rsqrt(ms + EPS) * fw_ref[...]
    o_ref[...] = h


def _mlp(ret_out, nsa_out, x2, p2, w_out, norm_w, wg, wv, conv_w, conv_b, wd, pg, pw, fw,
         B, S, tm, final):
    T, D = x2.shape
    F = wg.shape[1]
    nt = S // tm
    row = lambda w: pl.BlockSpec((tm, w), lambda b, j: (b * nt + j, 0))
    full = lambda a: pl.BlockSpec(a.shape, lambda b, j: (0,) * a.ndim, pipeline_mode=pl.Buffered(1))
    weights = (w_out, norm_w, wg, wv, conv_w, conv_b, wd, pg, pw, fw)
    return pl.pallas_call(
        functools.partial(_mlp_kernel, fc=256, final=final),
        grid=(B, nt),
        in_specs=[row(RET_WIDTH), row(NSA_WIDTH), row(D), row(p2.shape[1])]
                 + [full(w) for w in weights],
        out_specs=row(D),
        out_shape=jax.ShapeDtypeStruct((T, D), F32),
        scratch_shapes=[pltpu.VMEM((tm, D), F32), pltpu.VMEM((tm, F), BF16),
                        pltpu.VMEM((8, F), F32)],
        compiler_params=_params("arbitrary", "arbitrary"),
    )(ret_out, nsa_out, x2, p2, *weights)


def _rope_tables(tm):
    lane = np.arange(LANES) % HEAD_DIM
    half_r = HEAD_DIM // 2
    inv_r = jnp.power(jnp.float32(RET_THETA), -jnp.arange(half_r, dtype=F32) / half_r)
    half_n = ROPE_DIM // 2
    inv_n = jnp.power(jnp.float32(ROPE_THETA), -jnp.arange(half_n, dtype=F32) / half_n)
    signs = [jnp.asarray(np.where(lane < half_r, -1.0, 1.0), F32),
             jnp.asarray(np.where(lane < half_n, -1.0, 1.0), F32)]
    table = jnp.concatenate([jnp.stack(signs), jnp.zeros((6, LANES), F32)], axis=0)
    freq = jnp.concatenate([inv_r, inv_n])
    return table, jnp.broadcast_to(freq[:, None], (half_r + half_n, tm))


def _layer(h2, p2, pos2, B, S, final, norm_mix_w, w_in, ret_gn_w, cmp_pos, cmp_k_w1, cmp_k_w2,
           cmp_v_w1, cmp_v_w2, w_out, norm_ffn_w, ffn_w_up, ffn_conv_w, ffn_conv_b, ffn_w_down,
           ple_w, ple_gate_w, final_norm_w):
    T, D = h2.shape
    tm = _row_tile(S, 512)

    w_main, w_t = _in_proj_weights(w_in)
    ret_qkv, rg, qt, cmp_tok, kdup, vt, gt = _in_proj(
        h2, pos2, norm_mix_w.reshape(1, D), w_main, w_t, *_rope_tables(tm), tm)

    dec, qd, kd, cdm = _retention_constants()
    ret_out = _retention(ret_qkv, rg, ret_gn_w.reshape(1, RET_WIDTH), dec, qd, kd, cdm, B, S)

    w1 = jnp.stack([cmp_k_w1, cmp_v_w1]).astype(BF16)
    w2k = jnp.tile(cmp_k_w2, (1, HEADS_PER_VREG)).astype(BF16)
    ck, cvt = _compress(cmp_tok, cmp_pos.reshape(1, CMP_BLOCK * HEAD_DIM), w1, w2k,
                        cmp_v_w2.T.astype(BF16), B)

    ovt, eb = _nsa_constants(S)
    nsa_out = _nsa(qt, ck, cvt, kdup, vt, gt, ovt, eb, B, S, tq=256)

    d_ff = ffn_w_down.shape[0]
    conv_w = jnp.pad(ffn_conv_w, ((0, 8 - ffn_conv_w.shape[0]), (0, 0)))
    return _mlp(ret_out, nsa_out, h2, p2, w_out.astype(BF16), norm_ffn_w.reshape(1, D),
                ffn_w_up[:, :d_ff].astype(BF16), ffn_w_up[:, d_ff:].astype(BF16),
                conv_w, ffn_conv_b.reshape(1, d_ff), ffn_w_down.astype(BF16),
                ple_gate_w.astype(BF16), ple_w.astype(BF16), final_norm_w.reshape(1, D),
                B, S, tm, final)


def kernel(x, p, positions, norm_mix_w, w_in, ret_gn_w, cmp_pos, cmp_k_w1, cmp_k_w2, cmp_v_w1, cmp_v_w2, w_out, norm_ffn_w, ffn_w_up, ffn_conv_w, ffn_conv_b, ffn_w_down, ple_w, ple_gate_w, final_norm_w):
    B, S, D = x.shape
    T = B * S
    depth = w_in.shape[0]
    h = x.reshape(T, D)
    pos2 = positions.reshape(T).astype(jnp.int32)
    for i in range(depth):
        h = _layer(h, p[i].reshape(T, -1), pos2, B, S, i == depth - 1, norm_mix_w[i], w_in[i],
                   ret_gn_w[i], cmp_pos[i], cmp_k_w1[i], cmp_k_w2[i], cmp_v_w1[i], cmp_v_w2[i],
                   w_out[i], norm_ffn_w[i], ffn_w_up[i], ffn_conv_w[i], ffn_conv_b[i],
                   ffn_w_down[i], ple_w[i], ple_gate_w[i], final_norm_w)
    return h.reshape(B, S, D)
```

```python
import functools

import numpy as np
import jax
import jax.numpy as jnp
from jax import lax
from jax.experimental import pallas as pl
from jax.experimental.pallas import tpu as pltpu

F32 = jnp.float32
BF16 = jnp.bfloat16

LANES = 128
BF16_ROWS = 16
HEAD_DIM = 64
HEADS_PER_VREG = LANES // HEAD_DIM
RET_HEADS = 8
NSA_Q_HEADS = 8
NSA_KV_HEADS = 2
NSA_GROUP = NSA_Q_HEADS // NSA_KV_HEADS
RET_WIDTH = RET_HEADS * HEAD_DIM
NSA_WIDTH = NSA_Q_HEADS * HEAD_DIM
KV_WIDTH = NSA_KV_HEADS * HEAD_DIM
RET_CHUNK = 128
RET_THETA = 10000.0
ROPE_THETA = 500000.0
ROPE_DIM = HEAD_DIM // 4
CMP_BLOCK = 32
CMP_STRIDE = 16
SEL_BLOCK = 64
SEL_TOPK = 16
WINDOW = 512
N_BRANCH = 3
GATE_ROWS = 16
RANK_STEP = 16
SEL_UNROLL = 4
EPS = 1e-6
QK_SCALE = HEAD_DIM ** -0.5
LOG2_E = 1.4426950408889634
MASKED = -1e30
LOWEST = -3e38

_OFF = np.cumsum([0, RET_WIDTH, RET_WIDTH, RET_WIDTH, RET_WIDTH, NSA_WIDTH,
                  KV_WIDTH, KV_WIDTH, KV_WIDTH, KV_WIDTH, KV_WIDTH, KV_WIDTH])
(_RQ, _RK, _RV, _RG, _NQ, _KC, _VC, _KSL, _VSL, _KWN, _VWN, _NG) = (int(v) for v in _OFF)

VMEM_LIMIT = 56 * 1024 * 1024


def _dot(a, b):
    return jnp.dot(a, b, preferred_element_type=F32)


def _dot_nt(a, b):
    return lax.dot_general(a, b, (((1,), (1,)), ((), ())), preferred_element_type=F32)


def _dot_tn(a, b):
    return lax.dot_general(a, b, (((0,), (0,)), ((), ())), preferred_element_type=F32)


def _params(*semantics):
    return pltpu.CompilerParams(dimension_semantics=semantics, vmem_limit_bytes=VMEM_LIMIT)


def _row_tile(n, pref):
    t = pref
    while n % t:
        t //= 2
    return t


def _rope(y, cos, sin_signed, first_half, half):
    nxt = pltpu.roll(y, LANES - half, 1)
    prv = pltpu.roll(y, half, 1)
    return y * cos + jnp.where(first_half, nxt, prv) * sin_signed


def _in_proj_kernel(x_ref, posrow_ref, nw_ref, w_ref, wt_ref, tab_ref, freq_ref,
                    ret_ref, rg_ref, qt_ref, cmp_ref, kdup_ref, vt_ref, gt_ref, stage_ref):
    x = x_ref[...]
    tm = x.shape[0]
    ms = jnp.mean(x * x, axis=-1, keepdims=True)
    xn = (x * lax.rsqrt(ms + EPS) * nw_ref[...]).astype(BF16)

    tab = tab_ref[...]
    lane = lax.broadcasted_iota(jnp.int32, (tm, LANES), 1)
    in_head = lane % HEAD_DIM
    low = lane < HEAD_DIM
    half_r, half_n = HEAD_DIM // 2, ROPE_DIM // 2
    ang_t = freq_ref[...] * posrow_ref[0].astype(F32)
    cos_t, sin_t = jnp.cos(ang_t), jnp.sin(ang_t)
    pad = jnp.zeros((LANES - half_r - half_n, tm), F32)
    cos_a = jnp.concatenate([cos_t, pad], axis=0).T
    sin_a = jnp.concatenate([sin_t, pad], axis=0).T
    cos_q, sin_q = cos_t[half_r:, :], sin_t[half_r:, :]
    first_r = in_head < half_r
    first_n = in_head < half_n
    rot_n = in_head < ROPE_DIM

    def spread_r(c):
        c = jnp.where(lane < half_r, c, pltpu.roll(c, half_r, 1))
        return jnp.where(low, c, pltpu.roll(c, HEAD_DIM, 1))

    def spread_n(c, rest):
        c = pltpu.roll(c, LANES - half_r, 1)
        c = jnp.where(lane < half_n, c, pltpu.roll(c, half_n, 1))
        c = jnp.where(low, c, pltpu.roll(c, HEAD_DIM, 1))
        return jnp.where(rot_n, c, rest)

    cos_r, sin_r = spread_r(cos_a), spread_r(sin_a) * tab[0:1, :]
    cos_n, sin_n = spread_n(cos_a, 1.0), spread_n(sin_a, 0.0) * tab[1:2, :]

    rope_r = lambda y: _rope(y, cos_r, sin_r, first_r, HEAD_DIM // 2)
    rope_n = lambda y: _rope(y, cos_n, sin_n, first_n, ROPE_DIM // 2)

    def halves(ci):
        y = _dot(xn, w_ref[:, ci * 256:(ci + 1) * 256])
        return y[:, :LANES], y[:, LANES:]

    for ci in range(2):
        for h, y in enumerate(halves(4 + ci)):
            c0 = 2 * RET_WIDTH + ci * 256 + h * LANES
            ret_ref[:, c0:c0 + LANES] = y.astype(BF16)
    t = _dot_nt(wt_ref[...], xn)
    for hd in range(NSA_Q_HEADS):
        b0 = hd * HEAD_DIM
        x1, x2 = t[b0:b0 + half_n, :], t[b0 + half_n:b0 + ROPE_DIM, :]
        head = jnp.concatenate([x1 * cos_q - x2 * sin_q, x1 * sin_q + x2 * cos_q,
                                t[b0 + ROPE_DIM:b0 + HEAD_DIM, :]], axis=0)
        qt_ref[b0:b0 + HEAD_DIM, :] = (head * (QK_SCALE * LOG2_E)).astype(BF16)
    nq, nv = qt_ref.shape[0], vt_ref.shape[0]
    vt_ref[...] = t[nq:nq + nv, :].astype(BF16)
    gt_ref[...] = jax.nn.sigmoid(t[nq + nv:, :])
    for ci in range(2):
        for h, y in enumerate(halves(ci)):
            c0 = ci * 256 + h * LANES
            ret_ref[:, c0:c0 + LANES] = rope_r(y).astype(BF16)
    for ci in range(2):
        for h, y in enumerate(halves(2 + ci)):
            c0 = RET_WIDTH + ci * 256 + h * LANES
            ret_ref[:, c0:c0 + LANES] = (rope_r(y) * QK_SCALE).astype(BF16)
    kc, vc = halves(8)
    for a, y in enumerate((rope_n(kc), vc)):
        stage_ref[2 * a] = y
        stage_ref[2 * a + 1] = pltpu.roll(y, HEAD_DIM, 1)
    n_grp = tm // CMP_STRIDE
    low_g = lax.broadcasted_iota(jnp.int32, (n_grp, LANES), 1) < HEAD_DIM
    for a in range(2):
        for j in range(CMP_STRIDE // 2):
            token = lambda l, s: stage_ref[2 * a + s, pl.ds(l, n_grp, stride=CMP_STRIDE), :]
            even, odd = 2 * j, 2 * j + 1
            cmp_ref[a, 0, :, j * LANES:(j + 1) * LANES] = jnp.where(
                low_g, token(even, 0), token(odd, 1)).astype(BF16)
            cmp_ref[a, 1, :, j * LANES:(j + 1) * LANES] = jnp.where(
                low_g, token(even, 1), token(odd, 0)).astype(BF16)
    for n, y in enumerate(halves(9)):
        y = rope_n(y)
        sw = pltpu.roll(y, HEAD_DIM, 1)
        kdup_ref[:, (2 * n) * LANES:(2 * n + 1) * LANES] = jnp.where(low, y, sw).astype(BF16)
        kdup_ref[:, (2 * n + 1) * LANES:(2 * n + 2) * LANES] = jnp.where(low, sw, y).astype(BF16)
    for ci in range(2):
        for h, y in enumerate(halves(6 + ci)):
            c0 = ci * 256 + h * LANES
            rg_ref[:, c0:c0 + LANES] = y


def _in_proj(x2, positions, norm_w, w_main, w_t, tab, freq, tm):
    T, D = x2.shape
    nv = 2 * KV_WIDTH
    ng = w_t.shape[0] - nv - NSA_WIDTH
    grp_w = CMP_STRIDE * HEAD_DIM
    pos_rows = positions.reshape(T // tm, 1, tm)
    row = lambda w: pl.BlockSpec((tm, w), lambda i: (i, 0))
    col = lambda h: pl.BlockSpec((h, tm), lambda i: (0, i))
    full = lambda a: pl.BlockSpec(a.shape, lambda i: (0,) * a.ndim)
    return pl.pallas_call(
        _in_proj_kernel,
        grid=(T // tm,),
        in_specs=[row(D), pl.BlockSpec((1, 1, tm), lambda i: (i, 0, 0)),
                  full(norm_w), full(w_main), full(w_t), full(tab), full(freq)],
        out_specs=[row(3 * RET_WIDTH), row(RET_WIDTH), col(NSA_WIDTH),
                   pl.BlockSpec((2, NSA_KV_HEADS, tm // CMP_STRIDE, grp_w), lambda i: (0, 0, i, 0)),
                   row(4 * LANES), col(nv), col(ng)],
        out_shape=[jax.ShapeDtypeStruct((T, 3 * RET_WIDTH), BF16),
                   jax.ShapeDtypeStruct((T, RET_WIDTH), F32),
                   jax.ShapeDtypeStruct((NSA_WIDTH, T), BF16),
                   jax.ShapeDtypeStruct((2, NSA_KV_HEADS, T // CMP_STRIDE, grp_w), BF16),
                   jax.ShapeDtypeStruct((T, 4 * LANES), BF16),
                   jax.ShapeDtypeStruct((nv, T), BF16),
                   jax.ShapeDtypeStruct((ng, T), F32)],
        scratch_shapes=[pltpu.VMEM((4, tm, LANES), F32)],
        compiler_params=_params("parallel"),
    )(x2, pos_rows, norm_w, w_main, w_t, tab, freq)


def _in_proj_weights(w_in):
    w_main = jnp.concatenate([w_in[:, :_NQ], w_in[:, _KC:_KSL], w_in[:, _KSL:_VSL],
                              w_in[:, _KWN:_VWN]], axis=1)
    gate_cols = np.full((NSA_KV_HEADS * GATE_ROWS,), -1)
    for g in range(NSA_KV_HEADS):
        for r in range(NSA_GROUP):
            for br in range(N_BRANCH):
                gate_cols[g * GATE_ROWS + r * N_BRANCH + br] = _NG + (g * NSA_GROUP + r) * N_BRANCH + br
    w_gate = jnp.where(gate_cols[None, :] >= 0, w_in[:, np.maximum(gate_cols, 0)], 0.0)
    w_t = jnp.concatenate([w_in[:, _NQ:_KC], w_in[:, _VSL:_KWN], w_in[:, _VWN:_NG], w_gate],
                          axis=1).T
    return w_main.astype(BF16), w_t.astype(BF16)


def _retention_kernel(q_ref, k_ref, v_ref, rg_ref, gnw_ref, dec_ref, qd_ref, kd_ref, cdm_ref,
                      o_ref, state_ref, *, n_chunks, unroll):
    C = RET_CHUNK
    lane = lax.broadcasted_iota(jnp.int32, (C, LANES), 1)
    low = lane < HEAD_DIM
    r_i = lax.broadcasted_iota(jnp.int32, (LANES, LANES), 0)
    c_i = lax.broadcasted_iota(jnp.int32, (LANES, LANES), 1)
    same_head = (r_i < HEAD_DIM) == (c_i < HEAD_DIM)
    state_ref[...] = jnp.zeros_like(state_ref)

    def group(gi, carry):
        rows = [pl.multiple_of((gi * unroll + u) * C, C) for u in range(unroll)]
        q = [q_ref[pl.ds(r0, C), :] for r0 in rows]
        k = [k_ref[pl.ds(r0, C), :] for r0 in rows]
        v = [v_ref[pl.ds(r0, C), :] for r0 in rows]
        zero = jnp.zeros_like(q[0])
        s0 = [_dot_nt(jnp.where(low, q[u], zero), k[u]) for u in range(unroll)]
        s1 = [_dot_nt(jnp.where(low, zero, q[u]), k[u]) for u in range(unroll)]
        kv = [_dot_tn((k[u].astype(F32) * kd_ref[...]).astype(BF16), v[u]) for u in range(unroll)]

        p0 = [(s0[u] * dec_ref[0]).astype(BF16) for u in range(unroll)]
        p1 = [(s1[u] * dec_ref[1]).astype(BF16) for u in range(unroll)]
        states = [state_ref[...]]
        for u in range(unroll):
            states.append(states[u] * cdm_ref[0] + jnp.where(same_head, kv[u], 0.0))
        state_ref[...] = states[unroll]

        o0 = [_dot(p0[u], v[u]) for u in range(unroll)]
        o1 = [_dot(p1[u], v[u]) for u in range(unroll)]
        oc = [_dot((q[u].astype(F32) * qd_ref[...]).astype(BF16), states[u].astype(BF16))
              for u in range(unroll)]

        inv_d = 1.0 / HEAD_DIM
        for u in range(unroll):
            ro = jnp.where(low, o0[u], o1[u]) + oc[u]
            sum0 = jnp.sum(jnp.where(low, ro, 0.0), axis=-1, keepdims=True)
            sum1 = jnp.sum(jnp.where(low, 0.0, ro), axis=-1, keepdims=True)
            d = ro - jnp.where(low, sum0, sum1) * inv_d
            dd = d * d
            var0 = jnp.sum(jnp.where(low, dd, 0.0), axis=-1, keepdims=True)
            var1 = jnp.sum(jnp.where(low, 0.0, dd), axis=-1, keepdims=True)
            var = jnp.where(low, var0, var1) * inv_d
            y = d * lax.rsqrt(var + EPS) * gnw_ref[...]
            g = rg_ref[pl.ds(rows[u], C), :]
            o_ref[pl.ds(rows[u], C), :] = (y * (g * jax.nn.sigmoid(g))).astype(BF16)
        return carry

    lax.fori_loop(0, n_chunks // unroll, group, 0)


def _retention(ret_qkv, rg, gn_w, dec, qd, kd, cdm, B, S):
    T = B * S
    n_pairs = RET_HEADS // HEADS_PER_VREG
    seq = lambda off: pl.BlockSpec((S, LANES), lambda b, hp: (b, off + hp))
    return pl.pallas_call(
        functools.partial(_retention_kernel, n_chunks=S // RET_CHUNK,
                          unroll=_row_tile(S // RET_CHUNK, 16)),
        grid=(B, n_pairs),
        in_specs=[seq(0), seq(n_pairs), seq(2 * n_pairs), seq(0),
                  pl.BlockSpec((1, LANES), lambda b, hp: (0, hp)),
                  pl.BlockSpec((HEADS_PER_VREG, RET_CHUNK, RET_CHUNK), lambda b, hp: (hp, 0, 0)),
                  pl.BlockSpec((RET_CHUNK, LANES), lambda b, hp: (0, hp)),
                  pl.BlockSpec((RET_CHUNK, LANES), lambda b, hp: (0, hp)),
                  pl.BlockSpec((1, LANES, LANES), lambda b, hp: (hp, 0, 0))],
        out_specs=seq(0),
        out_shape=jax.ShapeDtypeStruct((T, RET_WIDTH), BF16),
        scratch_shapes=[pltpu.VMEM((LANES, LANES), F32)],
        compiler_params=_params("parallel", "parallel"),
    )(ret_qkv, ret_qkv, ret_qkv, rg, gn_w, dec, qd, kd, cdm)


def _retention_constants():
    H, C = RET_HEADS, RET_CHUNK
    log_g = jnp.log1p(-jnp.exp2(-5.0 - jnp.arange(H, dtype=F32)))
    pos = jnp.arange(C, dtype=F32)
    diff = pos[:, None] - pos[None, :]
    dec = jnp.where(diff >= 0, jnp.exp(jnp.maximum(diff, 0.0) * log_g[:, None, None]), 0.0)
    q_decay = jnp.exp((pos + 1.0) * log_g[:, None]).T
    k_decay = jnp.exp((C - 1.0 - pos) * log_g[:, None]).T
    chunk_decay = jnp.exp(C * log_g)
    qd = jnp.repeat(q_decay, HEAD_DIM, axis=1)
    kd = jnp.repeat(k_decay, HEAD_DIM, axis=1)
    cd_rows = jnp.repeat(chunk_decay, HEAD_DIM).reshape(H // HEADS_PER_VREG, LANES, 1)
    head_of = jnp.arange(LANES) // HEAD_DIM
    same = (head_of[:, None] == head_of[None, :]).astype(F32)
    return dec, qd, kd, cd_rows * same[None]


def _compress_kernel(tk_ref, tv_ref, pos_ref, w1_ref, w2k_ref, w2vt_ref, ck_ref, cvt_ref):
    pos = jnp.broadcast_to(pos_ref[...], (8, pos_ref.shape[1])).astype(BF16)

    def hidden(t, w1_ref_a):
        half = t.shape[1]
        bias = _dot(pos, w1_ref_a[...])[0:1, :]
        first = _dot(t, w1_ref_a[:half, :])
        second = _dot(t, w1_ref_a[half:, :])
        n = t.shape[0]
        h = first + pltpu.roll(second, n - 1, 0) + bias
        return jax.nn.gelu(h).astype(BF16)

    ck_ref[0, 0] = _dot(hidden(tk_ref[0, 0], w1_ref.at[0]), w2k_ref[...]).astype(BF16)
    cvt_ref[0, 0] = _dot_nt(w2vt_ref[...], hidden(tv_ref[0, 0], w1_ref.at[1])).astype(BF16)


def _compress(cmp_tok, pos_flat, w1, w2k, w2vt, B):
    _, G, rows, width = cmp_tok.shape
    n = rows // B
    tok = lambda a: pl.BlockSpec((1, 1, n, width), lambda b, g: (a, g, b, 0))
    full = lambda a: pl.BlockSpec(a.shape, lambda b, g: (0,) * a.ndim)
    return pl.pallas_call(
        _compress_kernel,
        grid=(B, G),
        in_specs=[tok(0), tok(1), full(pos_flat), full(w1), full(w2k), full(w2vt)],
        out_specs=[pl.BlockSpec((1, 1, n, LANES), lambda b, g: (b, g, 0, 0)),
                   pl.BlockSpec((1, 1, HEAD_DIM, n), lambda b, g: (b, g, 0, 0))],
        out_shape=[jax.ShapeDtypeStruct((B, G, n, LANES), BF16),
                   jax.ShapeDtypeStruct((B, G, HEAD_DIM, n), BF16)],
        compiler_params=_params("parallel", "parallel"),
    )(cmp_tok, cmp_tok, pos_flat, w1, w2k, w2vt)


def _nsa_kernel(q_ref, ck_ref, cvt_ref, ks_ref, kw_ref, vst_ref, vwt_ref, gt_ref, ovt_ref, eb_ref,
                o_ref, m_ref, acc_ref, mw_ref, accw_ref, out_ref, score_ref, ahead_ref,
                sa_ref, sb_ref, sw_ref,
                *, tq, top):
    R = NSA_GROUP
    D = HEAD_DIM
    t0 = pl.multiple_of(pl.program_id(2) * tq, tq)
    t_off = lax.broadcasted_iota(jnp.int32, (1, tq), 1)
    tcol = t0 + t_off
    k_off = lax.broadcasted_iota(jnp.int32, (tq, 1), 0)
    low = lax.broadcasted_iota(jnp.int32, (tq, LANES), 1) < D
    ones_rows = jnp.ones((BF16_ROWS, tq), BF16)
    gate = lambda r, br: gt_ref[r * N_BRANCH + br:r * N_BRANCH + br + 1, :]

    def q_aug(r, extra):
        return jnp.concatenate([q_ref[r * D:(r + 1) * D, :], extra], axis=0)

    zero_rows = jnp.zeros((LANES - D, tq), BF16)

    def value_rows(vt_ref_, start):
        return jnp.concatenate([vt_ref_[:, pl.ds(start, tq)], ones_rows], axis=0)

    sel_state, win_state = (m_ref, acc_ref), (mw_ref, accw_ref)

    def reset(state):
        for r in range(R):
            state[0][r] = jnp.full((1, tq), LOWEST, F32)
            state[1][r] = jnp.zeros(acc_ref.shape[1:], F32)

    def more(state, r, s_t, vrows):
        m_old = state[0][r]
        m_new = jnp.maximum(m_old, jnp.max(s_t, axis=0, keepdims=True))
        state[0][r] = m_new
        state[1][r] = (jnp.exp2(m_old - m_new) * state[1][r]
                       + _dot(vrows, jnp.exp2(s_t - m_new).astype(BF16)))

    def mix(state, r, br):
        o = state[1][r, :D, :] * ((1.0 / state[1][r, D:D + 1, :]) * gate(r, br))
        out_ref[r * D:(r + 1) * D, :] = out_ref[r * D:(r + 1) * D, :] + o

    ck = ck_ref[0, 0]
    ncp = ck.shape[0]
    cvt = jnp.concatenate([cvt_ref[0, 0], jnp.ones((BF16_ROWS, ncp), BF16)], axis=0)
    cmp_end = lax.broadcasted_iota(jnp.int32, (ncp, 1), 0) * CMP_STRIDE + (CMP_BLOCK - 1)
    cmask = cmp_end <= tcol
    sees_any = jnp.where(tcol >= CMP_BLOCK - 1, 1.0, 0.0)
    n_sel = ovt_ref.shape[0]
    imp_t = jnp.zeros((n_sel, tq), F32)
    scores = [_dot(ck, q_aug(r, zero_rows)) for r in range(R)]

    n_back = WINDOW // tq
    win_starts = [pl.multiple_of(jnp.maximum(t0 - j * tq, 0), tq) for j in range(n_back + 1)]
    flag_lane = lax.broadcasted_iota(jnp.int32, (tq, LANES), 1) == D
    flag_row = lax.broadcasted_iota(jnp.int32, (LANES - D, tq), 0) == 0
    qw = [q_aug(r, jnp.where(flag_row, 1.0, 0.0).astype(BF16)) for r in range(R)]

    def issue_window(j):
        off = jnp.full((tq, LANES), jnp.where(t0 >= j * tq, 0.0, MASKED), F32).astype(BF16)
        kw = jnp.where(flag_lane, off, kw_ref[0, pl.ds(win_starts[j], tq), :])
        for r in range(R):
            sw_ref[j, r] = _dot(kw, qw[r])

    for r in range(R):
        for j in range(r, n_back + 1, R):
            issue_window(j)
        s = jnp.where(cmask, scores[r], MASKED)
        e = jnp.exp2(s - jnp.max(s, axis=0, keepdims=True)).astype(BF16)
        acc = _dot(cvt, e)
        inv_l = sees_any * (1.0 / acc[D:D + 1, :])
        out_ref[r * D:(r + 1) * D, :] = acc[:D, :] * (inv_l * gate(r, 0))
        imp_t = imp_t + _dot(ovt_ref[...], e) * inv_l

    blk_i = lax.broadcasted_iota(jnp.int32, (n_sel, tq), 0)
    cur = tcol // SEL_BLOCK
    valid = blk_i * SEL_BLOCK <= tcol
    forced = (blk_i == 0) | (blk_i == cur) | (blk_i == cur - 1)
    score_ref[...] = jnp.where(forced, jnp.inf, jnp.where(valid, imp_t, -jnp.inf))
    ahead_ref[...] = jnp.zeros(ahead_ref.shape, F32)
    live_blocks = (t0 + tq) // SEL_BLOCK

    def count(jg, kb_lo, kb_hi):
        mine = score_ref[jg * 8:(jg + 1) * 8, :]
        blk_g = jg * 8 + lax.broadcasted_iota(jnp.int32, (8, tq), 0)
        ahead = ahead_ref[jg * 8:(jg + 1) * 8, :]
        for kb in range(kb_lo, kb_hi):
            other = score_ref[kb:kb + 1, :]
            if kb < jg * 8:
                ahead = ahead + jnp.where(other >= mine, 1.0, 0.0)
            elif kb >= (jg + 1) * 8:
                ahead = ahead + jnp.where(other > mine, 1.0, 0.0)
            else:
                ahead = ahead + jnp.where(blk_g > kb, jnp.where(other >= mine, 1.0, 0.0),
                                          jnp.where(other > mine, 1.0, 0.0))
        ahead_ref[jg * 8:(jg + 1) * 8, :] = ahead

    def rank_level(lo, hi):
        for jg in range(lo // 8):
            count(jg, lo, hi)
        for jg in range(lo // 8, hi // 8):
            count(jg, 0, hi)

    step = RANK_STEP if n_sel % RANK_STEP == 0 else n_sel
    rank_level(0, step)
    for lo in range(step, n_sel, step):
        @pl.when(live_blocks > lo)
        def _():
            rank_level(lo, lo + step)

    pad_rows = LANES - D - n_sel
    not_sel = jnp.where(ahead_ref[...] < top, 0.0, 1.0)
    if pad_rows:
        not_sel = jnp.concatenate([not_sel, jnp.zeros((pad_rows, tq), F32)], axis=0)
    not_sel = not_sel.astype(BF16)
    qa = [q_aug(r, not_sel) for r in range(R)]

    def keys_aug(start):
        return jnp.where(low, ks_ref[0, pl.ds(start, tq), :], eb_ref[pl.ds(start, tq), :])

    def issue(c, buf_ref):
        ka = keys_aug(pl.multiple_of(c * tq, tq))
        for r in range(R):
            buf_ref[r] = _dot(ka, qa[r])

    def consume(c, buf_ref, own):
        vr = value_rows(vst_ref, pl.multiple_of(c * tq, tq))
        for r in range(R):
            more(sel_state, r, jnp.where(causal, buf_ref[r], MASKED) if own else buf_ref[r], vr)

    causal = k_off <= t_off
    n_plain = pl.program_id(2)
    reset(sel_state)
    reset(win_state)
    issue(0, sa_ref)
    for j in range(n_back + 1):
        vr = value_rows(vwt_ref, win_starts[j])
        for r in range(R):
            s_t = sw_ref[j, r]
            if j == 0:
                s_t = jnp.where(causal, s_t, MASKED)
            elif j == n_back:
                s_t = jnp.where(k_off > t_off, s_t, MASKED)
            more(win_state, r, s_t, vr)
    bufs = (sa_ref, sb_ref)

    def sel_trip(k, carry):
        for u in range(SEL_UNROLL):
            c = SEL_UNROLL * k + u
            issue(c + 1, bufs[(u + 1) % 2])
            consume(c, bufs[u % 2], False)
        return carry

    lax.fori_loop(0, n_plain // SEL_UNROLL, sel_trip, 0)

    def finish():
        for r in range(R):
            mix(sel_state, r, 1)
            mix(win_state, r, 2)
        o_ref[...] = out_ref[...].T.astype(BF16)

    for rem in range(SEL_UNROLL):
        @pl.when(n_plain % SEL_UNROLL == rem)
        def _():
            for u in range(rem):
                c = n_plain - rem + u
                issue(c + 1, bufs[(u + 1) % 2])
                consume(c, bufs[u % 2], False)
            consume(n_plain, bufs[rem % 2], True)
            finish()


def _nsa(nq, ck, cvt, kdup, vt, gt, ovt, eb, B, S, tq):
    T = B * S
    G = NSA_KV_HEADS
    assert S % tq == 0 and WINDOW % tq == 0 and S // SEL_BLOCK <= LANES - HEAD_DIM
    nqt = S // tq
    ncp = ck.shape[2]
    kdup3 = kdup.reshape(B, S, kdup.shape[1])
    acc_rows = HEAD_DIM + BF16_ROWS
    keys = lambda off: pl.BlockSpec((1, S, LANES), lambda b, g, i: (b, 0, off + g))
    vals = lambda off: pl.BlockSpec((HEAD_DIM, S), lambda b, g, i: (off + g, b))
    full = lambda a: pl.BlockSpec(a.shape, lambda b, g, i: (0,) * a.ndim)
    return pl.pallas_call(
        functools.partial(_nsa_kernel, tq=tq, top=min(SEL_TOPK, S // SEL_BLOCK)),
        grid=(B, G, nqt),
        in_specs=[pl.BlockSpec((NSA_GROUP * HEAD_DIM, tq), lambda b, g, i: (g, b * nqt + i)),
                  pl.BlockSpec((1, 1, ncp, LANES), lambda b, g, i: (b, g, 0, 0)),
                  pl.BlockSpec((1, 1, HEAD_DIM, ncp), lambda b, g, i: (b, g, 0, 0)),
                  keys(0), keys(G), vals(0), vals(G),
                  pl.BlockSpec((GATE_ROWS, tq), lambda b, g, i: (g, b * nqt + i)),
                  full(ovt), full(eb)],
        out_specs=pl.BlockSpec((tq, NSA_GROUP * HEAD_DIM), lambda b, g, i: (b * nqt + i, g)),
        out_shape=jax.ShapeDtypeStruct((T, NSA_WIDTH), BF16),
        scratch_shapes=[pltpu.VMEM((NSA_GROUP, 1, tq), F32),
                        pltpu.VMEM((NSA_GROUP, acc_rows, tq), F32),
                        pltpu.VMEM((NSA_GROUP, 1, tq), F32),
                        pltpu.VMEM((NSA_GROUP, acc_rows, tq), F32),
                        pltpu.VMEM((NSA_GROUP * HEAD_DIM, tq), F32),
                        pltpu.VMEM((S // SEL_BLOCK, tq), F32),
                        pltpu.VMEM((S // SEL_BLOCK, tq), F32),
                        pltpu.VMEM((NSA_GROUP, tq, tq), F32),
                        pltpu.VMEM((NSA_GROUP, tq, tq), F32),
                        pltpu.VMEM((WINDOW // tq + 1, NSA_GROUP, tq, tq), F32)],
        compiler_params=_params("parallel", "parallel", "parallel"),
    )(nq, ck, cvt, kdup3, kdup3, vt, vt, gt, ovt, eb)


def _nsa_constants(S):
    n_cmp = (S - CMP_BLOCK) // CMP_STRIDE + 1
    ncp = S // CMP_STRIDE
    n_sel = S // SEL_BLOCK
    ci = np.arange(ncp)[None, :]
    sj = np.arange(n_sel)[:, None]
    ovt = ((ci * CMP_STRIDE < (sj + 1) * SEL_BLOCK)
           & (ci * CMP_STRIDE + CMP_BLOCK > sj * SEL_BLOCK) & (ci < n_cmp))
    eb = np.zeros((S, LANES), np.float32)
    eb[np.arange(S), HEAD_DIM + np.arange(S) // SEL_BLOCK] = MASKED
    return jnp.asarray(ovt, BF16), jnp.asarray(eb, BF16)


def _mlp_kernel(ret_ref, nsa_ref, x_ref, p_ref, wo_ref, nw_ref, wg_ref, wv_ref, cw_ref, cb_ref,
                wd_ref, pg_ref, pw_ref, fw_ref, o_ref, h_ref, act_ref, tail_ref, *, fc, final):
    @pl.when(pl.program_id(1) == 0)
    def _():
        tail_ref[...] = jnp.zeros_like(tail_ref)

    mix = _dot(ret_ref[...], wo_ref[:RET_WIDTH, :]) + _dot(nsa_ref[...], wo_ref[RET_WIDTH:, :])
    h = x_ref[...] + mix
    h_ref[...] = h
    ms = jnp.mean(h * h, axis=-1, keepdims=True)
    hn = (h * lax.rsqrt(ms + EPS) * nw_ref[...]).astype(BF16)

    tm = hn.shape[0]
    row = lax.broadcasted_iota(jnp.int32, (tm, fc), 0)
    for c in range(wg_ref.shape[1] // fc):
        sl = slice(c * fc, (c + 1) * fc)
        g = _dot(hn, wg_ref[:, sl])
        val = _dot(hn, wv_ref[:, sl])
        tail = tail_ref[:, sl]
        g1 = jnp.where(row == 0, tail[7:8, :], pltpu.roll(g, 1, 0))
        g2 = jnp.where(row == 0, tail[6:7, :],
                       jnp.where(row == 1, tail[7:8, :], pltpu.roll(g, 2, 0)))
        tail_ref[:, sl] = g[tm - 8:, :]
        cw = cw_ref[:, sl]
        y = cb_ref[:, sl] + cw[0:1, :] * g2 + cw[1:2, :] * g1 + cw[2:3, :] * g
        act_ref[:, sl] = (y * jax.nn.sigmoid(y) * val).astype(BF16)

    h = h_ref[...] + _dot(act_ref[...], wd_ref[...])
    gate = jax.nn.sigmoid(_dot(h.astype(BF16), pg_ref[...]))
    h = h + gate * _dot(p_ref[...].astype(BF16), pw_ref[...])
    if final:
        ms = jnp.mean(h * h, axis=-1, keepdims=True)
        h = h * lax.rsqrt(ms + EPS) * fw_ref[...]
    o_ref[...] = h


def _mlp(ret_out, nsa_out, x2, p2, w_out, norm_w, wg, wv, conv_w, conv_b, wd, pg, pw, fw,
         B, S, tm, final):
    T, D = x2.shape
    F = wg.shape[1]
    nt = S // tm
    row = lambda w: pl.BlockSpec((tm, w), lambda b, j: (b * nt + j, 0))
    full = lambda a: pl.BlockSpec(a.shape, lambda b, j: (0,) * a.ndim, pipeline_mode=pl.Buffered(1))
    weights = (w_out, norm_w, wg, wv, conv_w, conv_b, wd, pg, pw, fw)
    return pl.pallas_call(
        functools.partial(_mlp_kernel, fc=256, final=final),
        grid=(B, nt),
        in_specs=[row(RET_WIDTH), row(NSA_WIDTH), row(D), row(p2.shape[1])]
                 + [full(w) for w in weights],
        out_specs=row(D),
        out_shape=jax.ShapeDtypeStruct((T, D), F32),
        scratch_shapes=[pltpu.VMEM((tm, D), F32), pltpu.VMEM((tm, F), BF16),
                        pltpu.VMEM((8, F), F32)],
        compiler_params=_params("arbitrary", "arbitrary"),
    )(ret_out, nsa_out, x2, p2, *weights)


def _rope_tables(tm):
    lane = np.arange(LANES) % HEAD_DIM
    half_r = HEAD_DIM // 2
    inv_r = jnp.power(jnp.float32(RET_THETA), -jnp.arange(half_r, dtype=F32) / half_r)
    half_n = ROPE_DIM // 2
    inv_n = jnp.power(jnp.float32(ROPE_THETA), -jnp.arange(half_n, dtype=F32) / half_n)
    signs = [jnp.asarray(np.where(lane < half_r, -1.0, 1.0), F32),
             jnp.asarray(np.where(lane < half_n, -1.0, 1.0), F32)]
    table = jnp.concatenate([jnp.stack(signs), jnp.zeros((6, LANES), F32)], axis=0)
    freq = jnp.concatenate([inv_r, inv_n])
    return table, jnp.broadcast_to(freq[:, None], (half_r + half_n, tm))


def _layer(h2, p2, pos2, B, S, final, norm_mix_w, w_in, ret_gn_w, cmp_pos, cmp_k_w1, cmp_k_w2,
           cmp_v_w1, cmp_v_w2, w_out, norm_ffn_w, ffn_w_up, ffn_conv_w, ffn_conv_b, ffn_w_down,
           ple_w, ple_gate_w, final_norm_w):
    T, D = h2.shape
    tm = _row_tile(S, 512)

    w_main, w_t = _in_proj_weights(w_in)
    ret_qkv, rg, qt, cmp_tok, kdup, vt, gt = _in_proj(
        h2, pos2, norm_mix_w.reshape(1, D), w_main, w_t, *_rope_tables(tm), tm)

    dec, qd, kd, cdm = _retention_constants()
    ret_out = _retention(ret_qkv, rg, ret_gn_w.reshape(1, RET_WIDTH), dec, qd, kd, cdm, B, S)

    w1 = jnp.stack([cmp_k_w1, cmp_v_w1]).astype(BF16)
    w2k = jnp.tile(cmp_k_w2, (1, HEADS_PER_VREG)).astype(BF16)
    ck, cvt = _compress(cmp_tok, cmp_pos.reshape(1, CMP_BLOCK * HEAD_DIM), w1, w2k,
                        cmp_v_w2.T.astype(BF16), B)

    ovt, eb = _nsa_constants(S)
    nsa_out = _nsa(qt, ck, cvt, kdup, vt, gt, ovt, eb, B, S, tq=256)

    d_ff = ffn_w_down.shape[0]
    conv_w = jnp.pad(ffn_conv_w, ((0, 8 - ffn_conv_w.shape[0]), (0, 0)))
    return _mlp(ret_out, nsa_out, h2, p2, w_out.astype(BF16), norm_ffn_w.reshape(1, D),
                ffn_w_up[:, :d_ff].astype(BF16), ffn_w_up[:, d_ff:].astype(BF16),
                conv_w, ffn_conv_b.reshape(1, d_ff), ffn_w_down.astype(BF16),
                ple_gate_w.astype(BF16), ple_w.astype(BF16), final_norm_w.reshape(1, D),
                B, S, tm, final)


def kernel(x, p, positions, norm_mix_w, w_in, ret_gn_w, cmp_pos, cmp_k_w1, cmp_k_w2, cmp_v_w1, cmp_v_w2, w_out, norm_ffn_w, ffn_w_up, ffn_conv_w, ffn_conv_b, ffn_w_down, ple_w, ple_gate_w, final_norm_w):
    B, S, D = x.shape
    T = B * S
    depth = w_in.shape[0]
    h = x.reshape(T, D)
    pos2 = positions.reshape(T).astype(jnp.int32)
    for i in range(depth):
        h = _layer(h, p[i].reshape(T, -1), pos2, B, S, i == depth - 1, norm_mix_w[i], w_in[i],
                   ret_gn_w[i], cmp_pos[i], cmp_k_w1[i], cmp_k_w2[i], cmp_v_w1[i], cmp_v_w2[i],
                   w_out[i], norm_ffn_w[i], ffn_w_up[i], ffn_conv_w[i], ffn_conv_b[i],
                   ffn_w_down[i], ple_w[i], ple_gate_w[i], final_norm_w)
    return h.reshape(B, S, D)
```

```python
import functools

import numpy as np
import jax
import jax.numpy as jnp
from jax import lax
from jax.experimental import pallas as pl
from jax.experimental.pallas import tpu as pltpu

F32 = jnp.float32
BF16 = jnp.bfloat16

LANES = 128
BF16_ROWS = 16
HEAD_DIM = 64
HEADS_PER_VREG = LANES // HEAD_DIM
RET_HEADS = 8
NSA_Q_HEADS = 8
NSA_KV_HEADS = 2
NSA_GROUP = NSA_Q_HEADS // NSA_KV_HEADS
RET_WIDTH = RET_HEADS * HEAD_DIM
NSA_WIDTH = NSA_Q_HEADS * HEAD_DIM
KV_WIDTH = NSA_KV_HEADS * HEAD_DIM
RET_CHUNK = 128
RET_THETA = 10000.0
ROPE_THETA = 500000.0
ROPE_DIM = HEAD_DIM // 4
CMP_BLOCK = 32
CMP_STRIDE = 16
SEL_BLOCK = 64
SEL_TOPK = 16
WINDOW = 512
N_BRANCH = 3
GATE_ROWS = 16
RANK_STEP = 16
SEL_UNROLL = 8
EPS = 1e-6
QK_SCALE = HEAD_DIM ** -0.5
LOG2_E = 1.4426950408889634
MASKED = -1e30
LOWEST = -3e38

_OFF = np.cumsum([0, RET_WIDTH, RET_WIDTH, RET_WIDTH, RET_WIDTH, NSA_WIDTH,
                  KV_WIDTH, KV_WIDTH, KV_WIDTH, KV_WIDTH, KV_WIDTH, KV_WIDTH])
(_RQ, _RK, _RV, _RG, _NQ, _KC, _VC, _KSL, _VSL, _KWN, _VWN, _NG) = (int(v) for v in _OFF)

VMEM_LIMIT = 56 * 1024 * 1024


def _dot(a, b):
    return jnp.dot(a, b, preferred_element_type=F32)


def _dot_nt(a, b):
    return lax.dot_general(a, b, (((1,), (1,)), ((), ())), preferred_element_type=F32)


def _dot_tn(a, b):
    return lax.dot_general(a, b, (((0,), (0,)), ((), ())), preferred_element_type=F32)


def _params(*semantics):
    return pltpu.CompilerParams(dimension_semantics=semantics, vmem_limit_bytes=VMEM_LIMIT)


def _row_tile(n, pref):
    t = pref
    while n % t:
        t //= 2
    return t


def _rope(y, cos, sin_signed, first_half, half):
    nxt = pltpu.roll(y, LANES - half, 1)
    prv = pltpu.roll(y, half, 1)
    return y * cos + jnp.where(first_half, nxt, prv) * sin_signed


def _in_proj_kernel(x_ref, posrow_ref, nw_ref, w_ref, wt_ref, tab_ref, freq_ref,
                    ret_ref, rg_ref, qt_ref, cmp_ref, kdup_ref, vt_ref, gt_ref, stage_ref):
    x = x_ref[...]
    tm = x.shape[0]
    ms = jnp.mean(x * x, axis=-1, keepdims=True)
    xn = (x * lax.rsqrt(ms + EPS) * nw_ref[...]).astype(BF16)

    tab = tab_ref[...]
    lane = lax.broadcasted_iota(jnp.int32, (tm, LANES), 1)
    in_head = lane % HEAD_DIM
    low = lane < HEAD_DIM
    half_r, half_n = HEAD_DIM // 2, ROPE_DIM // 2
    ang_t = freq_ref[...] * posrow_ref[0].astype(F32)
    cos_t, sin_t = jnp.cos(ang_t), jnp.sin(ang_t)
    pad = jnp.zeros((LANES - half_r - half_n, tm), F32)
    cos_a = jnp.concatenate([cos_t, pad], axis=0).T
    sin_a = jnp.concatenate([sin_t, pad], axis=0).T
    cos_q, sin_q = cos_t[half_r:, :], sin_t[half_r:, :]
    first_r = in_head < half_r
    first_n = in_head < half_n
    rot_n = in_head < ROPE_DIM

    def spread_r(c):
        c = jnp.where(lane < half_r, c, pltpu.roll(c, half_r, 1))
        return jnp.where(low, c, pltpu.roll(c, HEAD_DIM, 1))

    def spread_n(c, rest):
        c = pltpu.roll(c, LANES - half_r, 1)
        c = jnp.where(lane < half_n, c, pltpu.roll(c, half_n, 1))
        c = jnp.where(low, c, pltpu.roll(c, HEAD_DIM, 1))
        return jnp.where(rot_n, c, rest)

    cos_r, sin_r = spread_r(cos_a), spread_r(sin_a) * tab[0:1, :]
    cos_n, sin_n = spread_n(cos_a, 1.0), spread_n(sin_a, 0.0) * tab[1:2, :]

    rope_r = lambda y: _rope(y, cos_r, sin_r, first_r, HEAD_DIM // 2)
    rope_n = lambda y: _rope(y, cos_n, sin_n, first_n, ROPE_DIM // 2)

    def halves(ci):
        y = _dot(xn, w_ref[:, ci * 256:(ci + 1) * 256])
        return y[:, :LANES], y[:, LANES:]

    for ci in range(2):
        for h, y in enumerate(halves(4 + ci)):
            c0 = 2 * RET_WIDTH + ci * 256 + h * LANES
            ret_ref[:, c0:c0 + LANES] = y.astype(BF16)
    t = _dot_nt(wt_ref[...], xn)
    for hd in range(NSA_Q_HEADS):
        b0 = hd * HEAD_DIM
        x1, x2 = t[b0:b0 + half_n, :], t[b0 + half_n:b0 + ROPE_DIM, :]
        head = jnp.concatenate([x1 * cos_q - x2 * sin_q, x1 * sin_q + x2 * cos_q,
                                t[b0 + ROPE_DIM:b0 + HEAD_DIM, :]], axis=0)
        qt_ref[b0:b0 + HEAD_DIM, :] = (head * (QK_SCALE * LOG2_E)).astype(BF16)
    nq, nv = qt_ref.shape[0], vt_ref.shape[0]
    vt_ref[...] = t[nq:nq + nv, :].astype(BF16)
    gt_ref[...] = jax.nn.sigmoid(t[nq + nv:, :])
    for ci in range(2):
        for h, y in enumerate(halves(ci)):
            c0 = ci * 256 + h * LANES
            ret_ref[:, c0:c0 + LANES] = rope_r(y).astype(BF16)
    for ci in range(2):
        for h, y in enumerate(halves(2 + ci)):
            c0 = RET_WIDTH + ci * 256 + h * LANES
            ret_ref[:, c0:c0 + LANES] = (rope_r(y) * QK_SCALE).astype(BF16)
    kc, vc = halves(8)
    for a, y in enumerate((rope_n(kc), vc)):
        stage_ref[2 * a] = y
        stage_ref[2 * a + 1] = pltpu.roll(y, HEAD_DIM, 1)
    n_grp = tm // CMP_STRIDE
    low_g = lax.broadcasted_iota(jnp.int32, (n_grp, LANES), 1) < HEAD_DIM
    for a in range(2):
        for j in range(CMP_STRIDE // 2):
            token = lambda l, s: stage_ref[2 * a + s, pl.ds(l, n_grp, stride=CMP_STRIDE), :]
            even, odd = 2 * j, 2 * j + 1
            cmp_ref[a, 0, :, j * LANES:(j + 1) * LANES] = jnp.where(
                low_g, token(even, 0), token(odd, 1)).astype(BF16)
            cmp_ref[a, 1, :, j * LANES:(j + 1) * LANES] = jnp.where(
                low_g, token(even, 1), token(odd, 0)).astype(BF16)
    for n, y in enumerate(halves(9)):
        y = rope_n(y)
        sw = pltpu.roll(y, HEAD_DIM, 1)
        kdup_ref[:, (2 * n) * LANES:(2 * n + 1) * LANES] = jnp.where(low, y, sw).astype(BF16)
        kdup_ref[:, (2 * n + 1) * LANES:(2 * n + 2) * LANES] = jnp.where(low, sw, y).astype(BF16)
    for ci in range(2):
        for h, y in enumerate(halves(6 + ci)):
            c0 = ci * 256 + h * LANES
            rg_ref[:, c0:c0 + LANES] = y


def _in_proj(x2, positions, norm_w, w_main, w_t, tab, freq, tm):
    T, D = x2.shape
    nv = 2 * KV_WIDTH
    ng = w_t.shape[0] - nv - NSA_WIDTH
    grp_w = CMP_STRIDE * HEAD_DIM
    pos_rows = positions.reshape(T // tm, 1, tm)
    row = lambda w: pl.BlockSpec((tm, w), lambda i: (i, 0))
    col = lambda h: pl.BlockSpec((h, tm), lambda i: (0, i))
    full = lambda a: pl.BlockSpec(a.shape, lambda i: (0,) * a.ndim)
    return pl.pallas_call(
        _in_proj_kernel,
        grid=(T // tm,),
        in_specs=[row(D), pl.BlockSpec((1, 1, tm), lambda i: (i, 0, 0)),
                  full(norm_w), full(w_main), full(w_t), full(tab), full(freq)],
        out_specs=[row(3 * RET_WIDTH), row(RET_WIDTH), col(NSA_WIDTH),
                   pl.BlockSpec((2, NSA_KV_HEADS, tm // CMP_STRIDE, grp_w), lambda i: (0, 0, i, 0)),
                   row(4 * LANES), col(nv), col(ng)],
        out_shape=[jax.ShapeDtypeStruct((T, 3 * RET_WIDTH), BF16),
                   jax.ShapeDtypeStruct((T, RET_WIDTH), F32),
                   jax.ShapeDtypeStruct((NSA_WIDTH, T), BF16),
                   jax.ShapeDtypeStruct((2, NSA_KV_HEADS, T // CMP_STRIDE, grp_w), BF16),
                   jax.ShapeDtypeStruct((T, 4 * LANES), BF16),
                   jax.ShapeDtypeStruct((nv, T), BF16),
                   jax.ShapeDtypeStruct((ng, T), F32)],
        scratch_shapes=[pltpu.VMEM((4, tm, LANES), F32)],
        compiler_params=_params("parallel"),
    )(x2, pos_rows, norm_w, w_main, w_t, tab, freq)


def _in_proj_weights(w_in):
    w_main = jnp.concatenate([w_in[:, :_NQ], w_in[:, _KC:_KSL], w_in[:, _KSL:_VSL],
                              w_in[:, _KWN:_VWN]], axis=1)
    gate_cols = np.full((NSA_KV_HEADS * GATE_ROWS,), -1)
    for g in range(NSA_KV_HEADS):
        for r in range(NSA_GROUP):
            for br in range(N_BRANCH):
                gate_cols[g * GATE_ROWS + r * N_BRANCH + br] = _NG + (g * NSA_GROUP + r) * N_BRANCH + br
    w_gate = jnp.where(gate_cols[None, :] >= 0, w_in[:, np.maximum(gate_cols, 0)], 0.0)
    w_t = jnp.concatenate([w_in[:, _NQ:_KC], w_in[:, _VSL:_KWN], w_in[:, _VWN:_NG], w_gate],
                          axis=1).T
    return w_main.astype(BF16), w_t.astype(BF16)


def _retention_kernel(q_ref, k_ref, v_ref, rg_ref, gnw_ref, dec_ref, qd_ref, kd_ref, cdm_ref,
                      o_ref, state_ref, *, n_chunks, unroll):
    C = RET_CHUNK
    lane = lax.broadcasted_iota(jnp.int32, (C, LANES), 1)
    low = lane < HEAD_DIM
    r_i = lax.broadcasted_iota(jnp.int32, (LANES, LANES), 0)
    c_i = lax.broadcasted_iota(jnp.int32, (LANES, LANES), 1)
    same_head = (r_i < HEAD_DIM) == (c_i < HEAD_DIM)
    state_ref[...] = jnp.zeros_like(state_ref)

    def group(gi, carry):
        rows = [pl.multiple_of((gi * unroll + u) * C, C) for u in range(unroll)]
        q = [q_ref[pl.ds(r0, C), :] for r0 in rows]
        k = [k_ref[pl.ds(r0, C), :] for r0 in rows]
        v = [v_ref[pl.ds(r0, C), :] for r0 in rows]
        zero = jnp.zeros_like(q[0])
        s0 = [_dot_nt(jnp.where(low, q[u], zero), k[u]) for u in range(unroll)]
        s1 = [_dot_nt(jnp.where(low, zero, q[u]), k[u]) for u in range(unroll)]
        kv = [_dot_tn((k[u].astype(F32) * kd_ref[...]).astype(BF16), v[u]) for u in range(unroll)]

        p0 = [(s0[u] * dec_ref[0]).astype(BF16) for u in range(unroll)]
        p1 = [(s1[u] * dec_ref[1]).astype(BF16) for u in range(unroll)]
        states = [state_ref[...]]
        for u in range(unroll):
            states.append(states[u] * cdm_ref[0] + jnp.where(same_head, kv[u], 0.0))
        state_ref[...] = states[unroll]

        o0 = [_dot(p0[u], v[u]) for u in range(unroll)]
        o1 = [_dot(p1[u], v[u]) for u in range(unroll)]
        oc = [_dot((q[u].astype(F32) * qd_ref[...]).astype(BF16), states[u].astype(BF16))
              for u in range(unroll)]

        inv_d = 1.0 / HEAD_DIM
        for u in range(unroll):
            ro = jnp.where(low, o0[u], o1[u]) + oc[u]
            sum0 = jnp.sum(jnp.where(low, ro, 0.0), axis=-1, keepdims=True)
            sum1 = jnp.sum(jnp.where(low, 0.0, ro), axis=-1, keepdims=True)
            d = ro - jnp.where(low, sum0, sum1) * inv_d
            dd = d * d
            var0 = jnp.sum(jnp.where(low, dd, 0.0), axis=-1, keepdims=True)
            var1 = jnp.sum(jnp.where(low, 0.0, dd), axis=-1, keepdims=True)
            var = jnp.where(low, var0, var1) * inv_d
            y = d * lax.rsqrt(var + EPS) * gnw_ref[...]
            g = rg_ref[pl.ds(rows[u], C), :]
            o_ref[pl.ds(rows[u], C), :] = (y * (g * jax.nn.sigmoid(g))).astype(BF16)
        return carry

    lax.fori_loop(0, n_chunks // unroll, group, 0)


def _retention(ret_qkv, rg, gn_w, dec, qd, kd, cdm, B, S):
    T = B * S
    n_pairs = RET_HEADS // HEADS_PER_VREG
    seq = lambda off: pl.BlockSpec((S, LANES), lambda b, hp: (b, off + hp))
    return pl.pallas_call(
        functools.partial(_retention_kernel, n_chunks=S // RET_CHUNK,
                          unroll=_row_tile(S // RET_CHUNK, 16)),
        grid=(B, n_pairs),
        in_specs=[seq(0), seq(n_pairs), seq(2 * n_pairs), seq(0),
                  pl.BlockSpec((1, LANES), lambda b, hp: (0, hp)),
                  pl.BlockSpec((HEADS_PER_VREG, RET_CHUNK, RET_CHUNK), lambda b, hp: (hp, 0, 0)),
                  pl.BlockSpec((RET_CHUNK, LANES), lambda b, hp: (0, hp)),
                  pl.BlockSpec((RET_CHUNK, LANES), lambda b, hp: (0, hp)),
                  pl.BlockSpec((1, LANES, LANES), lambda b, hp: (hp, 0, 0))],
        out_specs=seq(0),
        out_shape=jax.ShapeDtypeStruct((T, RET_WIDTH), BF16),
        scratch_shapes=[pltpu.VMEM((LANES, LANES), F32)],
        compiler_params=_params("parallel", "parallel"),
    )(ret_qkv, ret_qkv, ret_qkv, rg, gn_w, dec, qd, kd, cdm)


def _retention_constants():
    H, C = RET_HEADS, RET_CHUNK
    log_g = jnp.log1p(-jnp.exp2(-5.0 - jnp.arange(H, dtype=F32)))
    pos = jnp.arange(C, dtype=F32)
    diff = pos[:, None] - pos[None, :]
    dec = jnp.where(diff >= 0, jnp.exp(jnp.maximum(diff, 0.0) * log_g[:, None, None]), 0.0)
    q_decay = jnp.exp((pos + 1.0) * log_g[:, None]).T
    k_decay = jnp.exp((C - 1.0 - pos) * log_g[:, None]).T
    chunk_decay = jnp.exp(C * log_g)
    qd = jnp.repeat(q_decay, HEAD_DIM, axis=1)
    kd = jnp.repeat(k_decay, HEAD_DIM, axis=1)
    cd_rows = jnp.repeat(chunk_decay, HEAD_DIM).reshape(H // HEADS_PER_VREG, LANES, 1)
    head_of = jnp.arange(LANES) // HEAD_DIM
    same = (head_of[:, None] == head_of[None, :]).astype(F32)
    return dec, qd, kd, cd_rows * same[None]


def _compress_kernel(tk_ref, tv_ref, pos_ref, w1_ref, w2k_ref, w2vt_ref, ck_ref, cvt_ref):
    pos = jnp.broadcast_to(pos_ref[...], (8, pos_ref.shape[1])).astype(BF16)

    def hidden(t, w1_ref_a):
        half = t.shape[1]
        bias = _dot(pos, w1_ref_a[...])[0:1, :]
        first = _dot(t, w1_ref_a[:half, :])
        second = _dot(t, w1_ref_a[half:, :])
        n = t.shape[0]
        h = first + pltpu.roll(second, n - 1, 0) + bias
        return jax.nn.gelu(h).astype(BF16)

    ck_ref[0, 0] = _dot(hidden(tk_ref[0, 0], w1_ref.at[0]), w2k_ref[...]).astype(BF16)
    cvt_ref[0, 0] = _dot_nt(w2vt_ref[...], hidden(tv_ref[0, 0], w1_ref.at[1])).astype(BF16)


def _compress(cmp_tok, pos_flat, w1, w2k, w2vt, B):
    _, G, rows, width = cmp_tok.shape
    n = rows // B
    tok = lambda a: pl.BlockSpec((1, 1, n, width), lambda b, g: (a, g, b, 0))
    full = lambda a: pl.BlockSpec(a.shape, lambda b, g: (0,) * a.ndim)
    return pl.pallas_call(
        _compress_kernel,
        grid=(B, G),
        in_specs=[tok(0), tok(1), full(pos_flat), full(w1), full(w2k), full(w2vt)],
        out_specs=[pl.BlockSpec((1, 1, n, LANES), lambda b, g: (b, g, 0, 0)),
                   pl.BlockSpec((1, 1, HEAD_DIM, n), lambda b, g: (b, g, 0, 0))],
        out_shape=[jax.ShapeDtypeStruct((B, G, n, LANES), BF16),
                   jax.ShapeDtypeStruct((B, G, HEAD_DIM, n), BF16)],
        compiler_params=_params("parallel", "parallel"),
    )(cmp_tok, cmp_tok, pos_flat, w1, w2k, w2vt)


def _nsa_kernel(q_ref, ck_ref, cvt_ref, ks_ref, kw_ref, vst_ref, vwt_ref, gt_ref, ovt_ref, eb_ref,
                o_ref, m_ref, acc_ref, mw_ref, accw_ref, out_ref, score_ref, ahead_ref,
                sa_ref, sb_ref, sw_ref,
                *, tq, top):
    R = NSA_GROUP
    D = HEAD_DIM
    t0 = pl.multiple_of(pl.program_id(2) * tq, tq)
    t_off = lax.broadcasted_iota(jnp.int32, (1, tq), 1)
    tcol = t0 + t_off
    k_off = lax.broadcasted_iota(jnp.int32, (tq, 1), 0)
    low = lax.broadcasted_iota(jnp.int32, (tq, LANES), 1) < D
    ones_rows = jnp.ones((BF16_ROWS, tq), BF16)
    gate = lambda r, br: gt_ref[r * N_BRANCH + br:r * N_BRANCH + br + 1, :]

    def q_aug(r, extra):
        return jnp.concatenate([q_ref[r * D:(r + 1) * D, :], extra], axis=0)

    zero_rows = jnp.zeros((LANES - D, tq), BF16)

    def value_rows(vt_ref_, start):
        return jnp.concatenate([vt_ref_[:, pl.ds(start, tq)], ones_rows], axis=0)

    sel_state, win_state = (m_ref, acc_ref), (mw_ref, accw_ref)

    def reset(state):
        for r in range(R):
            state[0][r] = jnp.full((1, tq), LOWEST, F32)
            state[1][r] = jnp.zeros(acc_ref.shape[1:], F32)

    def more(state, r, s_t, vrows):
        m_old = state[0][r]
        m_new = jnp.maximum(m_old, jnp.max(s_t, axis=0, keepdims=True))
        state[0][r] = m_new
        state[1][r] = (jnp.exp2(m_old - m_new) * state[1][r]
                       + _dot(vrows, jnp.exp2(s_t - m_new).astype(BF16)))

    def mix(state, r, br):
        o = state[1][r, :D, :] * ((1.0 / state[1][r, D:D + 1, :]) * gate(r, br))
        out_ref[r * D:(r + 1) * D, :] = out_ref[r * D:(r + 1) * D, :] + o

    ck = ck_ref[0, 0]
    ncp = ck.shape[0]
    cvt = jnp.concatenate([cvt_ref[0, 0], jnp.ones((BF16_ROWS, ncp), BF16)], axis=0)
    cmp_end = lax.broadcasted_iota(jnp.int32, (ncp, 1), 0) * CMP_STRIDE + (CMP_BLOCK - 1)
    cmask = cmp_end <= tcol
    sees_any = jnp.where(tcol >= CMP_BLOCK - 1, 1.0, 0.0)
    n_sel = ovt_ref.shape[0]
    imp_t = jnp.zeros((n_sel, tq), F32)
    scores = [_dot(ck, q_aug(r, zero_rows)) for r in range(R)]

    n_back = WINDOW // tq
    win_starts = [pl.multiple_of(jnp.maximum(t0 - j * tq, 0), tq) for j in range(n_back + 1)]
    flag_lane = lax.broadcasted_iota(jnp.int32, (tq, LANES), 1) == D
    flag_row = lax.broadcasted_iota(jnp.int32, (LANES - D, tq), 0) == 0
    qw = [q_aug(r, jnp.where(flag_row, 1.0, 0.0).astype(BF16)) for r in range(R)]

    def issue_window(j):
        off = jnp.full((tq, LANES), jnp.where(t0 >= j * tq, 0.0, MASKED), F32).astype(BF16)
        kw = jnp.where(flag_lane, off, kw_ref[0, pl.ds(win_starts[j], tq), :])
        for r in range(R):
            sw_ref[j, r] = _dot(kw, qw[r])

    for r in range(R):
        for j in range(r, n_back + 1, R):
            issue_window(j)
        s = jnp.where(cmask, scores[r], MASKED)
        e = jnp.exp2(s - jnp.max(s, axis=0, keepdims=True)).astype(BF16)
        acc = _dot(cvt, e)
        inv_l = sees_any * (1.0 / acc[D:D + 1, :])
        out_ref[r * D:(r + 1) * D, :] = acc[:D, :] * (inv_l * gate(r, 0))
        imp_t = imp_t + _dot(ovt_ref[...], e) * inv_l

    blk_i = lax.broadcasted_iota(jnp.int32, (n_sel, tq), 0)
    cur = tcol // SEL_BLOCK
    valid = blk_i * SEL_BLOCK <= tcol
    forced = (blk_i == 0) | (blk_i == cur) | (blk_i == cur - 1)
    score_ref[...] = jnp.where(forced, jnp.inf, jnp.where(valid, imp_t, -jnp.inf))
    ahead_ref[...] = jnp.zeros(ahead_ref.shape, F32)
    live_blocks = (t0 + tq) // SEL_BLOCK

    def count(jg, kb_lo, kb_hi):
        mine = score_ref[jg * 8:(jg + 1) * 8, :]
        blk_g = jg * 8 + lax.broadcasted_iota(jnp.int32, (8, tq), 0)
        ahead = ahead_ref[jg * 8:(jg + 1) * 8, :]
        for kb in range(kb_lo, kb_hi):
            other = score_ref[kb:kb + 1, :]
            if kb < jg * 8:
                ahead = ahead + jnp.where(other >= mine, 1.0, 0.0)
            elif kb >= (jg + 1) * 8:
                ahead = ahead + jnp.where(other > mine, 1.0, 0.0)
            else:
                ahead = ahead + jnp.where(blk_g > kb, jnp.where(other >= mine, 1.0, 0.0),
                                          jnp.where(other > mine, 1.0, 0.0))
        ahead_ref[jg * 8:(jg + 1) * 8, :] = ahead

    def rank_level(lo, hi):
        for jg in range(lo // 8):
            count(jg, lo, hi)
        for jg in range(lo // 8, hi // 8):
            count(jg, 0, hi)

    step = RANK_STEP if n_sel % RANK_STEP == 0 else n_sel
    rank_level(0, step)
    for lo in range(step, n_sel, step):
        @pl.when(live_blocks > lo)
        def _():
            rank_level(lo, lo + step)

    pad_rows = LANES - D - n_sel
    not_sel = jnp.where(ahead_ref[...] < top, 0.0, 1.0)
    if pad_rows:
        not_sel = jnp.concatenate([not_sel, jnp.zeros((pad_rows, tq), F32)], axis=0)
    not_sel = not_sel.astype(BF16)
    qa = [q_aug(r, not_sel) for r in range(R)]

    def keys_aug(start):
        return jnp.where(low, ks_ref[0, pl.ds(start, tq), :], eb_ref[pl.ds(start, tq), :])

    def issue(c, buf_ref):
        ka = keys_aug(pl.multiple_of(c * tq, tq))
        for r in range(R):
            buf_ref[r] = _dot(ka, qa[r])

    def consume(c, buf_ref, own):
        vr = value_rows(vst_ref, pl.multiple_of(c * tq, tq))
        for r in range(R):
            more(sel_state, r, jnp.where(causal, buf_ref[r], MASKED) if own else buf_ref[r], vr)

    causal = k_off <= t_off
    n_plain = pl.program_id(2)
    reset(sel_state)
    reset(win_state)
    issue(0, sa_ref)
    for j in range(n_back + 1):
        vr = value_rows(vwt_ref, win_starts[j])
        for r in range(R):
            s_t = sw_ref[j, r]
            if j == 0:
                s_t = jnp.where(causal, s_t, MASKED)
            elif j == n_back:
                s_t = jnp.where(k_off > t_off, s_t, MASKED)
            more(win_state, r, s_t, vr)
    bufs = (sa_ref, sb_ref)

    def sel_trip(k, carry):
        for u in range(SEL_UNROLL):
            c = SEL_UNROLL * k + u
            issue(c + 1, bufs[(u + 1) % 2])
            consume(c, bufs[u % 2], False)
        return carry

    lax.fori_loop(0, n_plain // SEL_UNROLL, sel_trip, 0)

    def finish():
        for r in range(R):
            mix(sel_state, r, 1)
            mix(win_state, r, 2)
        o_ref[...] = out_ref[...].T.astype(BF16)

    for rem in range(SEL_UNROLL):
        @pl.when(n_plain % SEL_UNROLL == rem)
        def _():
            for u in range(rem):
                c = n_plain - rem + u
                issue(c + 1, bufs[(u + 1) % 2])
                consume(c, bufs[u % 2], False)
            consume(n_plain, bufs[rem % 2], True)
            finish()


def _nsa(nq, ck, cvt, kdup, vt, gt, ovt, eb, B, S, tq):
    T = B * S
    G = NSA_KV_HEADS
    assert S % tq == 0 and WINDOW % tq == 0 and S // SEL_BLOCK <= LANES - HEAD_DIM
    nqt = S // tq
    ncp = ck.shape[2]
    kdup3 = kdup.reshape(B, S, kdup.shape[1])
    acc_rows = HEAD_DIM + BF16_ROWS
    keys = lambda off: pl.BlockSpec((1, S, LANES), lambda b, g, i: (b, 0, off + g))
    vals = lambda off: pl.BlockSpec((HEAD_DIM, S), lambda b, g, i: (off + g, b))
    full = lambda a: pl.BlockSpec(a.shape, lambda b, g, i: (0,) * a.ndim)
    return pl.pallas_call(
        functools.partial(_nsa_kernel, tq=tq, top=min(SEL_TOPK, S // SEL_BLOCK)),
        grid=(B, G, nqt),
        in_specs=[pl.BlockSpec((NSA_GROUP * HEAD_DIM, tq), lambda b, g, i: (g, b * nqt + i)),
                  pl.BlockSpec((1, 1, ncp, LANES), lambda b, g, i: (b, g, 0, 0)),
                  pl.BlockSpec((1, 1, HEAD_DIM, ncp), lambda b, g, i: (b, g, 0, 0)),
                  keys(0), keys(G), vals(0), vals(G),
                  pl.BlockSpec((GATE_ROWS, tq), lambda b, g, i: (g, b * nqt + i)),
                  full(ovt), full(eb)],
        out_specs=pl.BlockSpec((tq, NSA_GROUP * HEAD_DIM), lambda b, g, i: (b * nqt + i, g)),
        out_shape=jax.ShapeDtypeStruct((T, NSA_WIDTH), BF16),
        scratch_shapes=[pltpu.VMEM((NSA_GROUP, 1, tq), F32),
                        pltpu.VMEM((NSA_GROUP, acc_rows, tq), F32),
                        pltpu.VMEM((NSA_GROUP, 1, tq), F32),
                        pltpu.VMEM((NSA_GROUP, acc_rows, tq), F32),
                        pltpu.VMEM((NSA_GROUP * HEAD_DIM, tq), F32),
                        pltpu.VMEM((S // SEL_BLOCK, tq), F32),
                        pltpu.VMEM((S // SEL_BLOCK, tq), F32),
                        pltpu.VMEM((NSA_GROUP, tq, tq), F32),
                        pltpu.VMEM((NSA_GROUP, tq, tq), F32),
                        pltpu.VMEM((WINDOW // tq + 1, NSA_GROUP, tq, tq), F32)],
        compiler_params=_params("parallel", "parallel", "parallel"),
    )(nq, ck, cvt, kdup3, kdup3, vt, vt, gt, ovt, eb)


def _nsa_constants(S):
    n_cmp = (S - CMP_BLOCK) // CMP_STRIDE + 1
    ncp = S // CMP_STRIDE
    n_sel = S // SEL_BLOCK
    ci = np.arange(ncp)[None, :]
    sj = np.arange(n_sel)[:, None]
    ovt = ((ci * CMP_STRIDE < (sj + 1) * SEL_BLOCK)
           & (ci * CMP_STRIDE + CMP_BLOCK > sj * SEL_BLOCK) & (ci < n_cmp))
    eb = np.zeros((S, LANES), np.float32)
    eb[np.arange(S), HEAD_DIM + np.arange(S) // SEL_BLOCK] = MASKED
    return jnp.asarray(ovt, BF16), jnp.asarray(eb, BF16)


def _mlp_kernel(ret_ref, nsa_ref, x_ref, p_ref, wo_ref, nw_ref, wg_ref, wv_ref, cw_ref, cb_ref,
                wd_ref, pg_ref, pw_ref, fw_ref, o_ref, h_ref, act_ref, tail_ref, *, fc, final):
    @pl.when(pl.program_id(1) == 0)
    def _():
        tail_ref[...] = jnp.zeros_like(tail_ref)

    mix = _dot(ret_ref[...], wo_ref[:RET_WIDTH, :]) + _dot(nsa_ref[...], wo_ref[RET_WIDTH:, :])
    h = x_ref[...] + mix
    h_ref[...] = h
    ms = jnp.mean(h * h, axis=-1, keepdims=True)
    hn = (h * lax.rsqrt(ms + EPS) * nw_ref[...]).astype(BF16)

    tm = hn.shape[0]
    row = lax.broadcasted_iota(jnp.int32, (tm, fc), 0)
    for c in range(wg_ref.shape[1] // fc):
        sl = slice(c * fc, (c + 1) * fc)
        g = _dot(hn, wg_ref[:, sl])
        val = _dot(hn, wv_ref[:, sl])
        tail = tail_ref[:, sl]
        g1 = jnp.where(row == 0, tail[7:8, :], pltpu.roll(g, 1, 0))
        g2 = jnp.where(row == 0, tail[6:7, :],
                       jnp.where(row == 1, tail[7:8, :], pltpu.roll(g, 2, 0)))
        tail_ref[:, sl] = g[tm - 8:, :]
        cw = cw_ref[:, sl]
        y = cb_ref[:, sl] + cw[0:1, :] * g2 + cw[1:2, :] * g1 + cw[2:3, :] * g
        act_ref[:, sl] = (y * jax.nn.sigmoid(y) * val).astype(BF16)

    h = h_ref[...] + _dot(act_ref[...], wd_ref[...])
    gate = jax.nn.sigmoid(_dot(h.astype(BF16), pg_ref[...]))
    h = h + gate * _dot(p_ref[...].astype(BF16), pw_ref[...])
    if final:
        ms = jnp.mean(h * h, axis=-1, keepdims=True)
        h = h * lax.rsqrt(ms + EPS) * fw_ref[...]
    o_ref[...] = h


def _mlp(ret_out, nsa_out, x2, p2, w_out, norm_w, wg, wv, conv_w, conv_b, wd, pg, pw, fw,
         B, S, tm, final):
    T, D = x2.shape
    F = wg.shape[1]
    nt = S // tm
    row = lambda w: pl.BlockSpec((tm, w), lambda b, j: (b * nt + j, 0))
    full = lambda a: pl.BlockSpec(a.shape, lambda b, j: (0,) * a.ndim, pipeline_mode=pl.Buffered(1))
    weights = (w_out, norm_w, wg, wv, conv_w, conv_b, wd, pg, pw, fw)
    return pl.pallas_call(
        functools.partial(_mlp_kernel, fc=256, final=final),
        grid=(B, nt),
        in_specs=[row(RET_WIDTH), row(NSA_WIDTH), row(D), row(p2.shape[1])]
                 + [full(w) for w in weights],
        out_specs=row(D),
        out_shape=jax.ShapeDtypeStruct((T, D), F32),
        scratch_shapes=[pltpu.VMEM((tm, D), F32), pltpu.VMEM((tm, F), BF16),
                        pltpu.VMEM((8, F), F32)],
        compiler_params=_params("arbitrary", "arbitrary"),
    )(ret_out, nsa_out, x2, p2, *weights)


def _rope_tables(tm):
    lane = np.arange(LANES) % HEAD_DIM
    half_r = HEAD_DIM // 2
    inv_r = jnp.power(jnp.float32(RET_THETA), -jnp.arange(half_r, dtype=F32) / half_r)
    half_n = ROPE_DIM // 2
    inv_n = jnp.power(jnp.float32(ROPE_THETA), -jnp.arange(half_n, dtype=F32) / half_n)
    signs = [jnp.asarray(np.where(lane < half_r, -1.0, 1.0), F32),
             jnp.asarray(np.where(lane < half_n, -1.0, 1.0), F32)]
    table = jnp.concatenate([jnp.stack(signs), jnp.zeros((6, LANES), F32)], axis=0)
    freq = jnp.concatenate([inv_r, inv_n])
    return table, jnp.broadcast_to(freq[:, None], (half_r + half_n, tm))


def _layer(h2, p2, pos2, B, S, final, norm_mix_w, w_in, ret_gn_w, cmp_pos, cmp_k_w1, cmp_k_w2,
           cmp_v_w1, cmp_v_w2, w_out, norm_ffn_w, ffn_w_up, ffn_conv_w, ffn_conv_b, ffn_w_down,
           ple_w, ple_gate_w, final_norm_w):
    T, D = h2.shape
    tm = _row_tile(S, 512)

    w_main, w_t = _in_proj_weights(w_in)
    ret_qkv, rg, qt, cmp_tok, kdup, vt, gt = _in_proj(
        h2, pos2, norm_mix_w.reshape(1, D), w_main, w_t, *_rope_tables(tm), tm)

    dec, qd, kd, cdm = _retention_constants()
    ret_out = _retention(ret_qkv, rg, ret_gn_w.reshape(1, RET_WIDTH), dec, qd, kd, cdm, B, S)

    w1 = jnp.stack([cmp_k_w1, cmp_v_w1]).astype(BF16)
    w2k = jnp.tile(cmp_k_w2, (1, HEADS_PER_VREG)).astype(BF16)
    ck, cvt = _compress(cmp_tok, cmp_pos.reshape(1, CMP_BLOCK * HEAD_DIM), w1, w2k,
                        cmp_v_w2.T.astype(BF16), B)

    ovt, eb = _nsa_constants(S)
    nsa_out = _nsa(qt, ck, cvt, kdup, vt, gt, ovt, eb, B, S, tq=256)

    d_ff = ffn_w_down.shape[0]
    conv_w = jnp.pad(ffn_conv_w, ((0, 8 - ffn_conv_w.shape[0]), (0, 0)))
    return _mlp(ret_out, nsa_out, h2, p2, w_out.astype(BF16), norm_ffn_w.reshape(1, D),
                ffn_w_up[:, :d_ff].astype(BF16), ffn_w_up[:, d_ff:].astype(BF16),
                conv_w, ffn_conv_b.reshape(1, d_ff), ffn_w_down.astype(BF16),
                ple_gate_w.astype(BF16), ple_w.astype(BF16), final_norm_w.reshape(1, D),
                B, S, tm, final)


def kernel(x, p, positions, norm_mix_w, w_in, ret_gn_w, cmp_pos, cmp_k_w1, cmp_k_w2, cmp_v_w1, cmp_v_w2, w_out, norm_ffn_w, ffn_w_up, ffn_conv_w, ffn_conv_b, ffn_w_down, ple_w, ple_gate_w, final_norm_w):
    B, S, D = x.shape
    T = B * S
    depth = w_in.shape[0]
    h = x.reshape(T, D)
    pos2 = positions.reshape(T).astype(jnp.int32)
    for i in range(depth):
        h = _layer(h, p[i].reshape(T, -1), pos2, B, S, i == depth - 1, norm_mix_w[i], w_in[i],
                   ret_gn_w[i], cmp_pos[i], cmp_k_w1[i], cmp_k_w2[i], cmp_v_w1[i], cmp_v_w2[i],
                   w_out[i], norm_ffn_w[i], ffn_w_up[i], ffn_conv_w[i], ffn_conv_b[i],
                   ffn_w_down[i], ple_w[i], ple_gate_w[i], final_norm_w)
    return h.reshape(B, S, D)
```

```python
import functools

import numpy as np
import jax
import jax.numpy as jnp
from jax import lax
from jax.experimental import pallas as pl
from jax.experimental.pallas import tpu as pltpu

F32 = jnp.float32
BF16 = jnp.bfloat16

LANES = 128
BF16_ROWS = 16
HEAD_DIM = 64
HEADS_PER_VREG = LANES // HEAD_DIM
RET_HEADS = 8
NSA_Q_HEADS = 8
NSA_KV_HEADS = 2
NSA_GROUP = NSA_Q_HEADS // NSA_KV_HEADS
RET_WIDTH = RET_HEADS * HEAD_DIM
NSA_WIDTH = NSA_Q_HEADS * HEAD_DIM
KV_WIDTH = NSA_KV_HEADS * HEAD_DIM
RET_CHUNK = 128
RET_THETA = 10000.0
ROPE_THETA = 500000.0
ROPE_DIM = HEAD_DIM // 4
CMP_BLOCK = 32
CMP_STRIDE = 16
SEL_BLOCK = 64
SEL_TOPK = 16
WINDOW = 512
N_BRANCH = 3
GATE_ROWS = 16
RANK_STEP = 8
SEL_UNROLL = 8
EPS = 1e-6
QK_SCALE = HEAD_DIM ** -0.5
LOG2_E = 1.4426950408889634
MASKED = -1e30
LOWEST = -3e38

_OFF = np.cumsum([0, RET_WIDTH, RET_WIDTH, RET_WIDTH, RET_WIDTH, NSA_WIDTH,
                  KV_WIDTH, KV_WIDTH, KV_WIDTH, KV_WIDTH, KV_WIDTH, KV_WIDTH])
(_RQ, _RK, _RV, _RG, _NQ, _KC, _VC, _KSL, _VSL, _KWN, _VWN, _NG) = (int(v) for v in _OFF)

VMEM_LIMIT = 56 * 1024 * 1024


def _dot(a, b):
    return jnp.dot(a, b, preferred_element_type=F32)


def _dot_nt(a, b):
    return lax.dot_general(a, b, (((1,), (1,)), ((), ())), preferred_element_type=F32)


def _dot_tn(a, b):
    return lax.dot_general(a, b, (((0,), (0,)), ((), ())), preferred_element_type=F32)


def _params(*semantics):
    return pltpu.CompilerParams(dimension_semantics=semantics, vmem_limit_bytes=VMEM_LIMIT)


def _row_tile(n, pref):
    t = pref
    while n % t:
        t //= 2
    return t


def _rope(y, cos, sin_signed, first_half, half):
    nxt = pltpu.roll(y, LANES - half, 1)
    prv = pltpu.roll(y, half, 1)
    return y * cos + jnp.where(first_half, nxt, prv) * sin_signed


def _in_proj_kernel(x_ref, posrow_ref, nw_ref, w_ref, wt_ref, tab_ref, freq_ref,
                    ret_ref, rg_ref, qt_ref, cmp_ref, kdup_ref, vt_ref, gt_ref, stage_ref):
    x = x_ref[...]
    tm = x.shape[0]
    ms = jnp.mean(x * x, axis=-1, keepdims=True)
    xn = (x * lax.rsqrt(ms + EPS) * nw_ref[...]).astype(BF16)

    tab = tab_ref[...]
    lane = lax.broadcasted_iota(jnp.int32, (tm, LANES), 1)
    in_head = lane % HEAD_DIM
    low = lane < HEAD_DIM
    half_r, half_n = HEAD_DIM // 2, ROPE_DIM // 2
    ang_t = freq_ref[...] * posrow_ref[0].astype(F32)
    cos_t, sin_t = jnp.cos(ang_t), jnp.sin(ang_t)
    pad = jnp.zeros((LANES - half_r - half_n, tm), F32)
    cos_a = jnp.concatenate([cos_t, pad], axis=0).T
    sin_a = jnp.concatenate([sin_t, pad], axis=0).T
    cos_q, sin_q = cos_t[half_r:, :], sin_t[half_r:, :]
    first_r = in_head < half_r
    first_n = in_head < half_n
    rot_n = in_head < ROPE_DIM

    def spread_r(c):
        c = jnp.where(lane < half_r, c, pltpu.roll(c, half_r, 1))
        return jnp.where(low, c, pltpu.roll(c, HEAD_DIM, 1))

    def spread_n(c, rest):
        c = pltpu.roll(c, LANES - half_r, 1)
        c = jnp.where(lane < half_n, c, pltpu.roll(c, half_n, 1))
        c = jnp.where(low, c, pltpu.roll(c, HEAD_DIM, 1))
        return jnp.where(rot_n, c, rest)

    cos_r, sin_r = spread_r(cos_a), spread_r(sin_a) * tab[0:1, :]
    cos_n, sin_n = spread_n(cos_a, 1.0), spread_n(sin_a, 0.0) * tab[1:2, :]

    rope_r = lambda y: _rope(y, cos_r, sin_r, first_r, HEAD_DIM // 2)
    rope_n = lambda y: _rope(y, cos_n, sin_n, first_n, ROPE_DIM // 2)

    def halves(ci):
        y = _dot(xn, w_ref[:, ci * 256:(ci + 1) * 256])
        return y[:, :LANES], y[:, LANES:]

    for ci in range(2):
        for h, y in enumerate(halves(4 + ci)):
            c0 = 2 * RET_WIDTH + ci * 256 + h * LANES
            ret_ref[:, c0:c0 + LANES] = y.astype(BF16)
    t = _dot_nt(wt_ref[...], xn)
    for hd in range(NSA_Q_HEADS):
        b0 = hd * HEAD_DIM
        x1, x2 = t[b0:b0 + half_n, :], t[b0 + half_n:b0 + ROPE_DIM, :]
        head = jnp.concatenate([x1 * cos_q - x2 * sin_q, x1 * sin_q + x2 * cos_q,
                                t[b0 + ROPE_DIM:b0 + HEAD_DIM, :]], axis=0)
        qt_ref[b0:b0 + HEAD_DIM, :] = (head * (QK_SCALE * LOG2_E)).astype(BF16)
    nq, nv = qt_ref.shape[0], vt_ref.shape[0]
    vt_ref[...] = t[nq:nq + nv, :].astype(BF16)
    gt_ref[...] = jax.nn.sigmoid(t[nq + nv:, :])
    for ci in range(2):
        for h, y in enumerate(halves(ci)):
            c0 = ci * 256 + h * LANES
            ret_ref[:, c0:c0 + LANES] = rope_r(y).astype(BF16)
    for ci in range(2):
        for h, y in enumerate(halves(2 + ci)):
            c0 = RET_WIDTH + ci * 256 + h * LANES
            ret_ref[:, c0:c0 + LANES] = (rope_r(y) * QK_SCALE).astype(BF16)
    kc, vc = halves(8)
    for a, y in enumerate((rope_n(kc), vc)):
        stage_ref[2 * a] = y
        stage_ref[2 * a + 1] = pltpu.roll(y, HEAD_DIM, 1)
    n_grp = tm // CMP_STRIDE
    low_g = lax.broadcasted_iota(jnp.int32, (n_grp, LANES), 1) < HEAD_DIM
    for a in range(2):
        for j in range(CMP_STRIDE // 2):
            token = lambda l, s: stage_ref[2 * a + s, pl.ds(l, n_grp, stride=CMP_STRIDE), :]
            even, odd = 2 * j, 2 * j + 1
            cmp_ref[a, 0, :, j * LANES:(j + 1) * LANES] = jnp.where(
                low_g, token(even, 0), token(odd, 1)).astype(BF16)
            cmp_ref[a, 1, :, j * LANES:(j + 1) * LANES] = jnp.where(
                low_g, token(even, 1), token(odd, 0)).astype(BF16)
    for n, y in enumerate(halves(9)):
        y = rope_n(y)
        sw = pltpu.roll(y, HEAD_DIM, 1)
        kdup_ref[:, (2 * n) * LANES:(2 * n + 1) * LANES] = jnp.where(low, y, sw).astype(BF16)
        kdup_ref[:, (2 * n + 1) * LANES:(2 * n + 2) * LANES] = jnp.where(low, sw, y).astype(BF16)
    for ci in range(2):
        for h, y in enumerate(halves(6 + ci)):
            c0 = ci * 256 + h * LANES
            rg_ref[:, c0:c0 + LANES] = y


def _in_proj(x2, positions, norm_w, w_main, w_t, tab, freq, tm):
    T, D = x2.shape
    nv = 2 * KV_WIDTH
    ng = w_t.shape[0] - nv - NSA_WIDTH
    grp_w = CMP_STRIDE * HEAD_DIM
    pos_rows = positions.reshape(T // tm, 1, tm)
    row = lambda w: pl.BlockSpec((tm, w), lambda i: (i, 0))
    col = lambda h: pl.BlockSpec((h, tm), lambda i: (0, i))
    full = lambda a: pl.BlockSpec(a.shape, lambda i: (0,) * a.ndim)
    return pl.pallas_call(
        _in_proj_kernel,
        grid=(T // tm,),
        in_specs=[row(D), pl.BlockSpec((1, 1, tm), lambda i: (i, 0, 0)),
                  full(norm_w), full(w_main), full(w_t), full(tab), full(freq)],
        out_specs=[row(3 * RET_WIDTH), row(RET_WIDTH), col(NSA_WIDTH),
                   pl.BlockSpec((2, NSA_KV_HEADS, tm // CMP_STRIDE, grp_w), lambda i: (0, 0, i, 0)),
                   row(4 * LANES), col(nv), col(ng)],
        out_shape=[jax.ShapeDtypeStruct((T, 3 * RET_WIDTH), BF16),
                   jax.ShapeDtypeStruct((T, RET_WIDTH), F32),
                   jax.ShapeDtypeStruct((NSA_WIDTH, T), BF16),
                   jax.ShapeDtypeStruct((2, NSA_KV_HEADS, T // CMP_STRIDE, grp_w), BF16),
                   jax.ShapeDtypeStruct((T, 4 * LANES), BF16),
                   jax.ShapeDtypeStruct((nv, T), BF16),
                   jax.ShapeDtypeStruct((ng, T), F32)],
        scratch_shapes=[pltpu.VMEM((4, tm, LANES), F32)],
        compiler_params=_params("parallel"),
    )(x2, pos_rows, norm_w, w_main, w_t, tab, freq)


def _in_proj_weights(w_in):
    w_main = jnp.concatenate([w_in[:, :_NQ], w_in[:, _KC:_KSL], w_in[:, _KSL:_VSL],
                              w_in[:, _KWN:_VWN]], axis=1)
    gate_cols = np.full((NSA_KV_HEADS * GATE_ROWS,), -1)
    for g in range(NSA_KV_HEADS):
        for r in range(NSA_GROUP):
            for br in range(N_BRANCH):
                gate_cols[g * GATE_ROWS + r * N_BRANCH + br] = _NG + (g * NSA_GROUP + r) * N_BRANCH + br
    w_gate = jnp.where(gate_cols[None, :] >= 0, w_in[:, np.maximum(gate_cols, 0)], 0.0)
    w_t = jnp.concatenate([w_in[:, _NQ:_KC], w_in[:, _VSL:_KWN], w_in[:, _VWN:_NG], w_gate],
                          axis=1).T
    return w_main.astype(BF16), w_t.astype(BF16)


def _retention_kernel(q_ref, k_ref, v_ref, rg_ref, gnw_ref, dec_ref, qd_ref, kd_ref, cdm_ref,
                      o_ref, state_ref, *, n_chunks, unroll):
    C = RET_CHUNK
    lane = lax.broadcasted_iota(jnp.int32, (C, LANES), 1)
    low = lane < HEAD_DIM
    r_i = lax.broadcasted_iota(jnp.int32, (LANES, LANES), 0)
    c_i = lax.broadcasted_iota(jnp.int32, (LANES, LANES), 1)
    same_head = (r_i < HEAD_DIM) == (c_i < HEAD_DIM)
    state_ref[...] = jnp.zeros_like(state_ref)

    def group(gi, carry):
        rows = [pl.multiple_of((gi * unroll + u) * C, C) for u in range(unroll)]
        q = [q_ref[pl.ds(r0, C), :] for r0 in rows]
        k = [k_ref[pl.ds(r0, C), :] for r0 in rows]
        v = [v_ref[pl.ds(r0, C), :] for r0 in rows]
        zero = jnp.zeros_like(q[0])
        s0 = [_dot_nt(jnp.where(low, q[u], zero), k[u]) for u in range(unroll)]
        s1 = [_dot_nt(jnp.where(low, zero, q[u]), k[u]) for u in range(unroll)]
        kv = [_dot_tn((k[u].astype(F32) * kd_ref[...]).astype(BF16), v[u]) for u in range(unroll)]

        p0 = [(s0[u] * dec_ref[0]).astype(BF16) for u in range(unroll)]
        p1 = [(s1[u] * dec_ref[1]).astype(BF16) for u in range(unroll)]
        states = [state_ref[...]]
        for u in range(unroll):
            states.append(states[u] * cdm_ref[0] + jnp.where(same_head, kv[u], 0.0))
        state_ref[...] = states[unroll]

        o0 = [_dot(p0[u], v[u]) for u in range(unroll)]
        o1 = [_dot(p1[u], v[u]) for u in range(unroll)]
        oc = [_dot((q[u].astype(F32) * qd_ref[...]).astype(BF16), states[u].astype(BF16))
              for u in range(unroll)]

        inv_d = 1.0 / HEAD_DIM
        for u in range(unroll):
            ro = jnp.where(low, o0[u], o1[u]) + oc[u]
            sum0 = jnp.sum(jnp.where(low, ro, 0.0), axis=-1, keepdims=True)
            sum1 = jnp.sum(jnp.where(low, 0.0, ro), axis=-1, keepdims=True)
            d = ro - jnp.where(low, sum0, sum1) * inv_d
            dd = d * d
            var0 = jnp.sum(jnp.where(low, dd, 0.0), axis=-1, keepdims=True)
            var1 = jnp.sum(jnp.where(low, 0.0, dd), axis=-1, keepdims=True)
            var = jnp.where(low, var0, var1) * inv_d
            y = d * lax.rsqrt(var + EPS) * gnw_ref[...]
            g = rg_ref[pl.ds(rows[u], C), :]
            o_ref[pl.ds(rows[u], C), :] = (y * (g * jax.nn.sigmoid(g))).astype(BF16)
        return carry

    lax.fori_loop(0, n_chunks // unroll, group, 0)


def _retention(ret_qkv, rg, gn_w, dec, qd, kd, cdm, B, S):
    T = B * S
    n_pairs = RET_HEADS // HEADS_PER_VREG
    seq = lambda off: pl.BlockSpec((S, LANES), lambda b, hp: (b, off + hp))
    return pl.pallas_call(
        functools.partial(_retention_kernel, n_chunks=S // RET_CHUNK,
                          unroll=_row_tile(S // RET_CHUNK, 16)),
        grid=(B, n_pairs),
        in_specs=[seq(0), seq(n_pairs), seq(2 * n_pairs), seq(0),
                  pl.BlockSpec((1, LANES), lambda b, hp: (0, hp)),
                  pl.BlockSpec((HEADS_PER_VREG, RET_CHUNK, RET_CHUNK), lambda b, hp: (hp, 0, 0)),
                  pl.BlockSpec((RET_CHUNK, LANES), lambda b, hp: (0, hp)),
                  pl.BlockSpec((RET_CHUNK, LANES), lambda b, hp: (0, hp)),
                  pl.BlockSpec((1, LANES, LANES), lambda b, hp: (hp, 0, 0))],
        out_specs=seq(0),
        out_shape=jax.ShapeDtypeStruct((T, RET_WIDTH), BF16),
        scratch_shapes=[pltpu.VMEM((LANES, LANES), F32)],
        compiler_params=_params("parallel", "parallel"),
    )(ret_qkv, ret_qkv, ret_qkv, rg, gn_w, dec, qd, kd, cdm)


def _retention_constants():
    H, C = RET_HEADS, RET_CHUNK
    log_g = jnp.log1p(-jnp.exp2(-5.0 - jnp.arange(H, dtype=F32)))
    pos = jnp.arange(C, dtype=F32)
    diff = pos[:, None] - pos[None, :]
    dec = jnp.where(diff >= 0, jnp.exp(jnp.maximum(diff, 0.0) * log_g[:, None, None]), 0.0)
    q_decay = jnp.exp((pos + 1.0) * log_g[:, None]).T
    k_decay = jnp.exp((C - 1.0 - pos) * log_g[:, None]).T
    chunk_decay = jnp.exp(C * log_g)
    qd = jnp.repeat(q_decay, HEAD_DIM, axis=1)
    kd = jnp.repeat(k_decay, HEAD_DIM, axis=1)
    cd_rows = jnp.repeat(chunk_decay, HEAD_DIM).reshape(H // HEADS_PER_VREG, LANES, 1)
    head_of = jnp.arange(LANES) // HEAD_DIM
    same = (head_of[:, None] == head_of[None, :]).astype(F32)
    return dec, qd, kd, cd_rows * same[None]


def _compress_kernel(tk_ref, tv_ref, pos_ref, w1_ref, w2k_ref, w2vt_ref, ck_ref, cvt_ref):
    pos = jnp.broadcast_to(pos_ref[...], (8, pos_ref.shape[1])).astype(BF16)

    def hidden(t, w1_ref_a):
        half = t.shape[1]
        bias = _dot(pos, w1_ref_a[...])[0:1, :]
        first = _dot(t, w1_ref_a[:half, :])
        second = _dot(t, w1_ref_a[half:, :])
        n = t.shape[0]
        h = first + pltpu.roll(second, n - 1, 0) + bias
        return jax.nn.gelu(h).astype(BF16)

    ck_ref[0, 0] = _dot(hidden(tk_ref[0, 0], w1_ref.at[0]), w2k_ref[...]).astype(BF16)
    cvt_ref[0, 0] = _dot_nt(w2vt_ref[...], hidden(tv_ref[0, 0], w1_ref.at[1])).astype(BF16)


def _compress(cmp_tok, pos_flat, w1, w2k, w2vt, B):
    _, G, rows, width = cmp_tok.shape
    n = rows // B
    tok = lambda a: pl.BlockSpec((1, 1, n, width), lambda b, g: (a, g, b, 0))
    full = lambda a: pl.BlockSpec(a.shape, lambda b, g: (0,) * a.ndim)
    return pl.pallas_call(
        _compress_kernel,
        grid=(B, G),
        in_specs=[tok(0), tok(1), full(pos_flat), full(w1), full(w2k), full(w2vt)],
        out_specs=[pl.BlockSpec((1, 1, n, LANES), lambda b, g: (b, g, 0, 0)),
                   pl.BlockSpec((1, 1, HEAD_DIM, n), lambda b, g: (b, g, 0, 0))],
        out_shape=[jax.ShapeDtypeStruct((B, G, n, LANES), BF16),
                   jax.ShapeDtypeStruct((B, G, HEAD_DIM, n), BF16)],
        compiler_params=_params("parallel", "parallel"),
    )(cmp_tok, cmp_tok, pos_flat, w1, w2k, w2vt)


def _nsa_kernel(q_ref, ck_ref, cvt_ref, ks_ref, kw_ref, vst_ref, vwt_ref, gt_ref, ovt_ref, eb_ref,
                o_ref, m_ref, acc_ref, mw_ref, accw_ref, out_ref, score_ref, ahead_ref,
                sa_ref, sb_ref, sw_ref,
                *, tq, top):
    R = NSA_GROUP
    D = HEAD_DIM
    t0 = pl.multiple_of(pl.program_id(2) * tq, tq)
    t_off = lax.broadcasted_iota(jnp.int32, (1, tq), 1)
    tcol = t0 + t_off
    k_off = lax.broadcasted_iota(jnp.int32, (tq, 1), 0)
    low = lax.broadcasted_iota(jnp.int32, (tq, LANES), 1) < D
    ones_rows = jnp.ones((BF16_ROWS, tq), BF16)
    gate = lambda r, br: gt_ref[r * N_BRANCH + br:r * N_BRANCH + br + 1, :]

    def q_aug(r, extra):
        return jnp.concatenate([q_ref[r * D:(r + 1) * D, :], extra], axis=0)

    zero_rows = jnp.zeros((LANES - D, tq), BF16)

    def value_rows(vt_ref_, start):
        return jnp.concatenate([vt_ref_[:, pl.ds(start, tq)], ones_rows], axis=0)

    sel_state, win_state = (m_ref, acc_ref), (mw_ref, accw_ref)

    def reset(state):
        for r in range(R):
            state[0][r] = jnp.full((1, tq), LOWEST, F32)
            state[1][r] = jnp.zeros(acc_ref.shape[1:], F32)

    def more(state, r, s_t, vrows):
        m_old = state[0][r]
        m_new = jnp.maximum(m_old, jnp.max(s_t, axis=0, keepdims=True))
        state[0][r] = m_new
        state[1][r] = (jnp.exp2(m_old - m_new) * state[1][r]
                       + _dot(vrows, jnp.exp2(s_t - m_new).astype(BF16)))

    def mix(state, r, br):
        o = state[1][r, :D, :] * ((1.0 / state[1][r, D:D + 1, :]) * gate(r, br))
        out_ref[r * D:(r + 1) * D, :] = out_ref[r * D:(r + 1) * D, :] + o

    ck = ck_ref[0, 0]
    ncp = ck.shape[0]
    cvt = jnp.concatenate([cvt_ref[0, 0], jnp.ones((BF16_ROWS, ncp), BF16)], axis=0)
    cmp_end = lax.broadcasted_iota(jnp.int32, (ncp, 1), 0) * CMP_STRIDE + (CMP_BLOCK - 1)
    cmask = cmp_end <= tcol
    sees_any = jnp.where(tcol >= CMP_BLOCK - 1, 1.0, 0.0)
    n_sel = ovt_ref.shape[0]
    imp_t = jnp.zeros((n_sel, tq), F32)
    scores = [_dot(ck, q_aug(r, zero_rows)) for r in range(R)]

    n_back = WINDOW // tq
    win_starts = [pl.multiple_of(jnp.maximum(t0 - j * tq, 0), tq) for j in range(n_back + 1)]
    flag_lane = lax.broadcasted_iota(jnp.int32, (tq, LANES), 1) == D
    flag_row = lax.broadcasted_iota(jnp.int32, (LANES - D, tq), 0) == 0
    qw = [q_aug(r, jnp.where(flag_row, 1.0, 0.0).astype(BF16)) for r in range(R)]

    def issue_window(j):
        off = jnp.full((tq, LANES), jnp.where(t0 >= j * tq, 0.0, MASKED), F32).astype(BF16)
        kw = jnp.where(flag_lane, off, kw_ref[0, pl.ds(win_starts[j], tq), :])
        for r in range(R):
            sw_ref[j, r] = _dot(kw, qw[r])

    for r in range(R):
        for j in range(r, n_back + 1, R):
            issue_window(j)
        s = jnp.where(cmask, scores[r], MASKED)
        e = jnp.exp2(s - jnp.max(s, axis=0, keepdims=True)).astype(BF16)
        acc = _dot(cvt, e)
        inv_l = sees_any * (1.0 / acc[D:D + 1, :])
        out_ref[r * D:(r + 1) * D, :] = acc[:D, :] * (inv_l * gate(r, 0))
        imp_t = imp_t + _dot(ovt_ref[...], e) * inv_l

    blk_i = lax.broadcasted_iota(jnp.int32, (n_sel, tq), 0)
    cur = tcol // SEL_BLOCK
    valid = blk_i * SEL_BLOCK <= tcol
    forced = (blk_i == 0) | (blk_i == cur) | (blk_i == cur - 1)
    score_ref[...] = jnp.where(forced, jnp.inf, jnp.where(valid, imp_t, -jnp.inf))
    ahead_ref[...] = jnp.zeros(ahead_ref.shape, F32)
    live_blocks = (t0 + tq) // SEL_BLOCK

    def count(jg, kb_lo, kb_hi):
        mine = score_ref[jg * 8:(jg + 1) * 8, :]
        blk_g = jg * 8 + lax.broadcasted_iota(jnp.int32, (8, tq), 0)
        ahead = ahead_ref[jg * 8:(jg + 1) * 8, :]
        for kb in range(kb_lo, kb_hi):
            other = score_ref[kb:kb + 1, :]
            if kb < jg * 8:
                ahead = ahead + jnp.where(other >= mine, 1.0, 0.0)
            elif kb >= (jg + 1) * 8:
                ahead = ahead + jnp.where(other > mine, 1.0, 0.0)
            else:
                ahead = ahead + jnp.where(blk_g > kb, jnp.where(other >= mine, 1.0, 0.0),
                                          jnp.where(other > mine, 1.0, 0.0))
        ahead_ref[jg * 8:(jg + 1) * 8, :] = ahead

    def rank_level(lo, hi):
        for jg in range(lo // 8):
            count(jg, lo, hi)
        for jg in range(lo // 8, hi // 8):
            count(jg, 0, hi)

    step = RANK_STEP if n_sel % RANK_STEP == 0 else n_sel
    rank_level(0, step)
    for lo in range(step, n_sel, step):
        @pl.when(live_blocks > lo)
        def _():
            rank_level(lo, lo + step)

    pad_rows = LANES - D - n_sel
    not_sel = jnp.where(ahead_ref[...] < top, 0.0, 1.0)
    if pad_rows:
        not_sel = jnp.concatenate([not_sel, jnp.zeros((pad_rows, tq), F32)], axis=0)
    not_sel = not_sel.astype(BF16)
    qa = [q_aug(r, not_sel) for r in range(R)]

    def keys_aug(start):
        return jnp.where(low, ks_ref[0, pl.ds(start, tq), :], eb_ref[pl.ds(start, tq), :])

    def issue(c, buf_ref):
        ka = keys_aug(pl.multiple_of(c * tq, tq))
        for r in range(R):
            buf_ref[r] = _dot(ka, qa[r])

    def consume(c, buf_ref, own):
        vr = value_rows(vst_ref, pl.multiple_of(c * tq, tq))
        for r in range(R):
            more(sel_state, r, jnp.where(causal, buf_ref[r], MASKED) if own else buf_ref[r], vr)

    causal = k_off <= t_off
    n_plain = pl.program_id(2)
    reset(sel_state)
    reset(win_state)
    issue(0, sa_ref)
    for j in range(n_back + 1):
        vr = value_rows(vwt_ref, win_starts[j])
        for r in range(R):
            s_t = sw_ref[j, r]
            if j == 0:
                s_t = jnp.where(causal, s_t, MASKED)
            elif j == n_back:
                s_t = jnp.where(k_off > t_off, s_t, MASKED)
            more(win_state, r, s_t, vr)
    bufs = (sa_ref, sb_ref)

    def sel_trip(k, carry):
        for u in range(SEL_UNROLL):
            c = SEL_UNROLL * k + u
            issue(c + 1, bufs[(u + 1) % 2])
            consume(c, bufs[u % 2], False)
        return carry

    lax.fori_loop(0, n_plain // SEL_UNROLL, sel_trip, 0)

    def finish():
        for r in range(R):
            mix(sel_state, r, 1)
            mix(win_state, r, 2)
        o_ref[...] = out_ref[...].T.astype(BF16)

    for rem in range(SEL_UNROLL):
        @pl.when(n_plain % SEL_UNROLL == rem)
        def _():
            for u in range(rem):
                c = n_plain - rem + u
                issue(c + 1, bufs[(u + 1) % 2])
                consume(c, bufs[u % 2], False)
            consume(n_plain, bufs[rem % 2], True)
            finish()


def _nsa(nq, ck, cvt, kdup, vt, gt, ovt, eb, B, S, tq):
    T = B * S
    G = NSA_KV_HEADS
    assert S % tq == 0 and WINDOW % tq == 0 and S // SEL_BLOCK <= LANES - HEAD_DIM
    nqt = S // tq
    ncp = ck.shape[2]
    kdup3 = kdup.reshape(B, S, kdup.shape[1])
    acc_rows = HEAD_DIM + BF16_ROWS
    keys = lambda off: pl.BlockSpec((1, S, LANES), lambda b, g, i: (b, 0, off + g))
    vals = lambda off: pl.BlockSpec((HEAD_DIM, S), lambda b, g, i: (off + g, b))
    full = lambda a: pl.BlockSpec(a.shape, lambda b, g, i: (0,) * a.ndim)
    return pl.pallas_call(
        functools.partial(_nsa_kernel, tq=tq, top=min(SEL_TOPK, S // SEL_BLOCK)),
        grid=(B, G, nqt),
        in_specs=[pl.BlockSpec((NSA_GROUP * HEAD_DIM, tq), lambda b, g, i: (g, b * nqt + i)),
                  pl.BlockSpec((1, 1, ncp, LANES), lambda b, g, i: (b, g, 0, 0)),
                  pl.BlockSpec((1, 1, HEAD_DIM, ncp), lambda b, g, i: (b, g, 0, 0)),
                  keys(0), keys(G), vals(0), vals(G),
                  pl.BlockSpec((GATE_ROWS, tq), lambda b, g, i: (g, b * nqt + i)),
                  full(ovt), full(eb)],
        out_specs=pl.BlockSpec((tq, NSA_GROUP * HEAD_DIM), lambda b, g, i: (b * nqt + i, g)),
        out_shape=jax.ShapeDtypeStruct((T, NSA_WIDTH), BF16),
        scratch_shapes=[pltpu.VMEM((NSA_GROUP, 1, tq), F32),
                        pltpu.VMEM((NSA_GROUP, acc_rows, tq), F32),
                        pltpu.VMEM((NSA_GROUP, 1, tq), F32),
                        pltpu.VMEM((NSA_GROUP, acc_rows, tq), F32),
                        pltpu.VMEM((NSA_GROUP * HEAD_DIM, tq), F32),
                        pltpu.VMEM((S // SEL_BLOCK, tq), F32),
                        pltpu.VMEM((S // SEL_BLOCK, tq), F32),
                        pltpu.VMEM((NSA_GROUP, tq, tq), F32),
                        pltpu.VMEM((NSA_GROUP, tq, tq), F32),
                        pltpu.VMEM((WINDOW // tq + 1, NSA_GROUP, tq, tq), F32)],
        compiler_params=_params("parallel", "parallel", "parallel"),
    )(nq, ck, cvt, kdup3, kdup3, vt, vt, gt, ovt, eb)


def _nsa_constants(S):
    n_cmp = (S - CMP_BLOCK) // CMP_STRIDE + 1
    ncp = S // CMP_STRIDE
    n_sel = S // SEL_BLOCK
    ci = np.arange(ncp)[None, :]
    sj = np.arange(n_sel)[:, None]
    ovt = ((ci * CMP_STRIDE < (sj + 1) * SEL_BLOCK)
           & (ci * CMP_STRIDE + CMP_BLOCK > sj * SEL_BLOCK) & (ci < n_cmp))
    eb = np.zeros((S, LANES), np.float32)
    eb[np.arange(S), HEAD_DIM + np.arange(S) // SEL_BLOCK] = MASKED
    return jnp.asarray(ovt, BF16), jnp.asarray(eb, BF16)


def _mlp_kernel(ret_ref, nsa_ref, x_ref, p_ref, wo_ref, nw_ref, wg_ref, wv_ref, cw_ref, cb_ref,
                wd_ref, pg_ref, pw_ref, fw_ref, o_ref, h_ref, act_ref, tail_ref, *, fc, final):
    @pl.when(pl.program_id(1) == 0)
    def _():
        tail_ref[...] = jnp.zeros_like(tail_ref)

    mix = _dot(ret_ref[...], wo_ref[:RET_WIDTH, :]) + _dot(nsa_ref[...], wo_ref[RET_WIDTH:, :])
    h = x_ref[...] + mix
    h_ref[...] = h
    ms = jnp.mean(h * h, axis=-1, keepdims=True)
    hn = (h * lax.rsqrt(ms + EPS) * nw_ref[...]).astype(BF16)

    tm = hn.shape[0]
    row = lax.broadcasted_iota(jnp.int32, (tm, fc), 0)
    for c in range(wg_ref.shape[1] // fc):
        sl = slice(c * fc, (c + 1) * fc)
        g = _dot(hn, wg_ref[:, sl])
        val = _dot(hn, wv_ref[:, sl])
        tail = tail_ref[:, sl]
        g1 = jnp.where(row == 0, tail[7:8, :], pltpu.roll(g, 1, 0))
        g2 = jnp.where(row == 0, tail[6:7, :],
                       jnp.where(row == 1, tail[7:8, :], pltpu.roll(g, 2, 0)))
        tail_ref[:, sl] = g[tm - 8:, :]
        cw = cw_ref[:, sl]
        y = cb_ref[:, sl] + cw[0:1, :] * g2 + cw[1:2, :] * g1 + cw[2:3, :] * g
        act_ref[:, sl] = (y * jax.nn.sigmoid(y) * val).astype(BF16)

    h = h_ref[...] + _dot(act_ref[...], wd_ref[...])
    gate = jax.nn.sigmoid(_dot(h.astype(BF16), pg_ref[...]))
    h = h + gate * _dot(p_ref[...].astype(BF16), pw_ref[...])
    if final:
        ms = jnp.mean(h * h, axis=-1, keepdims=True)
        h = h * lax.rsqrt(ms + EPS) * fw_ref[...]
    o_ref[...] = h


def _mlp(ret_out, nsa_out, x2, p2, w_out, norm_w, wg, wv, conv_w, conv_b, wd, pg, pw, fw,
         B, S, tm, final):
    T, D = x2.shape
    F = wg.shape[1]
    nt = S // tm
    row = lambda w: pl.BlockSpec((tm, w), lambda b, j: (b * nt + j, 0))
    full = lambda a: pl.BlockSpec(a.shape, lambda b, j: (0,) * a.ndim, pipeline_mode=pl.Buffered(1))
    weights = (w_out, norm_w, wg, wv, conv_w, conv_b, wd, pg, pw, fw)
    return pl.pallas_call(
        functools.partial(_mlp_kernel, fc=256, final=final),
        grid=(B, nt),
        in_specs=[row(RET_WIDTH), row(NSA_WIDTH), row(D), row(p2.shape[1])]
                 + [full(w) for w in weights],
        out_specs=row(D),
        out_shape=jax.ShapeDtypeStruct((T, D), F32),
        scratch_shapes=[pltpu.VMEM((tm, D), F32), pltpu.VMEM((tm, F), BF16),
                        pltpu.VMEM((8, F), F32)],
        compiler_params=_params("arbitrary", "arbitrary"),
    )(ret_out, nsa_out, x2, p2, *weights)


def _rope_tables(tm):
    lane = np.arange(LANES) % HEAD_DIM
    half_r = HEAD_DIM // 2
    inv_r = jnp.power(jnp.float32(RET_THETA), -jnp.arange(half_r, dtype=F32) / half_r)
    half_n = ROPE_DIM // 2
    inv_n = jnp.power(jnp.float32(ROPE_THETA), -jnp.arange(half_n, dtype=F32) / half_n)
    signs = [jnp.asarray(np.where(lane < half_r, -1.0, 1.0), F32),
             jnp.asarray(np.where(lane < half_n, -1.0, 1.0), F32)]
    table = jnp.concatenate([jnp.stack(signs), jnp.zeros((6, LANES), F32)], axis=0)
    freq = jnp.concatenate([inv_r, inv_n])
    return table, jnp.broadcast_to(freq[:, None], (half_r + half_n, tm))


def _layer(h2, p2, pos2, B, S, final, norm_mix_w, w_in, ret_gn_w, cmp_pos, cmp_k_w1, cmp_k_w2,
           cmp_v_w1, cmp_v_w2, w_out, norm_ffn_w, ffn_w_up, ffn_conv_w, ffn_conv_b, ffn_w_down,
           ple_w, ple_gate_w, final_norm_w):
    T, D = h2.shape
    tm = _row_tile(S, 512)

    w_main, w_t = _in_proj_weights(w_in)
    ret_qkv, rg, qt, cmp_tok, kdup, vt, gt = _in_proj(
        h2, pos2, norm_mix_w.reshape(1, D), w_main, w_t, *_rope_tables(tm), tm)

    dec, qd, kd, cdm = _retention_constants()
    ret_out = _retention(ret_qkv, rg, ret_gn_w.reshape(1, RET_WIDTH), dec, qd, kd, cdm, B, S)

    w1 = jnp.stack([cmp_k_w1, cmp_v_w1]).astype(BF16)
    w2k = jnp.tile(cmp_k_w2, (1, HEADS_PER_VREG)).astype(BF16)
    ck, cvt = _compress(cmp_tok, cmp_pos.reshape(1, CMP_BLOCK * HEAD_DIM), w1, w2k,
                        cmp_v_w2.T.astype(BF16), B)

    ovt, eb = _nsa_constants(S)
    nsa_out = _nsa(qt, ck, cvt, kdup, vt, gt, ovt, eb, B, S, tq=256)

    d_ff = ffn_w_down.shape[0]
    conv_w = jnp.pad(ffn_conv_w, ((0, 8 - ffn_conv_w.shape[0]), (0, 0)))
    return _mlp(ret_out, nsa_out, h2, p2, w_out.astype(BF16), norm_ffn_w.reshape(1, D),
                ffn_w_up[:, :d_ff].astype(BF16), ffn_w_up[:, d_ff:].astype(BF16),
                conv_w, ffn_conv_b.reshape(1, d_ff), ffn_w_down.astype(BF16),
                ple_gate_w.astype(BF16), ple_w.astype(BF16), final_norm_w.reshape(1, D),
                B, S, tm, final)


def kernel(x, p, positions, norm_mix_w, w_in, ret_gn_w, cmp_pos, cmp_k_w1, cmp_k_w2, cmp_v_w1, cmp_v_w2, w_out, norm_ffn_w, ffn_w_up, ffn_conv_w, ffn_conv_b, ffn_w_down, ple_w, ple_gate_w, final_norm_w):
    B, S, D = x.shape
    T = B * S
    depth = w_in.shape[0]
    h = x.reshape(T, D)
    pos2 = positions.reshape(T).astype(jnp.int32)
    for i in range(depth):
        h = _layer(h, p[i].reshape(T, -1), pos2, B, S, i == depth - 1, norm_mix_w[i], w_in[i],
                   ret_gn_w[i], cmp_pos[i], cmp_k_w1[i], cmp_k_w2[i], cmp_v_w1[i], cmp_v_w2[i],
                   w_out[i], norm_ffn_w[i], ffn_w_up[i], ffn_conv_w[i], ffn_conv_b[i],
                   ffn_w_down[i], ple_w[i], ple_gate_w[i], final_norm_w)
    return h.reshape(B, S, D)
```

```python
import functools

import numpy as np
import jax
import jax.numpy as jnp
from jax import lax
from jax.experimental import pallas as pl
from jax.experimental.pallas import tpu as pltpu

F32 = jnp.float32
BF16 = jnp.bfloat16

LANES = 128
BF16_ROWS = 16
HEAD_DIM = 64
HEADS_PER_VREG = LANES // HEAD_DIM
RET_HEADS = 8
NSA_Q_HEADS = 8
NSA_KV_HEADS = 2
NSA_GROUP = NSA_Q_HEADS // NSA_KV_HEADS
RET_WIDTH = RET_HEADS * HEAD_DIM
NSA_WIDTH = NSA_Q_HEADS * HEAD_DIM
KV_WIDTH = NSA_KV_HEADS * HEAD_DIM
RET_CHUNK = 128
RET_THETA = 10000.0
ROPE_THETA = 500000.0
ROPE_DIM = HEAD_DIM // 4
CMP_BLOCK = 32
CMP_STRIDE = 16
SEL_BLOCK = 64
SEL_TOPK = 16
WINDOW = 512
N_BRANCH = 3
GATE_ROWS = 16
RANK_STEP = 8
SEL_UNROLL = 8
EPS = 1e-6
QK_SCALE = HEAD_DIM ** -0.5
LOG2_E = 1.4426950408889634
MASKED = -1e30
LOWEST = -3e38

_OFF = np.cumsum([0, RET_WIDTH, RET_WIDTH, RET_WIDTH, RET_WIDTH, NSA_WIDTH,
                  KV_WIDTH, KV_WIDTH, KV_WIDTH, KV_WIDTH, KV_WIDTH, KV_WIDTH])
(_RQ, _RK, _RV, _RG, _NQ, _KC, _VC, _KSL, _VSL, _KWN, _VWN, _NG) = (int(v) for v in _OFF)

VMEM_LIMIT = 56 * 1024 * 1024


def _dot(a, b):
    return jnp.dot(a, b, preferred_element_type=F32)


def _dot_nt(a, b):
    return lax.dot_general(a, b, (((1,), (1,)), ((), ())), preferred_element_type=F32)


def _dot_tn(a, b):
    return lax.dot_general(a, b, (((0,), (0,)), ((), ())), preferred_element_type=F32)


def _params(*semantics):
    return pltpu.CompilerParams(dimension_semantics=semantics, vmem_limit_bytes=VMEM_LIMIT)


def _row_tile(n, pref):
    t = pref
    while n % t:
        t //= 2
    return t


def _rope(y, cos, sin_signed, first_half, half):
    nxt = pltpu.roll(y, LANES - half, 1)
    prv = pltpu.roll(y, half, 1)
    return y * cos + jnp.where(first_half, nxt, prv) * sin_signed


def _in_proj_kernel(x_ref, posrow_ref, nw_ref, w_ref, wt_ref, tab_ref, freq_ref,
                    ret_ref, rg_ref, qt_ref, cmp_ref, kdup_ref, vt_ref, gt_ref, stage_ref):
    x = x_ref[...]
    tm = x.shape[0]
    ms = jnp.mean(x * x, axis=-1, keepdims=True)
    xn = (x * lax.rsqrt(ms + EPS) * nw_ref[...]).astype(BF16)

    tab = tab_ref[...]
    lane = lax.broadcasted_iota(jnp.int32, (tm, LANES), 1)
    in_head = lane % HEAD_DIM
    low = lane < HEAD_DIM
    half_r, half_n = HEAD_DIM // 2, ROPE_DIM // 2
    ang_t = freq_ref[...] * posrow_ref[0].astype(F32)
    cos_t, sin_t = jnp.cos(ang_t), jnp.sin(ang_t)
    pad = jnp.zeros((LANES - half_r - half_n, tm), F32)
    cos_a = jnp.concatenate([cos_t, pad], axis=0).T
    sin_a = jnp.concatenate([sin_t, pad], axis=0).T
    cos_q, sin_q = cos_t[half_r:, :], sin_t[half_r:, :]
    first_r = in_head < half_r
    first_n = in_head < half_n
    rot_n = in_head < ROPE_DIM

    def spread_r(c):
        c = jnp.where(lane < half_r, c, pltpu.roll(c, half_r, 1))
        return jnp.where(low, c, pltpu.roll(c, HEAD_DIM, 1))

    def spread_n(c, rest):
        c = pltpu.roll(c, LANES - half_r, 1)
        c = jnp.where(lane < half_n, c, pltpu.roll(c, half_n, 1))
        c = jnp.where(low, c, pltpu.roll(c, HEAD_DIM, 1))
        return jnp.where(rot_n, c, rest)

    cos_r, sin_r = spread_r(cos_a), spread_r(sin_a) * tab[0:1, :]
    cos_n, sin_n = spread_n(cos_a, 1.0), spread_n(sin_a, 0.0) * tab[1:2, :]

    rope_r = lambda y: _rope(y, cos_r, sin_r, first_r, HEAD_DIM // 2)
    rope_n = lambda y: _rope(y, cos_n, sin_n, first_n, ROPE_DIM // 2)

    def halves(ci):
        y = _dot(xn, w_ref[:, ci * 256:(ci + 1) * 256])
        return y[:, :LANES], y[:, LANES:]

    for ci in range(2):
        for h, y in enumerate(halves(4 + ci)):
            c0 = 2 * RET_WIDTH + ci * 256 + h * LANES
            ret_ref[:, c0:c0 + LANES] = y.astype(BF16)
    t = _dot_nt(wt_ref[...], xn)
    for hd in range(NSA_Q_HEADS):
        b0 = hd * HEAD_DIM
        x1, x2 = t[b0:b0 + half_n, :], t[b0 + half_n:b0 + ROPE_DIM, :]
        head = jnp.concatenate([x1 * cos_q - x2 * sin_q, x1 * sin_q + x2 * cos_q,
                                t[b0 + ROPE_DIM:b0 + HEAD_DIM, :]], axis=0)
        qt_ref[b0:b0 + HEAD_DIM, :] = (head * (QK_SCALE * LOG2_E)).astype(BF16)
    nq, nv = qt_ref.shape[0], vt_ref.shape[0]
    vt_ref[...] = t[nq:nq + nv, :].astype(BF16)
    gt_ref[...] = jax.nn.sigmoid(t[nq + nv:, :])
    for ci in range(2):
        for h, y in enumerate(halves(ci)):
            c0 = ci * 256 + h * LANES
            ret_ref[:, c0:c0 + LANES] = rope_r(y).astype(BF16)
    for ci in range(2):
        for h, y in enumerate(halves(2 + ci)):
            c0 = RET_WIDTH + ci * 256 + h * LANES
            ret_ref[:, c0:c0 + LANES] = (rope_r(y) * QK_SCALE).astype(BF16)
    kc, vc = halves(8)
    for a, y in enumerate((rope_n(kc), vc)):
        stage_ref[2 * a] = y
        stage_ref[2 * a + 1] = pltpu.roll(y, HEAD_DIM, 1)
    n_grp = tm // CMP_STRIDE
    low_g = lax.broadcasted_iota(jnp.int32, (n_grp, LANES), 1) < HEAD_DIM
    for a in range(2):
        for j in range(CMP_STRIDE // 2):
            token = lambda l, s: stage_ref[2 * a + s, pl.ds(l, n_grp, stride=CMP_STRIDE), :]
            even, odd = 2 * j, 2 * j + 1
            cmp_ref[a, 0, :, j * LANES:(j + 1) * LANES] = jnp.where(
                low_g, token(even, 0), token(odd, 1)).astype(BF16)
            cmp_ref[a, 1, :, j * LANES:(j + 1) * LANES] = jnp.where(
                low_g, token(even, 1), token(odd, 0)).astype(BF16)
    for n, y in enumerate(halves(9)):
        y = rope_n(y)
        sw = pltpu.roll(y, HEAD_DIM, 1)
        kdup_ref[:, (2 * n) * LANES:(2 * n + 1) * LANES] = jnp.where(low, y, sw).astype(BF16)
        kdup_ref[:, (2 * n + 1) * LANES:(2 * n + 2) * LANES] = jnp.where(low, sw, y).astype(BF16)
    for ci in range(2):
        for h, y in enumerate(halves(6 + ci)):
            c0 = ci * 256 + h * LANES
            rg_ref[:, c0:c0 + LANES] = y


def _in_proj(x2, positions, norm_w, w_main, w_t, tab, freq, tm):
    T, D = x2.shape
    nv = 2 * KV_WIDTH
    ng = w_t.shape[0] - nv - NSA_WIDTH
    grp_w = CMP_STRIDE * HEAD_DIM
    pos_rows = positions.reshape(T // tm, 1, tm)
    row = lambda w: pl.BlockSpec((tm, w), lambda i: (i, 0))
    col = lambda h: pl.BlockSpec((h, tm), lambda i: (0, i))
    full = lambda a: pl.BlockSpec(a.shape, lambda i: (0,) * a.ndim)
    return pl.pallas_call(
        _in_proj_kernel,
        grid=(T // tm,),
        in_specs=[row(D), pl.BlockSpec((1, 1, tm), lambda i: (i, 0, 0)),
                  full(norm_w), full(w_main), full(w_t), full(tab), full(freq)],
        out_specs=[row(3 * RET_WIDTH), row(RET_WIDTH), col(NSA_WIDTH),
                   pl.BlockSpec((2, NSA_KV_HEADS, tm // CMP_STRIDE, grp_w), lambda i: (0, 0, i, 0)),
                   row(4 * LANES), col(nv), col(ng)],
        out_shape=[jax.ShapeDtypeStruct((T, 3 * RET_WIDTH), BF16),
                   jax.ShapeDtypeStruct((T, RET_WIDTH), F32),
                   jax.ShapeDtypeStruct((NSA_WIDTH, T), BF16),
                   jax.ShapeDtypeStruct((2, NSA_KV_HEADS, T // CMP_STRIDE, grp_w), BF16),
                   jax.ShapeDtypeStruct((T, 4 * LANES), BF16),
                   jax.ShapeDtypeStruct((nv, T), BF16),
                   jax.ShapeDtypeStruct((ng, T), F32)],
        scratch_shapes=[pltpu.VMEM((4, tm, LANES), F32)],
        compiler_params=_params("parallel"),
    )(x2, pos_rows, norm_w, w_main, w_t, tab, freq)


def _in_proj_weights(w_in):
    w_main = jnp.concatenate([w_in[:, :_NQ], w_in[:, _KC:_KSL], w_in[:, _KSL:_VSL],
                              w_in[:, _KWN:_VWN]], axis=1)
    gate_cols = np.full((NSA_KV_HEADS * GATE_ROWS,), -1)
    for g in range(NSA_KV_HEADS):
        for r in range(NSA_GROUP):
            for br in range(N_BRANCH):
                gate_cols[g * GATE_ROWS + r * N_BRANCH + br] = _NG + (g * NSA_GROUP + r) * N_BRANCH + br
    w_gate = jnp.where(gate_cols[None, :] >= 0, w_in[:, np.maximum(gate_cols, 0)], 0.0)
    w_t = jnp.concatenate([w_in[:, _NQ:_KC], w_in[:, _VSL:_KWN], w_in[:, _VWN:_NG], w_gate],
                          axis=1).T
    return w_main.astype(BF16), w_t.astype(BF16)


def _retention_kernel(q_ref, k_ref, v_ref, rg_ref, gnw_ref, dec_ref, qd_ref, kd_ref, cdm_ref,
                      o_ref, state_ref, *, n_chunks, unroll):
    C = RET_CHUNK
    lane = lax.broadcasted_iota(jnp.int32, (C, LANES), 1)
    low = lane < HEAD_DIM
    r_i = lax.broadcasted_iota(jnp.int32, (LANES, LANES), 0)
    c_i = lax.broadcasted_iota(jnp.int32, (LANES, LANES), 1)
    same_head = (r_i < HEAD_DIM) == (c_i < HEAD_DIM)
    state_ref[...] = jnp.zeros_like(state_ref)

    def group(gi, carry):
        rows = [pl.multiple_of((gi * unroll + u) * C, C) for u in range(unroll)]
        q = [q_ref[pl.ds(r0, C), :] for r0 in rows]
        k = [k_ref[pl.ds(r0, C), :] for r0 in rows]
        v = [v_ref[pl.ds(r0, C), :] for r0 in rows]
        zero = jnp.zeros_like(q[0])
        s0 = [_dot_nt(jnp.where(low, q[u], zero), k[u]) for u in range(unroll)]
        s1 = [_dot_nt(jnp.where(low, zero, q[u]), k[u]) for u in range(unroll)]
        kv = [_dot_tn((k[u].astype(F32) * kd_ref[...]).astype(BF16), v[u]) for u in range(unroll)]

        p0 = [(s0[u] * dec_ref[0]).astype(BF16) for u in range(unroll)]
        p1 = [(s1[u] * dec_ref[1]).astype(BF16) for u in range(unroll)]
        states = [state_ref[...]]
        for u in range(unroll):
            states.append(states[u] * cdm_ref[0] + jnp.where(same_head, kv[u], 0.0))
        state_ref[...] = states[unroll]

        o0 = [_dot(p0[u], v[u]) for u in range(unroll)]
        o1 = [_dot(p1[u], v[u]) for u in range(unroll)]
        oc = [_dot((q[u].astype(F32) * qd_ref[...]).astype(BF16), states[u].astype(BF16))
              for u in range(unroll)]

        avg = jnp.where(same_head, 1.0 / HEAD_DIM, 0.0).astype(BF16)

        def head_mean(x):
            hi = x.astype(BF16)
            lo = (x - hi.astype(F32)).astype(BF16)
            return _dot(hi, avg) + _dot(lo, avg)

        for u in range(unroll):
            ro = jnp.where(low, o0[u], o1[u]) + oc[u]
            d = ro - head_mean(ro)
            var = head_mean(d * d)
            y = d * lax.rsqrt(var + EPS) * gnw_ref[...]
            g = rg_ref[pl.ds(rows[u], C), :]
            o_ref[pl.ds(rows[u], C), :] = (y * (g * jax.nn.sigmoid(g))).astype(BF16)
        return carry

    lax.fori_loop(0, n_chunks // unroll, group, 0)


def _retention(ret_qkv, rg, gn_w, dec, qd, kd, cdm, B, S):
    T = B * S
    n_pairs = RET_HEADS // HEADS_PER_VREG
    seq = lambda off: pl.BlockSpec((S, LANES), lambda b, hp: (b, off + hp))
    return pl.pallas_call(
        functools.partial(_retention_kernel, n_chunks=S // RET_CHUNK,
                          unroll=_row_tile(S // RET_CHUNK, 16)),
        grid=(B, n_pairs),
        in_specs=[seq(0), seq(n_pairs), seq(2 * n_pairs), seq(0),
                  pl.BlockSpec((1, LANES), lambda b, hp: (0, hp)),
                  pl.BlockSpec((HEADS_PER_VREG, RET_CHUNK, RET_CHUNK), lambda b, hp: (hp, 0, 0)),
                  pl.BlockSpec((RET_CHUNK, LANES), lambda b, hp: (0, hp)),
                  pl.BlockSpec((RET_CHUNK, LANES), lambda b, hp: (0, hp)),
                  pl.BlockSpec((1, LANES, LANES), lambda b, hp: (hp, 0, 0))],
        out_specs=seq(0),
        out_shape=jax.ShapeDtypeStruct((T, RET_WIDTH), BF16),
        scratch_shapes=[pltpu.VMEM((LANES, LANES), F32)],
        compiler_params=_params("parallel", "parallel"),
    )(ret_qkv, ret_qkv, ret_qkv, rg, gn_w, dec, qd, kd, cdm)


def _retention_constants():
    H, C = RET_HEADS, RET_CHUNK
    log_g = jnp.log1p(-jnp.exp2(-5.0 - jnp.arange(H, dtype=F32)))
    pos = jnp.arange(C, dtype=F32)
    diff = pos[:, None] - pos[None, :]
    dec = jnp.where(diff >= 0, jnp.exp(jnp.maximum(diff, 0.0) * log_g[:, None, None]), 0.0)
    q_decay = jnp.exp((pos + 1.0) * log_g[:, None]).T
    k_decay = jnp.exp((C - 1.0 - pos) * log_g[:, None]).T
    chunk_decay = jnp.exp(C * log_g)
    qd = jnp.repeat(q_decay, HEAD_DIM, axis=1)
    kd = jnp.repeat(k_decay, HEAD_DIM, axis=1)
    cd_rows = jnp.repeat(chunk_decay, HEAD_DIM).reshape(H // HEADS_PER_VREG, LANES, 1)
    head_of = jnp.arange(LANES) // HEAD_DIM
    same = (head_of[:, None] == head_of[None, :]).astype(F32)
    return dec, qd, kd, cd_rows * same[None]


def _compress_kernel(tk_ref, tv_ref, pos_ref, w1_ref, w2k_ref, w2vt_ref, ck_ref, cvt_ref):
    pos = jnp.broadcast_to(pos_ref[...], (8, pos_ref.shape[1])).astype(BF16)

    def hidden(t, w1_ref_a):
        half = t.shape[1]
        bias = _dot(pos, w1_ref_a[...])[0:1, :]
        first = _dot(t, w1_ref_a[:half, :])
        second = _dot(t, w1_ref_a[half:, :])
        n = t.shape[0]
        h = first + pltpu.roll(second, n - 1, 0) + bias
        return jax.nn.gelu(h).astype(BF16)

    ck_ref[0, 0] = _dot(hidden(tk_ref[0, 0], w1_ref.at[0]), w2k_ref[...]).astype(BF16)
    cvt_ref[0, 0] = _dot_nt(w2vt_ref[...], hidden(tv_ref[0, 0], w1_ref.at[1])).astype(BF16)


def _compress(cmp_tok, pos_flat, w1, w2k, w2vt, B):
    _, G, rows, width = cmp_tok.shape
    n = rows // B
    tok = lambda a: pl.BlockSpec((1, 1, n, width), lambda b, g: (a, g, b, 0))
    full = lambda a: pl.BlockSpec(a.shape, lambda b, g: (0,) * a.ndim)
    return pl.pallas_call(
        _compress_kernel,
        grid=(B, G),
        in_specs=[tok(0), tok(1), full(pos_flat), full(w1), full(w2k), full(w2vt)],
        out_specs=[pl.BlockSpec((1, 1, n, LANES), lambda b, g: (b, g, 0, 0)),
                   pl.BlockSpec((1, 1, HEAD_DIM, n), lambda b, g: (b, g, 0, 0))],
        out_shape=[jax.ShapeDtypeStruct((B, G, n, LANES), BF16),
                   jax.ShapeDtypeStruct((B, G, HEAD_DIM, n), BF16)],
        compiler_params=_params("parallel", "parallel"),
    )(cmp_tok, cmp_tok, pos_flat, w1, w2k, w2vt)


def _nsa_kernel(q_ref, ck_ref, cvt_ref, ks_ref, kw_ref, vst_ref, vwt_ref, gt_ref, ovt_ref, eb_ref,
                o_ref, m_ref, acc_ref, mw_ref, accw_ref, out_ref, score_ref, ahead_ref,
                sa_ref, sb_ref, sw_ref,
                *, tq, top):
    R = NSA_GROUP
    D = HEAD_DIM
    t0 = pl.multiple_of(pl.program_id(2) * tq, tq)
    t_off = lax.broadcasted_iota(jnp.int32, (1, tq), 1)
    tcol = t0 + t_off
    k_off = lax.broadcasted_iota(jnp.int32, (tq, 1), 0)
    low = lax.broadcasted_iota(jnp.int32, (tq, LANES), 1) < D
    ones_rows = jnp.ones((BF16_ROWS, tq), BF16)
    gate = lambda r, br: gt_ref[r * N_BRANCH + br:r * N_BRANCH + br + 1, :]

    def q_aug(r, extra):
        return jnp.concatenate([q_ref[r * D:(r + 1) * D, :], extra], axis=0)

    zero_rows = jnp.zeros((LANES - D, tq), BF16)

    def value_rows(vt_ref_, start):
        return jnp.concatenate([vt_ref_[:, pl.ds(start, tq)], ones_rows], axis=0)

    sel_state, win_state = (m_ref, acc_ref), (mw_ref, accw_ref)

    def reset(state):
        for r in range(R):
            state[0][r] = jnp.full((1, tq), LOWEST, F32)
            state[1][r] = jnp.zeros(acc_ref.shape[1:], F32)

    def more(state, r, s_t, vrows):
        m_old = state[0][r]
        m_new = jnp.maximum(m_old, jnp.max(s_t, axis=0, keepdims=True))
        state[0][r] = m_new
        state[1][r] = (jnp.exp2(m_old - m_new) * state[1][r]
                       + _dot(vrows, jnp.exp2(s_t - m_new).astype(BF16)))

    def mix(state, r, br):
        o = state[1][r, :D, :] * ((1.0 / state[1][r, D:D + 1, :]) * gate(r, br))
        out_ref[r * D:(r + 1) * D, :] = out_ref[r * D:(r + 1) * D, :] + o

    ck = ck_ref[0, 0]
    ncp = ck.shape[0]
    cvt = jnp.concatenate([cvt_ref[0, 0], jnp.ones((BF16_ROWS, ncp), BF16)], axis=0)
    cmp_end = lax.broadcasted_iota(jnp.int32, (ncp, 1), 0) * CMP_STRIDE + (CMP_BLOCK - 1)
    cmask = cmp_end <= tcol
    sees_any = jnp.where(tcol >= CMP_BLOCK - 1, 1.0, 0.0)
    n_sel = ovt_ref.shape[0]
    imp_t = jnp.zeros((n_sel, tq), F32)
    scores = [_dot(ck, q_aug(r, zero_rows)) for r in range(R)]

    n_back = WINDOW // tq
    win_starts = [pl.multiple_of(jnp.maximum(t0 - j * tq, 0), tq) for j in range(n_back + 1)]
    flag_lane = lax.broadcasted_iota(jnp.int32, (tq, LANES), 1) == D
    flag_row = lax.broadcasted_iota(jnp.int32, (LANES - D, tq), 0) == 0
    qw = [q_aug(r, jnp.where(flag_row, 1.0, 0.0).astype(BF16)) for r in range(R)]

    def issue_window(j):
        off = jnp.full((tq, LANES), jnp.where(t0 >= j * tq, 0.0, MASKED), F32).astype(BF16)
        kw = jnp.where(flag_lane, off, kw_ref[0, pl.ds(win_starts[j], tq), :])
        for r in range(R):
            sw_ref[j, r] = _dot(kw, qw[r])

    for r in range(R):
        for j in range(r, n_back + 1, R):
            issue_window(j)
        s = jnp.where(cmask, scores[r], MASKED)
        e = jnp.exp2(s - jnp.max(s, axis=0, keepdims=True)).astype(BF16)
        acc = _dot(cvt, e)
        inv_l = sees_any * (1.0 / acc[D:D + 1, :])
        out_ref[r * D:(r + 1) * D, :] = acc[:D, :] * (inv_l * gate(r, 0))
        imp_t = imp_t + _dot(ovt_ref[...], e) * inv_l

    blk_i = lax.broadcasted_iota(jnp.int32, (n_sel, tq), 0)
    cur = tcol // SEL_BLOCK
    valid = blk_i * SEL_BLOCK <= tcol
    forced = (blk_i == 0) | (blk_i == cur) | (blk_i == cur - 1)
    score_ref[...] = jnp.where(forced, jnp.inf, jnp.where(valid, imp_t, -jnp.inf))
    ahead_ref[...] = jnp.zeros(ahead_ref.shape, F32)
    live_blocks = (t0 + tq) // SEL_BLOCK

    def count(jg, kb_lo, kb_hi):
        mine = score_ref[jg * 8:(jg + 1) * 8, :]
        blk_g = jg * 8 + lax.broadcasted_iota(jnp.int32, (8, tq), 0)
        ahead = ahead_ref[jg * 8:(jg + 1) * 8, :]
        for kb in range(kb_lo, kb_hi):
            other = score_ref[kb:kb + 1, :]
            if kb < jg * 8:
                ahead = ahead + jnp.where(other >= mine, 1.0, 0.0)
            elif kb >= (jg + 1) * 8:
                ahead = ahead + jnp.where(other > mine, 1.0, 0.0)
            else:
                ahead = ahead + jnp.where(blk_g > kb, jnp.where(other >= mine, 1.0, 0.0),
                                          jnp.where(other > mine, 1.0, 0.0))
        ahead_ref[jg * 8:(jg + 1) * 8, :] = ahead

    def rank_level(lo, hi):
        for jg in range(lo // 8):
            count(jg, lo, hi)
        for jg in range(lo // 8, hi // 8):
            count(jg, 0, hi)

    step = RANK_STEP if n_sel % RANK_STEP == 0 else n_sel
    rank_level(0, step)
    for lo in range(step, n_sel, step):
        @pl.when(live_blocks > lo)
        def _():
            rank_level(lo, lo + step)

    pad_rows = LANES - D - n_sel
    not_sel = jnp.where(ahead_ref[...] < top, 0.0, 1.0)
    if pad_rows:
        not_sel = jnp.concatenate([not_sel, jnp.zeros((pad_rows, tq), F32)], axis=0)
    not_sel = not_sel.astype(BF16)
    qa = [q_aug(r, not_sel) for r in range(R)]

    def keys_aug(start):
        return jnp.where(low, ks_ref[0, pl.ds(start, tq), :], eb_ref[pl.ds(start, tq), :])

    def issue(c, buf_ref):
        ka = keys_aug(pl.multiple_of(c * tq, tq))
        for r in range(R):
            buf_ref[r] = _dot(ka, qa[r])

    def consume(c, buf_ref, own):
        vr = value_rows(vst_ref, pl.multiple_of(c * tq, tq))
        for r in range(R):
            more(sel_state, r, jnp.where(causal, buf_ref[r], MASKED) if own else buf_ref[r], vr)

    causal = k_off <= t_off
    n_plain = pl.program_id(2)
    reset(sel_state)
    reset(win_state)
    issue(0, sa_ref)
    for j in range(n_back + 1):
        vr = value_rows(vwt_ref, win_starts[j])
        for r in range(R):
            s_t = sw_ref[j, r]
            if j == 0:
                s_t = jnp.where(causal, s_t, MASKED)
            elif j == n_back:
                s_t = jnp.where(k_off > t_off, s_t, MASKED)
            more(win_state, r, s_t, vr)
    bufs = (sa_ref, sb_ref)

    def sel_trip(k, carry):
        for u in range(SEL_UNROLL):
            c = SEL_UNROLL * k + u
            issue(c + 1, bufs[(u + 1) % 2])
            consume(c, bufs[u % 2], False)
        return carry

    lax.fori_loop(0, n_plain // SEL_UNROLL, sel_trip, 0)

    def finish():
        for r in range(R):
            mix(sel_state, r, 1)
            mix(win_state, r, 2)
        o_ref[...] = out_ref[...].T.astype(BF16)

    for rem in range(SEL_UNROLL):
        @pl.when(n_plain % SEL_UNROLL == rem)
        def _():
            for u in range(rem):
                c = n_plain - rem + u
                issue(c + 1, bufs[(u + 1) % 2])
                consume(c, bufs[u % 2], False)
            consume(n_plain, bufs[rem % 2], True)
            finish()


def _nsa(nq, ck, cvt, kdup, vt, gt, ovt, eb, B, S, tq):
    T = B * S
    G = NSA_KV_HEADS
    assert S % tq == 0 and WINDOW % tq == 0 and S // SEL_BLOCK <= LANES - HEAD_DIM
    nqt = S // tq
    ncp = ck.shape[2]
    kdup3 = kdup.reshape(B, S, kdup.shape[1])
    acc_rows = HEAD_DIM + BF16_ROWS
    keys = lambda off: pl.BlockSpec((1, S, LANES), lambda b, g, i: (b, 0, off + g))
    vals = lambda off: pl.BlockSpec((HEAD_DIM, S), lambda b, g, i: (off + g, b))
    full = lambda a: pl.BlockSpec(a.shape, lambda b, g, i: (0,) * a.ndim)
    return pl.pallas_call(
        functools.partial(_nsa_kernel, tq=tq, top=min(SEL_TOPK, S // SEL_BLOCK)),
        grid=(B, G, nqt),
        in_specs=[pl.BlockSpec((NSA_GROUP * HEAD_DIM, tq), lambda b, g, i: (g, b * nqt + i)),
                  pl.BlockSpec((1, 1, ncp, LANES), lambda b, g, i: (b, g, 0, 0)),
                  pl.BlockSpec((1, 1, HEAD_DIM, ncp), lambda b, g, i: (b, g, 0, 0)),
                  keys(0), keys(G), vals(0), vals(G),
                  pl.BlockSpec((GATE_ROWS, tq), lambda b, g, i: (g, b * nqt + i)),
                  full(ovt), full(eb)],
        out_specs=pl.BlockSpec((tq, NSA_GROUP * HEAD_DIM), lambda b, g, i: (b * nqt + i, g)),
        out_shape=jax.ShapeDtypeStruct((T, NSA_WIDTH), BF16),
        scratch_shapes=[pltpu.VMEM((NSA_GROUP, 1, tq), F32),
                        pltpu.VMEM((NSA_GROUP, acc_rows, tq), F32),
                        pltpu.VMEM((NSA_GROUP, 1, tq), F32),
                        pltpu.VMEM((NSA_GROUP, acc_rows, tq), F32),
                        pltpu.VMEM((NSA_GROUP * HEAD_DIM, tq), F32),
                        pltpu.VMEM((S // SEL_BLOCK, tq), F32),
                        pltpu.VMEM((S // SEL_BLOCK, tq), F32),
                        pltpu.VMEM((NSA_GROUP, tq, tq), F32),
                        pltpu.VMEM((NSA_GROUP, tq, tq), F32),
                        pltpu.VMEM((WINDOW // tq + 1, NSA_GROUP, tq, tq), F32)],
        compiler_params=_params("parallel", "parallel", "parallel"),
    )(nq, ck, cvt, kdup3, kdup3, vt, vt, gt, ovt, eb)


def _nsa_constants(S):
    n_cmp = (S - CMP_BLOCK) // CMP_STRIDE + 1
    ncp = S // CMP_STRIDE
    n_sel = S // SEL_BLOCK
    ci = np.arange(ncp)[None, :]
    sj = np.arange(n_sel)[:, None]
    ovt = ((ci * CMP_STRIDE < (sj + 1) * SEL_BLOCK)
           & (ci * CMP_STRIDE + CMP_BLOCK > sj * SEL_BLOCK) & (ci < n_cmp))
    eb = np.zeros((S, LANES), np.float32)
    eb[np.arange(S), HEAD_DIM + np.arange(S) // SEL_BLOCK] = MASKED
    return jnp.asarray(ovt, BF16), jnp.asarray(eb, BF16)


def _mlp_kernel(ret_ref, nsa_ref, x_ref, p_ref, wo_ref, nw_ref, wg_ref, wv_ref, cw_ref, cb_ref,
                wd_ref, pg_ref, pw_ref, fw_ref, o_ref, h_ref, act_ref, tail_ref, *, fc, final):
    @pl.when(pl.program_id(1) == 0)
    def _():
        tail_ref[...] = jnp.zeros_like(tail_ref)

    mix = _dot(ret_ref[...], wo_ref[:RET_WIDTH, :]) + _dot(nsa_ref[...], wo_ref[RET_WIDTH:, :])
    h = x_ref[...] + mix
    h_ref[...] = h
    ms = jnp.mean(h * h, axis=-1, keepdims=True)
    hn = (h * lax.rsqrt(ms + EPS) * nw_ref[...]).astype(BF16)

    tm = hn.shape[0]
    row = lax.broadcasted_iota(jnp.int32, (tm, fc), 0)
    for c in range(wg_ref.shape[1] // fc):
        sl = slice(c * fc, (c + 1) * fc)
        g = _dot(hn, wg_ref[:, sl])
        val = _dot(hn, wv_ref[:, sl])
        tail = tail_ref[:, sl]
        g1 = jnp.where(row == 0, tail[7:8, :], pltpu.roll(g, 1, 0))
        g2 = jnp.where(row == 0, tail[6:7, :],
                       jnp.where(row == 1, tail[7:8, :], pltpu.roll(g, 2, 0)))
        tail_ref[:, sl] = g[tm - 8:, :]
        cw = cw_ref[:, sl]
        y = cb_ref[:, sl] + cw[0:1, :] * g2 + cw[1:2, :] * g1 + cw[2:3, :] * g
        act_ref[:, sl] = (y * jax.nn.sigmoid(y) * val).astype(BF16)

    h = h_ref[...] + _dot(act_ref[...], wd_ref[...])
    gate = jax.nn.sigmoid(_dot(h.astype(BF16), pg_ref[...]))
    h = h + gate * _dot(p_ref[...].astype(BF16), pw_ref[...])
    if final:
        ms = jnp.mean(h * h, axis=-1, keepdims=True)
        h = h * lax.rsqrt(ms + EPS) * fw_ref[...]
    o_ref[...] = h


def _mlp(ret_out, nsa_out, x2, p2, w_out, norm_w, wg, wv, conv_w, conv_b, wd, pg, pw, fw,
         B, S, tm, final):
    T, D = x2.shape
    F = wg.shape[1]
    nt = S // tm
    row = lambda w: pl.BlockSpec((tm, w), lambda b, j: (b * nt + j, 0))
    full = lambda a: pl.BlockSpec(a.shape, lambda b, j: (0,) * a.ndim, pipeline_mode=pl.Buffered(1))
    weights = (w_out, norm_w, wg, wv, conv_w, conv_b, wd, pg, pw, fw)
    return pl.pallas_call(
        functools.partial(_mlp_kernel, fc=256, final=final),
        grid=(B, nt),
        in_specs=[row(RET_WIDTH), row(NSA_WIDTH), row(D), row(p2.shape[1])]
                 + [full(w) for w in weights],
        out_specs=row(D),
        out_shape=jax.ShapeDtypeStruct((T, D), F32),
        scratch_shapes=[pltpu.VMEM((tm, D), F32), pltpu.VMEM((tm, F), BF16),
                        pltpu.VMEM((8, F), F32)],
        compiler_params=_params("arbitrary", "arbitrary"),
    )(ret_out, nsa_out, x2, p2, *weights)


def _rope_tables(tm):
    lane = np.arange(LANES) % HEAD_DIM
    half_r = HEAD_DIM // 2
    inv_r = jnp.power(jnp.float32(RET_THETA), -jnp.arange(half_r, dtype=F32) / half_r)
    half_n = ROPE_DIM // 2
    inv_n = jnp.power(jnp.float32(ROPE_THETA), -jnp.arange(half_n, dtype=F32) / half_n)
    signs = [jnp.asarray(np.where(lane < half_r, -1.0, 1.0), F32),
             jnp.asarray(np.where(lane < half_n, -1.0, 1.0), F32)]
    table = jnp.concatenate([jnp.stack(signs), jnp.zeros((6, LANES), F32)], axis=0)
    freq = jnp.concatenate([inv_r, inv_n])
    return table, jnp.broadcast_to(freq[:, None], (half_r + half_n, tm))


def _layer(h2, p2, pos2, B, S, final, norm_mix_w, w_in, ret_gn_w, cmp_pos, cmp_k_w1, cmp_k_w2,
           cmp_v_w1, cmp_v_w2, w_out, norm_ffn_w, ffn_w_up, ffn_conv_w, ffn_conv_b, ffn_w_down,
           ple_w, ple_gate_w, final_norm_w):
    T, D = h2.shape
    tm = _row_tile(S, 512)

    w_main, w_t = _in_proj_weights(w_in)
    ret_qkv, rg, qt, cmp_tok, kdup, vt, gt = _in_proj(
        h2, pos2, norm_mix_w.reshape(1, D), w_main, w_t, *_rope_tables(tm), tm)

    dec, qd, kd, cdm = _retention_constants()
    ret_out = _retention(ret_qkv, rg, ret_gn_w.reshape(1, RET_WIDTH), dec, qd, kd, cdm, B, S)

    w1 = jnp.stack([cmp_k_w1, cmp_v_w1]).astype(BF16)
    w2k = jnp.tile(cmp_k_w2, (1, HEADS_PER_VREG)).astype(BF16)
    ck, cvt = _compress(cmp_tok, cmp_pos.reshape(1, CMP_BLOCK * HEAD_DIM), w1, w2k,
                        cmp_v_w2.T.astype(BF16), B)

    ovt, eb = _nsa_constants(S)
    nsa_out = _nsa(qt, ck, cvt, kdup, vt, gt, ovt, eb, B, S, tq=256)

    d_ff = ffn_w_down.shape[0]
    conv_w = jnp.pad(ffn_conv_w, ((0, 8 - ffn_conv_w.shape[0]), (0, 0)))
    return _mlp(ret_out, nsa_out, h2, p2, w_out.astype(BF16), norm_ffn_w.reshape(1, D),
                ffn_w_up[:, :d_ff].astype(BF16), ffn_w_up[:, d_ff:].astype(BF16),
                conv_w, ffn_conv_b.reshape(1, d_ff), ffn_w_down.astype(BF16),
                ple_gate_w.astype(BF16), ple_w.astype(BF16), final_norm_w.reshape(1, D),
                B, S, tm, final)


def kernel(x, p, positions, norm_mix_w, w_in, ret_gn_w, cmp_pos, cmp_k_w1, cmp_k_w2, cmp_v_w1, cmp_v_w2, w_out, norm_ffn_w, ffn_w_up, ffn_conv_w, ffn_conv_b, ffn_w_down, ple_w, ple_gate_w, final_norm_w):
    B, S, D = x.shape
    T = B * S
    depth = w_in.shape[0]
    h = x.reshape(T, D)
    pos2 = positions.reshape(T).astype(jnp.int32)
    for i in range(depth):
        h = _layer(h, p[i].reshape(T, -1), pos2, B, S, i == depth - 1, norm_mix_w[i], w_in[i],
                   ret_gn_w[i], cmp_pos[i], cmp_k_w1[i], cmp_k_w2[i], cmp_v_w1[i], cmp_v_w2[i],
                   w_out[i], norm_ffn_w[i], ffn_w_up[i], ffn_conv_w[i], ffn_conv_b[i],
                   ffn_w_down[i], ple_w[i], ple_gate_w[i], final_norm_w)
    return h.reshape(B, S, D)
```

```python
import functools

import numpy as np
import jax
import jax.numpy as jnp
from jax import lax
from jax.experimental import pallas as pl
from jax.experimental.pallas import tpu as pltpu

F32 = jnp.float32
BF16 = jnp.bfloat16

LANES = 128
BF16_ROWS = 16
HEAD_DIM = 64
HEADS_PER_VREG = LANES // HEAD_DIM
RET_HEADS = 8
NSA_Q_HEADS = 8
NSA_KV_HEADS = 2
NSA_GROUP = NSA_Q_HEADS // NSA_KV_HEADS
RET_WIDTH = RET_HEADS * HEAD_DIM
NSA_WIDTH = NSA_Q_HEADS * HEAD_DIM
KV_WIDTH = NSA_KV_HEADS * HEAD_DIM
RET_CHUNK = 128
RET_THETA = 10000.0
ROPE_THETA = 500000.0
ROPE_DIM = HEAD_DIM // 4
CMP_BLOCK = 32
CMP_STRIDE = 16
SEL_BLOCK = 64
SEL_TOPK = 16
WINDOW = 512
N_BRANCH = 3
GATE_ROWS = 16
RANK_STEP = 8
SEL_UNROLL = 8
EPS = 1e-6
QK_SCALE = HEAD_DIM ** -0.5
LOG2_E = 1.4426950408889634
MASKED = -1e30
LOWEST = -3e38

_OFF = np.cumsum([0, RET_WIDTH, RET_WIDTH, RET_WIDTH, RET_WIDTH, NSA_WIDTH,
                  KV_WIDTH, KV_WIDTH, KV_WIDTH, KV_WIDTH, KV_WIDTH, KV_WIDTH])
(_RQ, _RK, _RV, _RG, _NQ, _KC, _VC, _KSL, _VSL, _KWN, _VWN, _NG) = (int(v) for v in _OFF)

VMEM_LIMIT = 56 * 1024 * 1024


def _dot(a, b):
    return jnp.dot(a, b, preferred_element_type=F32)


def _dot_nt(a, b):
    return lax.dot_general(a, b, (((1,), (1,)), ((), ())), preferred_element_type=F32)


def _dot_tn(a, b):
    return lax.dot_general(a, b, (((0,), (0,)), ((), ())), preferred_element_type=F32)


def _params(*semantics):
    return pltpu.CompilerParams(dimension_semantics=semantics, vmem_limit_bytes=VMEM_LIMIT)


def _row_tile(n, pref):
    t = pref
    while n % t:
        t //= 2
    return t


def _rope(y, cos, sin_signed, first_half, half):
    nxt = pltpu.roll(y, LANES - half, 1)
    prv = pltpu.roll(y, half, 1)
    return y * cos + jnp.where(first_half, nxt, prv) * sin_signed


def _in_proj_kernel(x_ref, posrow_ref, nw_ref, w_ref, wt_ref, tab_ref, freq_ref,
                    ret_ref, rg_ref, qt_ref, cmp_ref, kdup_ref, vt_ref, gt_ref, stage_ref):
    x = x_ref[...]
    tm = x.shape[0]
    ms = jnp.mean(x * x, axis=-1, keepdims=True)
    xn = (x * lax.rsqrt(ms + EPS) * nw_ref[...]).astype(BF16)

    tab = tab_ref[...]
    lane = lax.broadcasted_iota(jnp.int32, (tm, LANES), 1)
    in_head = lane % HEAD_DIM
    low = lane < HEAD_DIM
    half_r, half_n = HEAD_DIM // 2, ROPE_DIM // 2
    ang_t = freq_ref[...] * posrow_ref[0].astype(F32)
    cos_t, sin_t = jnp.cos(ang_t), jnp.sin(ang_t)
    pad = jnp.zeros((LANES - half_r - half_n, tm), F32)
    cos_a = jnp.concatenate([cos_t, pad], axis=0).T
    sin_a = jnp.concatenate([sin_t, pad], axis=0).T
    cos_q, sin_q = cos_t[half_r:, :], sin_t[half_r:, :]
    first_r = in_head < half_r
    first_n = in_head < half_n
    rot_n = in_head < ROPE_DIM

    def spread_r(c):
        c = jnp.where(lane < half_r, c, pltpu.roll(c, half_r, 1))
        return jnp.where(low, c, pltpu.roll(c, HEAD_DIM, 1))

    def spread_n(c, rest):
        c = pltpu.roll(c, LANES - half_r, 1)
        c = jnp.where(lane < half_n, c, pltpu.roll(c, half_n, 1))
        c = jnp.where(low, c, pltpu.roll(c, HEAD_DIM, 1))
        return jnp.where(rot_n, c, rest)

    cos_r, sin_r = spread_r(cos_a), spread_r(sin_a) * tab[0:1, :]
    cos_n, sin_n = spread_n(cos_a, 1.0), spread_n(sin_a, 0.0) * tab[1:2, :]

    rope_r = lambda y: _rope(y, cos_r, sin_r, first_r, HEAD_DIM // 2)
    rope_n = lambda y: _rope(y, cos_n, sin_n, first_n, ROPE_DIM // 2)

    def halves(ci):
        y = _dot(xn, w_ref[:, ci * 256:(ci + 1) * 256])
        return y[:, :LANES], y[:, LANES:]

    for ci in range(2):
        for h, y in enumerate(halves(4 + ci)):
            c0 = 2 * RET_WIDTH + ci * 256 + h * LANES
            ret_ref[:, c0:c0 + LANES] = y.astype(BF16)
    t = _dot_nt(wt_ref[...], xn)
    for hd in range(NSA_Q_HEADS):
        b0 = hd * HEAD_DIM
        x1, x2 = t[b0:b0 + half_n, :], t[b0 + half_n:b0 + ROPE_DIM, :]
        head = jnp.concatenate([x1 * cos_q - x2 * sin_q, x1 * sin_q + x2 * cos_q,
                                t[b0 + ROPE_DIM:b0 + HEAD_DIM, :]], axis=0)
        qt_ref[b0:b0 + HEAD_DIM, :] = (head * (QK_SCALE * LOG2_E)).astype(BF16)
    nq, nv = qt_ref.shape[0], vt_ref.shape[0]
    vt_ref[...] = t[nq:nq + nv, :].astype(BF16)
    gt_ref[...] = jax.nn.sigmoid(t[nq + nv:, :])
    for ci in range(2):
        for h, y in enumerate(halves(ci)):
            c0 = ci * 256 + h * LANES
            ret_ref[:, c0:c0 + LANES] = rope_r(y).astype(BF16)
    for ci in range(2):
        for h, y in enumerate(halves(2 + ci)):
            c0 = RET_WIDTH + ci * 256 + h * LANES
            ret_ref[:, c0:c0 + LANES] = (rope_r(y) * QK_SCALE).astype(BF16)
    kc, vc = halves(8)
    for a, y in enumerate((rope_n(kc), vc)):
        stage_ref[2 * a] = y
        stage_ref[2 * a + 1] = pltpu.roll(y, HEAD_DIM, 1)
    n_grp = tm // CMP_STRIDE
    low_g = lax.broadcasted_iota(jnp.int32, (n_grp, LANES), 1) < HEAD_DIM
    for a in range(2):
        for j in range(CMP_STRIDE // 2):
            token = lambda l, s: stage_ref[2 * a + s, pl.ds(l, n_grp, stride=CMP_STRIDE), :]
            even, odd = 2 * j, 2 * j + 1
            cmp_ref[a, 0, :, j * LANES:(j + 1) * LANES] = jnp.where(
                low_g, token(even, 0), token(odd, 1)).astype(BF16)
            cmp_ref[a, 1, :, j * LANES:(j + 1) * LANES] = jnp.where(
                low_g, token(even, 1), token(odd, 0)).astype(BF16)
    for n, y in enumerate(halves(9)):
        y = rope_n(y)
        sw = pltpu.roll(y, HEAD_DIM, 1)
        kdup_ref[:, (2 * n) * LANES:(2 * n + 1) * LANES] = jnp.where(low, y, sw).astype(BF16)
        kdup_ref[:, (2 * n + 1) * LANES:(2 * n + 2) * LANES] = jnp.where(low, sw, y).astype(BF16)
    for ci in range(2):
        for h, y in enumerate(halves(6 + ci)):
            c0 = ci * 256 + h * LANES
            rg_ref[:, c0:c0 + LANES] = y


def _in_proj(x2, positions, norm_w, w_main, w_t, tab, freq, tm):
    T, D = x2.shape
    nv = 2 * KV_WIDTH
    ng = w_t.shape[0] - nv - NSA_WIDTH
    grp_w = CMP_STRIDE * HEAD_DIM
    pos_rows = positions.reshape(T // tm, 1, tm)
    row = lambda w: pl.BlockSpec((tm, w), lambda i: (i, 0))
    col = lambda h: pl.BlockSpec((h, tm), lambda i: (0, i))
    full = lambda a: pl.BlockSpec(a.shape, lambda i: (0,) * a.ndim)
    return pl.pallas_call(
        _in_proj_kernel,
        grid=(T // tm,),
        in_specs=[row(D), pl.BlockSpec((1, 1, tm), lambda i: (i, 0, 0)),
                  full(norm_w), full(w_main), full(w_t), full(tab), full(freq)],
        out_specs=[row(3 * RET_WIDTH), row(RET_WIDTH), col(NSA_WIDTH),
                   pl.BlockSpec((2, NSA_KV_HEADS, tm // CMP_STRIDE, grp_w), lambda i: (0, 0, i, 0)),
                   row(4 * LANES), col(nv), col(ng)],
        out_shape=[jax.ShapeDtypeStruct((T, 3 * RET_WIDTH), BF16),
                   jax.ShapeDtypeStruct((T, RET_WIDTH), F32),
                   jax.ShapeDtypeStruct((NSA_WIDTH, T), BF16),
                   jax.ShapeDtypeStruct((2, NSA_KV_HEADS, T // CMP_STRIDE, grp_w), BF16),
                   jax.ShapeDtypeStruct((T, 4 * LANES), BF16),
                   jax.ShapeDtypeStruct((nv, T), BF16),
                   jax.ShapeDtypeStruct((ng, T), F32)],
        scratch_shapes=[pltpu.VMEM((4, tm, LANES), F32)],
        compiler_params=_params("parallel"),
    )(x2, pos_rows, norm_w, w_main, w_t, tab, freq)


def _in_proj_weights(w_in):
    w_main = jnp.concatenate([w_in[:, :_NQ], w_in[:, _KC:_KSL], w_in[:, _KSL:_VSL],
                              w_in[:, _KWN:_VWN]], axis=1)
    gate_cols = np.full((NSA_KV_HEADS * GATE_ROWS,), -1)
    for g in range(NSA_KV_HEADS):
        for r in range(NSA_GROUP):
            for br in range(N_BRANCH):
                gate_cols[g * GATE_ROWS + r * N_BRANCH + br] = _NG + (g * NSA_GROUP + r) * N_BRANCH + br
    w_gate = jnp.where(gate_cols[None, :] >= 0, w_in[:, np.maximum(gate_cols, 0)], 0.0)
    w_t = jnp.concatenate([w_in[:, _NQ:_KC], w_in[:, _VSL:_KWN], w_in[:, _VWN:_NG], w_gate],
                          axis=1).T
    return w_main.astype(BF16), w_t.astype(BF16)


def _retention_kernel(q_ref, k_ref, v_ref, rg_ref, gnw_ref, dec_ref, qd_ref, kd_ref, cdm_ref,
                      o_ref, state_ref, *, n_chunks, unroll):
    C = RET_CHUNK
    lane = lax.broadcasted_iota(jnp.int32, (C, LANES), 1)
    low = lane < HEAD_DIM
    r_i = lax.broadcasted_iota(jnp.int32, (LANES, LANES), 0)
    c_i = lax.broadcasted_iota(jnp.int32, (LANES, LANES), 1)
    same_head = (r_i < HEAD_DIM) == (c_i < HEAD_DIM)
    state_ref[...] = jnp.zeros_like(state_ref)

    def group(gi, carry):
        rows = [pl.multiple_of((gi * unroll + u) * C, C) for u in range(unroll)]
        q = [q_ref[pl.ds(r0, C), :] for r0 in rows]
        k = [k_ref[pl.ds(r0, C), :] for r0 in rows]
        v = [v_ref[pl.ds(r0, C), :] for r0 in rows]
        zero = jnp.zeros_like(q[0])
        s0 = [_dot_nt(jnp.where(low, q[u], zero), k[u]) for u in range(unroll)]
        s1 = [_dot_nt(jnp.where(low, zero, q[u]), k[u]) for u in range(unroll)]
        kv = [_dot_tn((k[u].astype(F32) * kd_ref[...]).astype(BF16), v[u]) for u in range(unroll)]

        p0 = [(s0[u] * dec_ref[0]).astype(BF16) for u in range(unroll)]
        p1 = [(s1[u] * dec_ref[1]).astype(BF16) for u in range(unroll)]
        states = [state_ref[...]]
        for u in range(unroll):
            states.append(states[u] * cdm_ref[0] + jnp.where(same_head, kv[u], 0.0))
        state_ref[...] = states[unroll]

        o0 = [_dot(p0[u], v[u]) for u in range(unroll)]
        o1 = [_dot(p1[u], v[u]) for u in range(unroll)]
        oc = [_dot((q[u].astype(F32) * qd_ref[...]).astype(BF16), states[u].astype(BF16))
              for u in range(unroll)]

        avg = jnp.where(same_head, 1.0 / HEAD_DIM, 0.0).astype(BF16)

        def head_mean(x):
            hi = x.astype(BF16)
            lo = (x - hi.astype(F32)).astype(BF16)
            return _dot(hi, avg) + _dot(lo, avg)

        for u in range(unroll):
            ro = jnp.where(low, o0[u], o1[u]) + oc[u]
            d = ro - head_mean(ro)
            var = head_mean(d * d)
            y = d * lax.rsqrt(var + EPS) * gnw_ref[...]
            g = rg_ref[pl.ds(rows[u], C), :]
            o_ref[pl.ds(rows[u], C), :] = (y * (g * jax.nn.sigmoid(g))).astype(BF16)
        return carry

    lax.fori_loop(0, n_chunks // unroll, group, 0)


def _retention(ret_qkv, rg, gn_w, dec, qd, kd, cdm, B, S):
    T = B * S
    n_pairs = RET_HEADS // HEADS_PER_VREG
    seq = lambda off: pl.BlockSpec((S, LANES), lambda b, hp: (b, off + hp))
    return pl.pallas_call(
        functools.partial(_retention_kernel, n_chunks=S // RET_CHUNK,
                          unroll=_row_tile(S // RET_CHUNK, 32)),
        grid=(B, n_pairs),
        in_specs=[seq(0), seq(n_pairs), seq(2 * n_pairs), seq(0),
                  pl.BlockSpec((1, LANES), lambda b, hp: (0, hp)),
                  pl.BlockSpec((HEADS_PER_VREG, RET_CHUNK, RET_CHUNK), lambda b, hp: (hp, 0, 0)),
                  pl.BlockSpec((RET_CHUNK, LANES), lambda b, hp: (0, hp)),
                  pl.BlockSpec((RET_CHUNK, LANES), lambda b, hp: (0, hp)),
                  pl.BlockSpec((1, LANES, LANES), lambda b, hp: (hp, 0, 0))],
        out_specs=seq(0),
        out_shape=jax.ShapeDtypeStruct((T, RET_WIDTH), BF16),
        scratch_shapes=[pltpu.VMEM((LANES, LANES), F32)],
        compiler_params=_params("parallel", "parallel"),
    )(ret_qkv, ret_qkv, ret_qkv, rg, gn_w, dec, qd, kd, cdm)


def _retention_constants():
    H, C = RET_HEADS, RET_CHUNK
    log_g = jnp.log1p(-jnp.exp2(-5.0 - jnp.arange(H, dtype=F32)))
    pos = jnp.arange(C, dtype=F32)
    diff = pos[:, None] - pos[None, :]
    dec = jnp.where(diff >= 0, jnp.exp(jnp.maximum(diff, 0.0) * log_g[:, None, None]), 0.0)
    q_decay = jnp.exp((pos + 1.0) * log_g[:, None]).T
    k_decay = jnp.exp((C - 1.0 - pos) * log_g[:, None]).T
    chunk_decay = jnp.exp(C * log_g)
    qd = jnp.repeat(q_decay, HEAD_DIM, axis=1)
    kd = jnp.repeat(k_decay, HEAD_DIM, axis=1)
    cd_rows = jnp.repeat(chunk_decay, HEAD_DIM).reshape(H // HEADS_PER_VREG, LANES, 1)
    head_of = jnp.arange(LANES) // HEAD_DIM
    same = (head_of[:, None] == head_of[None, :]).astype(F32)
    return dec, qd, kd, cd_rows * same[None]


def _compress_kernel(tk_ref, tv_ref, pos_ref, w1_ref, w2k_ref, w2vt_ref, ck_ref, cvt_ref):
    pos = jnp.broadcast_to(pos_ref[...], (8, pos_ref.shape[1])).astype(BF16)

    def hidden(t, w1_ref_a):
        half = t.shape[1]
        bias = _dot(pos, w1_ref_a[...])[0:1, :]
        first = _dot(t, w1_ref_a[:half, :])
        second = _dot(t, w1_ref_a[half:, :])
        n = t.shape[0]
        h = first + pltpu.roll(second, n - 1, 0) + bias
        return jax.nn.gelu(h).astype(BF16)

    ck_ref[0, 0] = _dot(hidden(tk_ref[0, 0], w1_ref.at[0]), w2k_ref[...]).astype(BF16)
    cvt_ref[0, 0] = _dot_nt(w2vt_ref[...], hidden(tv_ref[0, 0], w1_ref.at[1])).astype(BF16)


def _compress(cmp_tok, pos_flat, w1, w2k, w2vt, B):
    _, G, rows, width = cmp_tok.shape
    n = rows // B
    tok = lambda a: pl.BlockSpec((1, 1, n, width), lambda b, g: (a, g, b, 0))
    full = lambda a: pl.BlockSpec(a.shape, lambda b, g: (0,) * a.ndim)
    return pl.pallas_call(
        _compress_kernel,
        grid=(B, G),
        in_specs=[tok(0), tok(1), full(pos_flat), full(w1), full(w2k), full(w2vt)],
        out_specs=[pl.BlockSpec((1, 1, n, LANES), lambda b, g: (b, g, 0, 0)),
                   pl.BlockSpec((1, 1, HEAD_DIM, n), lambda b, g: (b, g, 0, 0))],
        out_shape=[jax.ShapeDtypeStruct((B, G, n, LANES), BF16),
                   jax.ShapeDtypeStruct((B, G, HEAD_DIM, n), BF16)],
        compiler_params=_params("parallel", "parallel"),
    )(cmp_tok, cmp_tok, pos_flat, w1, w2k, w2vt)


def _nsa_kernel(q_ref, ck_ref, cvt_ref, ks_ref, kw_ref, vst_ref, vwt_ref, gt_ref, ovt_ref, eb_ref,
                o_ref, m_ref, acc_ref, mw_ref, accw_ref, out_ref, score_ref, ahead_ref,
                sa_ref, sb_ref, sw_ref,
                *, tq, top):
    R = NSA_GROUP
    D = HEAD_DIM
    t0 = pl.multiple_of(pl.program_id(2) * tq, tq)
    t_off = lax.broadcasted_iota(jnp.int32, (1, tq), 1)
    tcol = t0 + t_off
    k_off = lax.broadcasted_iota(jnp.int32, (tq, 1), 0)
    low = lax.broadcasted_iota(jnp.int32, (tq, LANES), 1) < D
    ones_rows = jnp.ones((BF16_ROWS, tq), BF16)
    gate = lambda r, br: gt_ref[r * N_BRANCH + br:r * N_BRANCH + br + 1, :]

    def q_aug(r, extra):
        return jnp.concatenate([q_ref[r * D:(r + 1) * D, :], extra], axis=0)

    zero_rows = jnp.zeros((LANES - D, tq), BF16)

    def value_rows(vt_ref_, start):
        return jnp.concatenate([vt_ref_[:, pl.ds(start, tq)], ones_rows], axis=0)

    sel_state, win_state = (m_ref, acc_ref), (mw_ref, accw_ref)

    def reset(state):
        for r in range(R):
            state[0][r] = jnp.full((1, tq), LOWEST, F32)
            state[1][r] = jnp.zeros(acc_ref.shape[1:], F32)

    def more(state, r, s_t, vrows):
        m_old = state[0][r]
        m_new = jnp.maximum(m_old, jnp.max(s_t, axis=0, keepdims=True))
        state[0][r] = m_new
        state[1][r] = (jnp.exp2(m_old - m_new) * state[1][r]
                       + _dot(vrows, jnp.exp2(s_t - m_new).astype(BF16)))

    def mix(state, r, br):
        o = state[1][r, :D, :] * ((1.0 / state[1][r, D:D + 1, :]) * gate(r, br))
        out_ref[r * D:(r + 1) * D, :] = out_ref[r * D:(r + 1) * D, :] + o

    ck = ck_ref[0, 0]
    ncp = ck.shape[0]
    cvt = jnp.concatenate([cvt_ref[0, 0], jnp.ones((BF16_ROWS, ncp), BF16)], axis=0)
    cmp_end = lax.broadcasted_iota(jnp.int32, (ncp, 1), 0) * CMP_STRIDE + (CMP_BLOCK - 1)
    cmask = cmp_end <= tcol
    sees_any = jnp.where(tcol >= CMP_BLOCK - 1, 1.0, 0.0)
    n_sel = ovt_ref.shape[0]
    imp_t = jnp.zeros((n_sel, tq), F32)
    scores = [_dot(ck, q_aug(r, zero_rows)) for r in range(R)]

    n_back = WINDOW // tq
    win_starts = [pl.multiple_of(jnp.maximum(t0 - j * tq, 0), tq) for j in range(n_back + 1)]
    flag_lane = lax.broadcasted_iota(jnp.int32, (tq, LANES), 1) == D
    flag_row = lax.broadcasted_iota(jnp.int32, (LANES - D, tq), 0) == 0
    qw = [q_aug(r, jnp.where(flag_row, 1.0, 0.0).astype(BF16)) for r in range(R)]

    def issue_window(j):
        off = jnp.full((tq, LANES), jnp.where(t0 >= j * tq, 0.0, MASKED), F32).astype(BF16)
        kw = jnp.where(flag_lane, off, kw_ref[0, pl.ds(win_starts[j], tq), :])
        for r in range(R):
            sw_ref[j, r] = _dot(kw, qw[r])

    for r in range(R):
        for j in range(r, n_back + 1, R):
            issue_window(j)
        s = jnp.where(cmask, scores[r], MASKED)
        e = jnp.exp2(s - jnp.max(s, axis=0, keepdims=True)).astype(BF16)
        acc = _dot(cvt, e)
        inv_l = sees_any * (1.0 / acc[D:D + 1, :])
        out_ref[r * D:(r + 1) * D, :] = acc[:D, :] * (inv_l * gate(r, 0))
        imp_t = imp_t + _dot(ovt_ref[...], e) * inv_l

    blk_i = lax.broadcasted_iota(jnp.int32, (n_sel, tq), 0)
    cur = tcol // SEL_BLOCK
    valid = blk_i * SEL_BLOCK <= tcol
    forced = (blk_i == 0) | (blk_i == cur) | (blk_i == cur - 1)
    score_ref[...] = jnp.where(forced, jnp.inf, jnp.where(valid, imp_t, -jnp.inf))
    ahead_ref[...] = jnp.zeros(ahead_ref.shape, F32)
    live_blocks = (t0 + tq) // SEL_BLOCK

    def count(jg, kb_lo, kb_hi):
        mine = score_ref[jg * 8:(jg + 1) * 8, :]
        blk_g = jg * 8 + lax.broadcasted_iota(jnp.int32, (8, tq), 0)
        ahead = ahead_ref[jg * 8:(jg + 1) * 8, :]
        for kb in range(kb_lo, kb_hi):
            other = score_ref[kb:kb + 1, :]
            if kb < jg * 8:
                ahead = ahead + jnp.where(other >= mine, 1.0, 0.0)
            elif kb >= (jg + 1) * 8:
                ahead = ahead + jnp.where(other > mine, 1.0, 0.0)
            else:
                ahead = ahead + jnp.where(blk_g > kb, jnp.where(other >= mine, 1.0, 0.0),
                                          jnp.where(other > mine, 1.0, 0.0))
        ahead_ref[jg * 8:(jg + 1) * 8, :] = ahead

    def rank_level(lo, hi):
        for jg in range(lo // 8):
            count(jg, lo, hi)
        for jg in range(lo // 8, hi // 8):
            count(jg, 0, hi)

    step = RANK_STEP if n_sel % RANK_STEP == 0 else n_sel
    rank_level(0, step)
    for lo in range(step, n_sel, step):
        @pl.when(live_blocks > lo)
        def _():
            rank_level(lo, lo + step)

    pad_rows = LANES - D - n_sel
    not_sel = jnp.where(ahead_ref[...] < top, 0.0, 1.0)
    if pad_rows:
        not_sel = jnp.concatenate([not_sel, jnp.zeros((pad_rows, tq), F32)], axis=0)
    not_sel = not_sel.astype(BF16)
    qa = [q_aug(r, not_sel) for r in range(R)]

    def keys_aug(start):
        return jnp.where(low, ks_ref[0, pl.ds(start, tq), :], eb_ref[pl.ds(start, tq), :])

    def issue(c, buf_ref):
        ka = keys_aug(pl.multiple_of(c * tq, tq))
        for r in range(R):
            buf_ref[r] = _dot(ka, qa[r])

    def consume(c, buf_ref, own):
        vr = value_rows(vst_ref, pl.multiple_of(c * tq, tq))
        for r in range(R):
            more(sel_state, r, jnp.where(causal, buf_ref[r], MASKED) if own else buf_ref[r], vr)

    causal = k_off <= t_off
    n_plain = pl.program_id(2)
    reset(sel_state)
    reset(win_state)
    issue(0, sa_ref)
    for j in range(n_back + 1):
        vr = value_rows(vwt_ref, win_starts[j])
        for r in range(R):
            s_t = sw_ref[j, r]
            if j == 0:
                s_t = jnp.where(causal, s_t, MASKED)
            elif j == n_back:
                s_t = jnp.where(k_off > t_off, s_t, MASKED)
            more(win_state, r, s_t, vr)
    bufs = (sa_ref, sb_ref)

    def sel_trip(k, carry):
        for u in range(SEL_UNROLL):
            c = SEL_UNROLL * k + u
            issue(c + 1, bufs[(u + 1) % 2])
            consume(c, bufs[u % 2], False)
        return carry

    lax.fori_loop(0, n_plain // SEL_UNROLL, sel_trip, 0)

    def finish():
        for r in range(R):
            mix(sel_state, r, 1)
            mix(win_state, r, 2)
        o_ref[...] = out_ref[...].T.astype(BF16)

    for rem in range(SEL_UNROLL):
        @pl.when(n_plain % SEL_UNROLL == rem)
        def _():
            for u in range(rem):
                c = n_plain - rem + u
                issue(c + 1, bufs[(u + 1) % 2])
                consume(c, bufs[u % 2], False)
            consume(n_plain, bufs[rem % 2], True)
            finish()


def _nsa(nq, ck, cvt, kdup, vt, gt, ovt, eb, B, S, tq):
    T = B * S
    G = NSA_KV_HEADS
    assert S % tq == 0 and WINDOW % tq == 0 and S // SEL_BLOCK <= LANES - HEAD_DIM
    nqt = S // tq
    ncp = ck.shape[2]
    kdup3 = kdup.reshape(B, S, kdup.shape[1])
    acc_rows = HEAD_DIM + BF16_ROWS
    keys = lambda off: pl.BlockSpec((1, S, LANES), lambda b, g, i: (b, 0, off + g))
    vals = lambda off: pl.BlockSpec((HEAD_DIM, S), lambda b, g, i: (off + g, b))
    full = lambda a: pl.BlockSpec(a.shape, lambda b, g, i: (0,) * a.ndim)
    return pl.pallas_call(
        functools.partial(_nsa_kernel, tq=tq, top=min(SEL_TOPK, S // SEL_BLOCK)),
        grid=(B, G, nqt),
        in_specs=[pl.BlockSpec((NSA_GROUP * HEAD_DIM, tq), lambda b, g, i: (g, b * nqt + i)),
                  pl.BlockSpec((1, 1, ncp, LANES), lambda b, g, i: (b, g, 0, 0)),
                  pl.BlockSpec((1, 1, HEAD_DIM, ncp), lambda b, g, i: (b, g, 0, 0)),
                  keys(0), keys(G), vals(0), vals(G),
                  pl.BlockSpec((GATE_ROWS, tq), lambda b, g, i: (g, b * nqt + i)),
                  full(ovt), full(eb)],
        out_specs=pl.BlockSpec((tq, NSA_GROUP * HEAD_DIM), lambda b, g, i: (b * nqt + i, g)),
        out_shape=jax.ShapeDtypeStruct((T, NSA_WIDTH), BF16),
        scratch_shapes=[pltpu.VMEM((NSA_GROUP, 1, tq), F32),
                        pltpu.VMEM((NSA_GROUP, acc_rows, tq), F32),
                        pltpu.VMEM((NSA_GROUP, 1, tq), F32),
                        pltpu.VMEM((NSA_GROUP, acc_rows, tq), F32),
                        pltpu.VMEM((NSA_GROUP * HEAD_DIM, tq), F32),
                        pltpu.VMEM((S // SEL_BLOCK, tq), F32),
                        pltpu.VMEM((S // SEL_BLOCK, tq), F32),
                        pltpu.VMEM((NSA_GROUP, tq, tq), F32),
                        pltpu.VMEM((NSA_GROUP, tq, tq), F32),
                        pltpu.VMEM((WINDOW // tq + 1, NSA_GROUP, tq, tq), F32)],
        compiler_params=_params("parallel", "parallel", "parallel"),
    )(nq, ck, cvt, kdup3, kdup3, vt, vt, gt, ovt, eb)


def _nsa_constants(S):
    n_cmp = (S - CMP_BLOCK) // CMP_STRIDE + 1
    ncp = S // CMP_STRIDE
    n_sel = S // SEL_BLOCK
    ci = np.arange(ncp)[None, :]
    sj = np.arange(n_sel)[:, None]
    ovt = ((ci * CMP_STRIDE < (sj + 1) * SEL_BLOCK)
           & (ci * CMP_STRIDE + CMP_BLOCK > sj * SEL_BLOCK) & (ci < n_cmp))
    eb = np.zeros((S, LANES), np.float32)
    eb[np.arange(S), HEAD_DIM + np.arange(S) // SEL_BLOCK] = MASKED
    return jnp.asarray(ovt, BF16), jnp.asarray(eb, BF16)


def _mlp_kernel(ret_ref, nsa_ref, x_ref, p_ref, wo_ref, nw_ref, wg_ref, wv_ref, cw_ref, cb_ref,
                wd_ref, pg_ref, pw_ref, fw_ref, o_ref, h_ref, act_ref, tail_ref, *, fc, final):
    @pl.when(pl.program_id(1) == 0)
    def _():
        tail_ref[...] = jnp.zeros_like(tail_ref)

    mix = _dot(ret_ref[...], wo_ref[:RET_WIDTH, :]) + _dot(nsa_ref[...], wo_ref[RET_WIDTH:, :])
    h = x_ref[...] + mix
    h_ref[...] = h
    ms = jnp.mean(h * h, axis=-1, keepdims=True)
    hn = (h * lax.rsqrt(ms + EPS) * nw_ref[...]).astype(BF16)

    tm = hn.shape[0]
    row = lax.broadcasted_iota(jnp.int32, (tm, fc), 0)
    for c in range(wg_ref.shape[1] // fc):
        sl = slice(c * fc, (c + 1) * fc)
        g = _dot(hn, wg_ref[:, sl])
        val = _dot(hn, wv_ref[:, sl])
        tail = tail_ref[:, sl]
        g1 = jnp.where(row == 0, tail[7:8, :], pltpu.roll(g, 1, 0))
        g2 = jnp.where(row == 0, tail[6:7, :],
                       jnp.where(row == 1, tail[7:8, :], pltpu.roll(g, 2, 0)))
        tail_ref[:, sl] = g[tm - 8:, :]
        cw = cw_ref[:, sl]
        y = cb_ref[:, sl] + cw[0:1, :] * g2 + cw[1:2, :] * g1 + cw[2:3, :] * g
        act_ref[:, sl] = (y * jax.nn.sigmoid(y) * val).astype(BF16)

    h = h_ref[...] + _dot(act_ref[...], wd_ref[...])
    gate = jax.nn.sigmoid(_dot(h.astype(BF16), pg_ref[...]))
    h = h + gate * _dot(p_ref[...].astype(BF16), pw_ref[...])
    if final:
        ms = jnp.mean(h * h, axis=-1, keepdims=True)
        h = h * lax.rsqrt(ms + EPS) * fw_ref[...]
    o_ref[...] = h


def _mlp(ret_out, nsa_out, x2, p2, w_out, norm_w, wg, wv, conv_w, conv_b, wd, pg, pw, fw,
         B, S, tm, final):
    T, D = x2.shape
    F = wg.shape[1]
    nt = S // tm
    row = lambda w: pl.BlockSpec((tm, w), lambda b, j: (b * nt + j, 0))
    full = lambda a: pl.BlockSpec(a.shape, lambda b, j: (0,) * a.ndim, pipeline_mode=pl.Buffered(1))
    weights = (w_out, norm_w, wg, wv, conv_w, conv_b, wd, pg, pw, fw)
    return pl.pallas_call(
        functools.partial(_mlp_kernel, fc=256, final=final),
        grid=(B, nt),
        in_specs=[row(RET_WIDTH), row(NSA_WIDTH), row(D), row(p2.shape[1])]
                 + [full(w) for w in weights],
        out_specs=row(D),
        out_shape=jax.ShapeDtypeStruct((T, D), F32),
        scratch_shapes=[pltpu.VMEM((tm, D), F32), pltpu.VMEM((tm, F), BF16),
                        pltpu.VMEM((8, F), F32)],
        compiler_params=_params("arbitrary", "arbitrary"),
    )(ret_out, nsa_out, x2, p2, *weights)


def _rope_tables(tm):
    lane = np.arange(LANES) % HEAD_DIM
    half_r = HEAD_DIM // 2
    inv_r = jnp.power(jnp.float32(RET_THETA), -jnp.arange(half_r, dtype=F32) / half_r)
    half_n = ROPE_DIM // 2
    inv_n = jnp.power(jnp.float32(ROPE_THETA), -jnp.arange(half_n, dtype=F32) / half_n)
    signs = [jnp.asarray(np.where(lane < half_r, -1.0, 1.0), F32),
             jnp.asarray(np.where(lane < half_n, -1.0, 1.0), F32)]
    table = jnp.concatenate([jnp.stack(signs), jnp.zeros((6, LANES), F32)], axis=0)
    freq = jnp.concatenate([inv_r, inv_n])
    return table, jnp.broadcast_to(freq[:, None], (half_r + half_n, tm))


def _layer(h2, p2, pos2, B, S, final, norm_mix_w, w_in, ret_gn_w, cmp_pos, cmp_k_w1, cmp_k_w2,
           cmp_v_w1, cmp_v_w2, w_out, norm_ffn_w, ffn_w_up, ffn_conv_w, ffn_conv_b, ffn_w_down,
           ple_w, ple_gate_w, final_norm_w):
    T, D = h2.shape
    tm = _row_tile(S, 512)
    tm_in = _row_tile(S, 1024)

    w_main, w_t = _in_proj_weights(w_in)
    ret_qkv, rg, qt, cmp_tok, kdup, vt, gt = _in_proj(
        h2, pos2, norm_mix_w.reshape(1, D), w_main, w_t, *_rope_tables(tm_in), tm_in)

    dec, qd, kd, cdm = _retention_constants()
    ret_out = _retention(ret_qkv, rg, ret_gn_w.reshape(1, RET_WIDTH), dec, qd, kd, cdm, B, S)

    w1 = jnp.stack([cmp_k_w1, cmp_v_w1]).astype(BF16)
    w2k = jnp.tile(cmp_k_w2, (1, HEADS_PER_VREG)).astype(BF16)
    ck, cvt = _compress(cmp_tok, cmp_pos.reshape(1, CMP_BLOCK * HEAD_DIM), w1, w2k,
                        cmp_v_w2.T.astype(BF16), B)

    ovt, eb = _nsa_constants(S)
    nsa_out = _nsa(qt, ck, cvt, kdup, vt, gt, ovt, eb, B, S, tq=256)

    d_ff = ffn_w_down.shape[0]
    conv_w = jnp.pad(ffn_conv_w, ((0, 8 - ffn_conv_w.shape[0]), (0, 0)))
    return _mlp(ret_out, nsa_out, h2, p2, w_out.astype(BF16), norm_ffn_w.reshape(1, D),
                ffn_w_up[:, :d_ff].astype(BF16), ffn_w_up[:, d_ff:].astype(BF16),
                conv_w, ffn_conv_b.reshape(1, d_ff), ffn_w_down.astype(BF16),
                ple_gate_w.astype(BF16), ple_w.astype(BF16), final_norm_w.reshape(1, D),
                B, S, tm, final)


def kernel(x, p, positions, norm_mix_w, w_in, ret_gn_w, cmp_pos, cmp_k_w1, cmp_k_w2, cmp_v_w1, cmp_v_w2, w_out, norm_ffn_w, ffn_w_up, ffn_conv_w, ffn_conv_b, ffn_w_down, ple_w, ple_gate_w, final_norm_w):
    B, S, D = x.shape
    T = B * S
    depth = w_in.shape[0]
    h = x.reshape(T, D)
    pos2 = positions.reshape(T).astype(jnp.int32)
    for i in range(depth):
        h = _layer(h, p[i].reshape(T, -1), pos2, B, S, i == depth - 1, norm_mix_w[i], w_in[i],
                   ret_gn_w[i], cmp_pos[i], cmp_k_w1[i], cmp_k_w2[i], cmp_v_w1[i], cmp_v_w2[i],
                   w_out[i], norm_ffn_w[i], ffn_w_up[i], ffn_conv_w[i], ffn_conv_b[i],
                   ffn_w_down[i], ple_w[i], ple_gate_w[i], final_norm_w)
    return h.reshape(B, S, D)
```
